```python
import math
import jax, jax.numpy as jnp
from jax import lax
import numpy as np

D_MODEL = 2048
BATCH = 2
SEQ = 4096
DEPTH = 4
DEC_BATCH = 32
DEC_SEQ = 64
PAST_LEN = 4096

CHUNK = 64
N_MIXERS = 4
N_REPEAT = DEPTH // N_MIXERS
EPS = 1e-6
N_HEADS = 16
N_KV_HEADS = 4
HEAD_DIM = 128
GQA_GROUP = N_HEADS // N_KV_HEADS
ROT_DIM = HEAD_DIM // 4
ROPE_THETA = 500000.0
ATTN_SCALE = HEAD_DIM ** -0.5
N_IDX_HEADS = 16
IDX_DIM = 64
IDX_ROT_DIM = IDX_DIM // 4
TOPK_MAX = 256
Q_BLOCK = 16
Q_W = N_HEADS * HEAD_DIM
KV_W = N_KV_HEADS * HEAD_DIM
IQ_W = N_IDX_HEADS * IDX_DIM
ATTN_SPLITS = (Q_W, Q_W + KV_W, Q_W + 2 * KV_W, Q_W + 2 * KV_W + IQ_W, Q_W + 2 * KV_W + IQ_W + IDX_DIM)
ATTN_IN = ATTN_SPLITS[-1] + N_IDX_HEADS
LRU_WIDTH = D_MODEL
LRU_BLOCKS = 16
LRU_BLOCK = LRU_WIDTH // LRU_BLOCKS
LRU_CONV = 4
LRU_C = 8.0
S5_GROUP = 16
S5_GROUPS = D_MODEL // S5_GROUP
S5_STATE = 64
SCONV_W = 3
D_FF = 4 * D_MODEL

kernel_name = 'hybrid_chunk_streaming_encoder_step'


def rmsnorm(x, g):
    xf = x.astype(jnp.float32)
    y = xf * lax.rsqrt(jnp.mean(xf * xf, axis=-1, keepdims=True) + EPS)
    return (y * g.astype(jnp.float32)).astype(x.dtype)


def partial_rope(x, pos, rot_dim):
    half = rot_dim // 2
    inv = ROPE_THETA ** (-jnp.arange(half, dtype=jnp.float32) / half)
    ang = pos.astype(jnp.float32)[:, None] * inv[None, :]
    ang = ang.reshape(ang.shape[:1] + (1,) * (x.ndim - 3) + ang.shape[1:])
    cos, sin = jnp.cos(ang), jnp.sin(ang)
    xf = x.astype(jnp.float32)
    x1, x2, rest = xf[..., :half], xf[..., half:rot_dim], xf[..., rot_dim:]
    out = jnp.concatenate([x1 * cos - x2 * sin, x2 * cos + x1 * sin, rest], axis=-1)
    return out.astype(x.dtype)


def causal_conv(x, buf, w):
    width = w.shape[0]
    T = x.shape[1]
    xp = jnp.concatenate([buf.astype(x.dtype), x], axis=1)
    y = xp[:, 0:T] * w[0]
    for j in range(1, width):
        y = y + xp[:, j:j + T] * w[j]
    return y, xp[:, T:]


def linear_combine(c1, c2):
    a1, b1 = c1
    a2, b2 = c2
    return a1 * a2, a2 * b1 + b2


def dsa_attention(h, offset, past, w_in, g_q, g_k, g_ki, w_o):
    f32 = jnp.float32
    B, T, _ = h.shape
    pos = offset + jnp.arange(T)
    q, k, v, qi, ki, wi = jnp.split(h @ w_in, ATTN_SPLITS, axis=-1)
    q = partial_rope(rmsnorm(q.reshape(B, T, N_HEADS, HEAD_DIM), g_q), pos, ROT_DIM)
    k = partial_rope(rmsnorm(k.reshape(B, T, N_KV_HEADS, HEAD_DIM), g_k), pos, ROT_DIM)
    v = v.reshape(B, T, N_KV_HEADS, HEAD_DIM)
    qi = partial_rope(qi.reshape(B, T, N_IDX_HEADS, IDX_DIM), pos, IDX_ROT_DIM)
    ki = partial_rope(rmsnorm(ki, g_ki), pos, IDX_ROT_DIM)
    if past is None:
        k_all, v_all, ki_all = k, v, ki
    else:
        k_all = jnp.concatenate([past[0].astype(k.dtype), k], axis=1)
        v_all = jnp.concatenate([past[1].astype(v.dtype), v], axis=1)
        ki_all = jnp.concatenate([past[2].astype(ki.dtype), ki], axis=1)
    L = offset + T
    n_sel = min(TOPK_MAX, L // 4)
    key_chunk = jnp.arange(L) // CHUNK
    ki_f = ki_all.astype(f32)
    QB = math.gcd(T, Q_BLOCK)
    nb = T // QB

    def to_blocks(a):
        return a.reshape((B, nb, QB) + a.shape[2:]).swapaxes(0, 1)

    def attend_block(args):
        qb, qib, wib, qpos = args
        qchunk = qpos // CHUNK
        s = jnp.einsum('bqhd,bsd->bqhs', qib.astype(f32), ki_f)
        score = jnp.einsum('bqhs,bqh->bqs', jax.nn.relu(s), wib.astype(f32))
        adm = key_chunk[None, :] <= qchunk[:, None]
        score = jnp.where(adm[None], score, -jnp.inf)
        _, idx = lax.top_k(score, n_sel)
        k_sel = jax.vmap(lambda a, j: a[j])(k_all, idx).astype(f32)
        v_sel = jax.vmap(lambda a, j: a[j])(v_all, idx).astype(f32)
        valid = key_chunk[idx] <= qchunk[None, :, None]
        qg = qb.reshape(B, QB, N_KV_HEADS, GQA_GROUP, HEAD_DIM).astype(f32)
        logits = jnp.einsum('bqhgd,bqkhd->bqhgk', qg, k_sel) * ATTN_SCALE
        logits = jnp.where(valid[:, :, None, None, :], logits, -jnp.inf)
        p = jax.nn.softmax(logits, axis=-1)
        o = jnp.einsum('bqhgk,bqkhd->bqhgd', p, v_sel)
        return o.reshape(B, QB, Q_W)

    o = lax.map(attend_block, (to_blocks(q), to_blocks(qi), to_blocks(wi), pos.reshape(nb, QB)))
    o = o.swapaxes(0, 1).reshape(B, T, Q_W).astype(h.dtype)
    return o @ w_o, k, v, ki


def rglru_block(x, conv_buf, h0, w_in, conv_w, conv_b, w_a, b_a, w_x, b_x, lam, w_out):
    f32 = jnp.float32
    B, T, _ = x.shape
    gate, xb = jnp.split(x @ w_in, 2, axis=-1)
    xc, new_buf = causal_conv(xb, conv_buf, conv_w)
    xf = (xc + conv_b).astype(f32)
    xblk = xf.reshape(B, T, LRU_BLOCKS, LRU_BLOCK)
    r = jax.nn.sigmoid(jnp.einsum('btnd,nde->btne', xblk, w_a.astype(f32)).reshape(B, T, LRU_WIDTH) + b_a.astype(f32))
    i = jax.nn.sigmoid(jnp.einsum('btnd,nde->btne', xblk, w_x.astype(f32)).reshape(B, T, LRU_WIDTH) + b_x.astype(f32))
    log_a = -LRU_C * r * jax.nn.softplus(-lam.astype(f32))
    a = jnp.exp(log_a)
    b = jnp.sqrt(-jnp.expm1(2.0 * log_a)) * (i * xf)
    b = b.at[:, 0].add(a[:, 0] * h0.astype(f32))
    _, hs = lax.associative_scan(linear_combine, (a, b), axis=1)
    y = (hs * jax.nn.gelu(gate.astype(f32))).astype(x.dtype) @ w_out
    return y, new_buf, hs[:, -1]


def s5_block(x, s_re, s_im, a_re, a_im, log_dt, b_re, b_im, c_re, c_im, d_skip, w_glu):
    f32 = jnp.float32
    B, T, D = x.shape
    u = x.astype(f32).reshape(B, T, S5_GROUPS, S5_GROUP)
    A = lax.complex(a_re.astype(f32), a_im.astype(f32))
    step = jnp.exp(log_dt.astype(f32))[:, None]
    A_bar = jnp.exp(A * step)
    B_bar = ((A_bar - 1.0) / A)[:, :, None] * lax.complex(b_re.astype(f32), b_im.astype(f32))
    bu = lax.complex(jnp.einsum('btgc,gpc->btgp', u, jnp.real(B_bar)), jnp.einsum('btgc,gpc->btgp', u, jnp.imag(B_bar)))
    bu = bu.at[:, 0].add(A_bar * lax.complex(s_re.astype(f32), s_im.astype(f32)))
    a_seq = jnp.broadcast_to(A_bar, bu.shape)
    _, hs = lax.associative_scan(linear_combine, (a_seq, bu), axis=1)
    y = jnp.einsum('btgp,gcp->btgc', jnp.real(hs), c_re.astype(f32)) - jnp.einsum('btgp,gcp->btgc', jnp.imag(hs), c_im.astype(f32))
    y = y.reshape(B, T, D) + d_skip.astype(f32) * x.astype(f32)
    z = jax.nn.gelu(y).astype(x.dtype) @ w_glu
    out = z[..., :D] * jax.nn.sigmoid(z[..., D:])
    return out.astype(x.dtype), jnp.real(hs[:, -1]), jnp.imag(hs[:, -1])


def short_conv_block(x, buf, w_in, conv_w, w_out):
    b_gate, c_gate, v = jnp.split(x @ w_in, 3, axis=-1)
    z, new_buf = causal_conv(c_gate * v, buf, conv_w)
    return (b_gate * z) @ w_out, new_buf


def sq_relu_mlp(x, w_up, w_down):
    return jnp.square(jax.nn.relu(x @ w_up)) @ w_down


def setup_inputs(seed: int = 0) -> dict:
    key = jax.random.key(seed)
    ks = list(jax.random.split(key, 48))
    f32 = jnp.float32
    R = N_REPEAT
    D = D_MODEL

    def nrm(shape, scale=1.0):
        return jax.random.normal(ks.pop(), shape, f32) * scale

    def gain(shape):
        return 1.0 + 0.02 * jax.random.normal(ks.pop(), shape, f32)

    a_pow = jax.random.uniform(ks.pop(), (R, LRU_WIDTH), f32, 0.9, 0.999)
    sig = a_pow ** (1.0 / LRU_C)
    lru_lambda = jnp.log(sig) - jnp.log1p(-sig)
    log_dt = jax.random.uniform(ks.pop(), (R, S5_GROUPS), f32, math.log(1e-3), math.log(1e-1))
    return {
        'x_prompt': nrm((BATCH, SEQ, D)),
        'x_sample': nrm((DEC_BATCH, DEC_SEQ, D)),
        'cache_attn_k': nrm((R, DEC_BATCH, PAST_LEN, N_KV_HEADS, HEAD_DIM)),
        'cache_attn_v': nrm((R, DEC_BATCH, PAST_LEN, N_KV_HEADS, HEAD_DIM)),
        'cache_idx_k': nrm((R, DEC_BATCH, PAST_LEN, IDX_DIM)),
        'state_lru_conv': nrm((R, DEC_BATCH, LRU_CONV - 1, LRU_WIDTH)),
        'state_lru_h': nrm((R, DEC_BATCH, LRU_WIDTH), 0.5),
        'state_s5_re': nrm((R, DEC_BATCH, S5_GROUPS, S5_STATE), 0.1),
        'state_s5_im': nrm((R, DEC_BATCH, S5_GROUPS, S5_STATE), 0.1),
        'state_sconv': nrm((R, DEC_BATCH, SCONV_W - 1, D), 0.5),
        'norm_mix': gain((DEPTH, D)),
        'norm_ffn': gain((DEPTH, D)),
        'attn_w_in': nrm((R, D, ATTN_IN), D ** -0.5),
        'attn_g_q': gain((R, HEAD_DIM)),
        'attn_g_k': gain((R, HEAD_DIM)),
        'attn_g_ki': gain((R, IDX_DIM)),
        'attn_w_o': nrm((R, Q_W, D), Q_W ** -0.5),
        'lru_w_in': nrm((R, D, 2 * LRU_WIDTH), D ** -0.5),
        'lru_conv_w': nrm((R, LRU_CONV, LRU_WIDTH), LRU_CONV ** -0.5),
        'lru_conv_b': nrm((R, LRU_WIDTH), 0.01),
        'lru_w_a': nrm((R, LRU_BLOCKS, LRU_BLOCK, LRU_BLOCK), LRU_BLOCK ** -0.5),
        'lru_b_a': nrm((R, LRU_WIDTH), 0.01),
        'lru_w_x': nrm((R, LRU_BLOCKS, LRU_BLOCK, LRU_BLOCK), LRU_BLOCK ** -0.5),
        'lru_b_x': nrm((R, LRU_WIDTH), 0.01),
        'lru_lambda': lru_lambda,
        'lru_w_out': nrm((R, LRU_WIDTH, D), LRU_WIDTH ** -0.5),
        's5_a_re': -0.5 + nrm((R, S5_GROUPS, S5_STATE), 0.01),
        's5_a_im': jnp.pi * jnp.arange(S5_STATE, dtype=f32) + nrm((R, S5_GROUPS, S5_STATE), 0.01),
        's5_log_dt': log_dt,
        's5_b_re': nrm((R, S5_GROUPS, S5_STATE, S5_GROUP), (2.0 * S5_GROUP) ** -0.5),
        's5_b_im': nrm((R, S5_GROUPS, S5_STATE, S5_GROUP), (2.0 * S5_GROUP) ** -0.5),
        's5_c_re': nrm((R, S5_GROUPS, S5_GROUP, S5_STATE), S5_STATE ** -0.5),
        's5_c_im': nrm((R, S5_GROUPS, S5_GROUP, S5_STATE), S5_STATE ** -0.5),
        's5_d': nrm((R, D)),
        's5_w_glu': nrm((R, D, 2 * D), D ** -0.5),
        'sc_w_in': nrm((R, D, 3 * D), D ** -0.5),
        'sc_conv_w': nrm((R, SCONV_W, D), SCONV_W ** -0.5),
        'sc_w_out': nrm((R, D, D), D ** -0.5),
        'ffn_w_up': nrm((DEPTH, D, D_FF), D ** -0.5),
        'ffn_w_down': nrm((DEPTH, D_FF, D), D_FF ** -0.5),
    }


def reference(x_prompt, x_sample, cache_attn_k, cache_attn_v, cache_idx_k, state_lru_conv, state_lru_h,
              state_s5_re, state_s5_im, state_sconv, norm_mix, norm_ffn, attn_w_in, attn_g_q, attn_g_k,
              attn_g_ki, attn_w_o, lru_w_in, lru_conv_w, lru_conv_b, lru_w_a, lru_b_a, lru_w_x, lru_b_x,
              lru_lambda, lru_w_out, s5_a_re, s5_a_im, s5_log_dt, s5_b_re, s5_b_im, s5_c_re, s5_c_im, s5_d,
              s5_w_glu, sc_w_in, sc_conv_w, sc_w_out, ffn_w_up, ffn_w_down):

    def run_group(x, offset, past):
        B = x.shape[0]
        f32 = jnp.float32
        outs = ([], [], [], [], [], [], [], [])
        for i in range(DEPTH):
            m, r = i % N_MIXERS, i // N_MIXERS
            h = rmsnorm(x, norm_mix[i])
            if m == 0:
                kv_past = None if past is None else (past[0][r], past[1][r], past[2][r])
                o, k_new, v_new, ki_new = dsa_attention(h, offset, kv_past, attn_w_in[r], attn_g_q[r],
                                                        attn_g_k[r], attn_g_ki[r], attn_w_o[r])
                outs[0].append(k_new)
                outs[1].append(v_new)
                outs[2].append(ki_new)
            elif m == 1:
                buf = jnp.zeros((B, LRU_CONV - 1, LRU_WIDTH), x.dtype) if past is None else past[3][r]
                h0 = jnp.zeros((B, LRU_WIDTH), f32) if past is None else past[4][r]
                o, nbuf, nh = rglru_block(h, buf, h0, lru_w_in[r], lru_conv_w[r], lru_conv_b[r], lru_w_a[r],
                                          lru_b_a[r], lru_w_x[r], lru_b_x[r], lru_lambda[r], lru_w_out[r])
                outs[3].append(nbuf)
                outs[4].append(nh)
            elif m == 2:
                s_re = jnp.zeros((B, S5_GROUPS, S5_STATE), f32) if past is None else past[5][r]
                s_im = jnp.zeros((B, S5_GROUPS, S5_STATE), f32) if past is None else past[6][r]
                o, n_re, n_im = s5_block(h, s_re, s_im, s5_a_re[r], s5_a_im[r], s5_log_dt[r], s5_b_re[r],
                                         s5_b_im[r], s5_c_re[r], s5_c_im[r], s5_d[r], s5_w_glu[r])
                outs[5].append(n_re)
                outs[6].append(n_im)
            else:
                buf = jnp.zeros((B, SCONV_W - 1, D_MODEL), x.dtype) if past is None else past[7][r]
                o, nbuf = short_conv_block(h, buf, sc_w_in[r], sc_conv_w[r], sc_w_out[r])
                outs[7].append(nbuf)
            x = x + o.astype(x.dtype)
            x = x + sq_relu_mlp(rmsnorm(x, norm_ffn[i]), ffn_w_up[i], ffn_w_down[i]).astype(x.dtype)
        return x, [jnp.stack(l) for l in outs]

    y_prompt, st_p = run_group(x_prompt, 0, None)
    y_sample, st_s = run_group(x_sample, PAST_LEN, (cache_attn_k, cache_attn_v, cache_idx_k, state_lru_conv,
                                                    state_lru_h, state_s5_re, state_s5_im, state_sconv))
    k_p, v_p, idxk_p, lruconv_p, lruh_p, s5re_p, s5im_p, sconv_p = st_p
    k_s, v_s, idxk_s, lruconv_s, lruh_s, s5re_s, s5im_s, sconv_s = st_s
    return (y_prompt, y_sample, k_p, v_p, idxk_p, lruconv_p, lruh_p, s5re_p, s5im_p, sconv_p,
            k_s, v_s, idxk_s, lruconv_s, lruh_s, s5re_s, s5im_s, sconv_s)
```

```python
import functools
import math

import jax
import jax.numpy as jnp
from jax import lax
from jax.experimental import pallas as pl
from jax.experimental.pallas import tpu as pltpu

f32 = jnp.float32
bf16 = jnp.bfloat16
i32 = jnp.int32

LANES = 128
SUBLANES = 8
VMEM_LIMIT = 56 * 1024 * 1024

CHUNK = 64
EPS = 1e-6
N_HEADS = 16
N_KV_HEADS = 4
HEAD_DIM = 128
GQA_GROUP = N_HEADS // N_KV_HEADS
ROT_DIM = HEAD_DIM // 4
ROPE_THETA = 500000.0
ATTN_SCALE = HEAD_DIM ** -0.5
N_IDX_HEADS = 16
IDX_DIM = 64
IDX_ROT_DIM = IDX_DIM // 4
TOPK_MAX = 256
LRU_BLOCK = 128
LRU_CONV = 4
LRU_C = 8.0
S5_GROUP = 16
S5_STATE = 64
SCONV_W = 3
PROMPT_STREAMS = 8

NEG_BIG = -1e30
INT_MIN = -(2 ** 31)
LOG2E = math.log2(math.e)


def _cparams(n_axes):
    return pltpu.CompilerParams(dimension_semantics=("arbitrary",) * n_axes, vmem_limit_bytes=VMEM_LIMIT)


def _pick(n, cands):
    for c in cands:
        if n % c == 0:
            return c
    raise ValueError(f"no tile in {cands} divides {n}")


def _rmsnorm_rows(x, g):
    ms = jnp.mean(x * x, axis=-1, keepdims=True)
    return x * lax.rsqrt(ms + EPS) * g


def _mm_body(*refs, norm, n_w, has_res, glu):
    it = iter(refs)
    x_ref = next(it)
    g_ref = next(it) if norm else None
    w_refs = [next(it) for _ in range(n_w)]
    r_ref = next(it) if has_res else None
    o_ref = next(it)
    h_ref = next(it) if norm else None

    if norm:
        @pl.when(pl.program_id(1) == 0)
        def _():
            h_ref[...] = _rmsnorm_rows(x_ref[...], g_ref[...]).astype(bf16)
        h = h_ref[...]
    else:
        h = x_ref[...]
    acc = jnp.dot(h, w_refs[0][...], preferred_element_type=f32)
    if glu:
        acc = acc * jax.nn.sigmoid(jnp.dot(h, w_refs[1][...], preferred_element_type=f32))
    if has_res:
        acc = acc + r_ref[...]
    o_ref[...] = acc.astype(o_ref.dtype)


def _matmul(x, ws, *, gain=None, residual=None, out_dtype=f32, glu=False, name="mm"):
    n, k = x.shape
    m = ws[0].shape[1]
    norm = gain is not None
    tm = _pick(n, (1024, 512, 256, 128, 64))
    tn = _pick(m, (512, 256, 128))
    in_specs = [pl.BlockSpec((tm, k), lambda i, j: (i, 0))]
    args = [x]
    if norm:
        in_specs.append(pl.BlockSpec((1, k), lambda i, j: (0, 0)))
        args.append(gain.reshape(1, k).astype(f32))
    for w in ws:
        in_specs.append(pl.BlockSpec((k, tn), lambda i, j: (0, j)))
        args.append(w)
    if residual is not None:
        in_specs.append(pl.BlockSpec((tm, tn), lambda i, j: (i, j)))
        args.append(residual)
    body = functools.partial(_mm_body, norm=norm, n_w=len(ws), has_res=residual is not None, glu=glu)
    return pl.pallas_call(
        body,
        grid=(n // tm, m // tn),
        in_specs=in_specs,
        out_specs=pl.BlockSpec((tm, tn), lambda i, j: (i, j)),
        out_shape=jax.ShapeDtypeStruct((n, m), out_dtype),
        scratch_shapes=[pltpu.VMEM((tm, k), bf16)] if norm else [],
        compiler_params=_cparams(2),
        name=name,
    )(*args)


def _ffn_body(x_ref, g_ref, wu_ref, wd_ref, o_ref, h_ref):
    @pl.when(pl.program_id(1) == 0)
    def _():
        x = x_ref[...]
        h_ref[...] = _rmsnorm_rows(x, g_ref[...]).astype(bf16)
        o_ref[...] = x

    u = jnp.dot(h_ref[...], wu_ref[...], preferred_element_type=f32)
    u = jnp.square(jnp.maximum(u, 0.0)).astype(bf16)
    o_ref[...] += jnp.dot(u, wd_ref[...], preferred_element_type=f32)


def _ffn(x, gain, w_up, w_down):
    n, d = x.shape
    ff = w_up.shape[1]
    tm = _pick(n, (512, 256, 128, 64))
    tf = _pick(ff, (1024, 512, 256, 128))
    return pl.pallas_call(
        _ffn_body,
        grid=(n // tm, ff // tf),
        in_specs=[
            pl.BlockSpec((tm, d), lambda i, j: (i, 0)),
            pl.BlockSpec((1, d), lambda i, j: (0, 0)),
            pl.BlockSpec((d, tf), lambda i, j: (0, j)),
            pl.BlockSpec((tf, d), lambda i, j: (j, 0)),
        ],
        out_specs=pl.BlockSpec((tm, d), lambda i, j: (i, 0)),
        out_shape=jax.ShapeDtypeStruct((n, d), f32),
        scratch_shapes=[pltpu.VMEM((tm, d), bf16)],
        compiler_params=_cparams(2),
        name="ffn",
    )(x, gain.reshape(1, d).astype(f32), w_up, w_down)


def _norm_body(x_ref, g_ref, o_ref):
    o_ref[...] = _rmsnorm_rows(x_ref[...], g_ref[...])


def _norm(x, gain):
    n, d = x.shape
    tm = _pick(n, (512, 256, 128, 64))
    return pl.pallas_call(
        _norm_body,
        grid=(n // tm,),
        in_specs=[pl.BlockSpec((tm, d), lambda i: (i, 0)), pl.BlockSpec((1, d), lambda i: (0, 0))],
        out_specs=pl.BlockSpec((tm, d), lambda i: (i, 0)),
        out_shape=jax.ShapeDtypeStruct((n, d), f32),
        compiler_params=_cparams(1),
        name="rmsnorm",
    )(x, gain.reshape(1, d).astype(f32))


def _rope_tables(pos, width, rot):
    half = rot // 2
    inv = ROPE_THETA ** (-jnp.arange(half, dtype=f32) / half)
    ang = pos.astype(f32)[:, None] * inv[None, :]
    cos, sin = jnp.cos(ang), jnp.sin(ang)
    n = pos.shape[0]
    pad = jnp.zeros((n, width - rot), f32)
    zero = jnp.zeros((n, half), f32)
    c = jnp.concatenate([cos, cos, pad + 1.0], axis=1)
    s_up = jnp.concatenate([-sin, zero, pad], axis=1)
    s_dn = jnp.concatenate([zero, sin, pad], axis=1)
    reps = LANES // width
    return tuple(jnp.tile(t, (1, reps)) for t in (c, s_up, s_dn))


def _rope(x, c, s_up, s_dn, half):
    return x * c + pltpu.roll(x, LANES - half, axis=1) * s_up + pltpu.roll(x, half, axis=1) * s_dn


def _attn_post_body(q_ref, k_ref, qi_ref, kw_ref, c_ref, su_ref, sd_ref, ci_ref, sui_ref, sdi_ref,
                    gq_ref, gk_ref, gki_ref, qo_ref, ko_ref, qio_ref, kio_ref):
    c, su, sd = c_ref[...], su_ref[...], sd_ref[...]
    ci, sui, sdi = ci_ref[...], sui_ref[...], sdi_ref[...]
    for h in range(N_HEADS):
        sl = slice(h * HEAD_DIM, (h + 1) * HEAD_DIM)
        y = _rope(_rmsnorm_rows(q_ref[:, sl], gq_ref[...]), c, su, sd, ROT_DIM // 2)
        qo_ref[:, sl] = y.astype(qo_ref.dtype)
    for h in range(N_KV_HEADS):
        sl = slice(h * HEAD_DIM, (h + 1) * HEAD_DIM)
        ko_ref[:, sl] = _rope(_rmsnorm_rows(k_ref[:, sl], gk_ref[...]), c, su, sd, ROT_DIM // 2)
    for j in range(N_IDX_HEADS // 2):
        y = _rope(qi_ref[:, j * LANES:(j + 1) * LANES], ci, sui, sdi, IDX_ROT_DIM // 2).astype(qio_ref.dtype)
        qio_ref[2 * j] = y[:, :IDX_DIM]
        qio_ref[2 * j + 1] = y[:, IDX_DIM:]
    kw = kw_ref[:, :LANES]
    lane = lax.broadcasted_iota(i32, kw.shape, 1)
    ms = jnp.sum(jnp.where(lane < IDX_DIM, kw * kw, 0.0), axis=-1, keepdims=True) * (1.0 / IDX_DIM)
    y = _rope(kw * lax.rsqrt(ms + EPS) * gki_ref[...], ci, sui, sdi, IDX_ROT_DIM // 2)
    kio_ref[...] = y[:, :IDX_DIM]


def _attn_post(proj, pos, g_q, g_k, g_ki):
    n = proj.shape[0]
    tm = _pick(n, (256, 128, 64))
    q_w, kv_w, iq_w = N_HEADS * HEAD_DIM, N_KV_HEADS * HEAD_DIM, N_IDX_HEADS * IDX_DIM
    tabs = _rope_tables(pos, HEAD_DIM, ROT_DIM) + _rope_tables(pos, IDX_DIM, IDX_ROT_DIM)
    gki = jnp.concatenate([g_ki.astype(f32), jnp.zeros((LANES - IDX_DIM,), f32)]).reshape(1, LANES)
    row = lambda i: (i, 0)
    const = lambda i: (0, 0)
    in_specs = [
        pl.BlockSpec((tm, q_w), row),
        pl.BlockSpec((tm, kv_w), lambda i: (i, q_w // kv_w)),
        pl.BlockSpec((tm, iq_w), lambda i: (i, (q_w + 2 * kv_w) // iq_w)),
        pl.BlockSpec((tm, kv_w), lambda i: (i, (q_w + 2 * kv_w + iq_w) // kv_w)),
    ] + [pl.BlockSpec((tm, LANES), row)] * 6 + [pl.BlockSpec((1, LANES), const)] * 3
    out_shape = (
        jax.ShapeDtypeStruct((n, q_w), bf16),
        jax.ShapeDtypeStruct((n, kv_w), f32),
        jax.ShapeDtypeStruct((N_IDX_HEADS, n, IDX_DIM), bf16),
        jax.ShapeDtypeStruct((n, IDX_DIM), f32),
    )
    out_specs = (
        pl.BlockSpec((tm, q_w), row),
        pl.BlockSpec((tm, kv_w), row),
        pl.BlockSpec((N_IDX_HEADS, tm, IDX_DIM), lambda i: (0, i, 0)),
        pl.BlockSpec((tm, IDX_DIM), row),
    )
    return pl.pallas_call(
        _attn_post_body,
        grid=(n // tm,),
        in_specs=in_specs,
        out_specs=out_specs,
        out_shape=out_shape,
        compiler_params=_cparams(1),
        name="attn_post",
    )(proj, proj, proj, proj, *tabs, g_q.reshape(1, -1).astype(f32), g_k.reshape(1, -1).astype(f32), gki)


def _sortable(x):
    b = pltpu.bitcast(x, i32)
    return b ^ ((b >> 31) & 0x7FFFFFFF)


def _attn_core_body(*refs, seg_lens, tq, n_sel, causal, q_pos0):
    n_seg = len(seg_lens)
    it = iter(refs)
    qi_ref, wi_ref, q_ref = next(it), next(it), next(it)
    seg_refs = [(next(it), next(it), next(it)) for _ in range(n_seg)]
    o_ref = next(it)
    keys_ref, bias_ref, m_ref, l_ref, acc_ref = (next(it) for _ in range(5))

    l_tot = sum(seg_lens)
    offs = [sum(seg_lens[:s]) for s in range(n_seg)]
    dup = LANES // tq
    wq = GQA_GROUP * tq
    q_tile = pl.program_id(1)

    @pl.when(pl.program_id(2) == 0)
    def _select():
        wi_rows = [wi_ref[h:h + 1, :] for h in range(N_IDX_HEADS)]
        qi_heads = []
        for h in range(N_IDX_HEADS):
            qh = qi_ref[h]
            qi_heads.append(jnp.concatenate([qh] * dup, axis=0) if dup > 1 else qh)

        for s in range(n_seg):
            ki_ref = seg_refs[s][0]
            lc = _pick(seg_lens[s], (256, 128, 64))

            def score_chunk(c, carry, ki_ref=ki_ref, lc=lc, off=offs[s]):
                r0 = pl.multiple_of(c * lc, lc)
                kic = ki_ref[pl.ds(r0, lc), :].astype(bf16)
                acc = jnp.zeros((lc, LANES), f32)
                for h in range(N_IDX_HEADS):
                    sc = lax.dot_general(kic, qi_heads[h], (((1,), (1,)), ((), ())), preferred_element_type=f32)
                    acc = acc + jnp.maximum(sc, 0.0) * wi_rows[h]
                key = _sortable(acc)
                if causal:
                    kpos = r0 + lax.broadcasted_iota(i32, (lc, LANES), 0)
                    qpos = q_pos0 + q_tile * tq + lax.broadcasted_iota(i32, (lc, LANES), 1)
                    key = jnp.where((kpos // CHUNK) <= (qpos // CHUNK), key, INT_MIN)
                keys_ref[pl.ds(pl.multiple_of(off + r0, CHUNK), lc), :] = key
                return carry

            lax.fori_loop(0, seg_lens[s] // lc, score_chunk, 0)

        blk = 64
        n_blk = l_tot // blk

        def count(pred):
            def body(i, acc):
                r0 = pl.multiple_of(i * blk, blk)
                hit = jnp.where(pred(keys_ref[pl.ds(r0, blk), :], r0), 1, 0)
                return acc + jnp.sum(hit.reshape(blk // SUBLANES, SUBLANES, LANES), axis=0)
            acc = lax.fori_loop(0, n_blk, body, jnp.zeros((SUBLANES, LANES), i32))
            return jnp.sum(acc, axis=0, keepdims=True)

        def count_ge(cand):
            return count(lambda k, r0: k >= cand)

        ans = jnp.where(count_ge(jnp.zeros((1, LANES), i32)) >= n_sel, 0, INT_MIN)

        def bit_step(i, ans):
            cand = ans | (1 << (30 - i))
            return jnp.where(count_ge(cand) >= n_sel, cand, ans)

        ans = lax.fori_loop(0, 31, bit_step, ans)
        tau = jnp.maximum(ans, INT_MIN + 1)
        n_ge = count_ge(tau)
        ties = jnp.max(n_ge) > n_sel

        @pl.when(jnp.logical_not(ties))
        def _():
            def body(i, carry):
                r0 = pl.multiple_of(i * blk, blk)
                bias_ref[pl.ds(r0, blk), :] = jnp.where(keys_ref[pl.ds(r0, blk), :] >= tau, 0.0, NEG_BIG)
                return carry
            lax.fori_loop(0, n_blk, body, 0)

        @pl.when(ties)
        def _():
            def row_idx(r0):
                return r0 + lax.broadcasted_iota(i32, (blk, LANES), 0)
            need = n_sel - count(lambda k, r0: k > tau)

            def idx_step(i, m):
                cand = m | (1 << (14 - i))
                n_below = count(lambda k, r0: jnp.logical_and(k == tau, row_idx(r0) < cand))
                return jnp.where(n_below <= need, cand, m)

            m_cut = lax.fori_loop(0, 15, idx_step, jnp.zeros((1, LANES), i32))

            def body(i, carry):
                r0 = pl.multiple_of(i * blk, blk)
                k = keys_ref[pl.ds(r0, blk), :]
                keep = jnp.logical_or(k > tau, jnp.logical_and(k == tau, row_idx(r0) < m_cut))
                bias_ref[pl.ds(r0, blk), :] = jnp.where(keep, 0.0, NEG_BIG)
                return carry
            lax.fori_loop(0, n_blk, body, 0)

    qg = jnp.concatenate([q_ref[:, j * HEAD_DIM:(j + 1) * HEAD_DIM] for j in range(GQA_GROUP)], axis=0)
    m_ref[...] = jnp.full(m_ref.shape, NEG_BIG, f32)
    l_ref[...] = jnp.zeros(l_ref.shape, f32)
    acc_ref[...] = jnp.zeros(acc_ref.shape, f32)
    c_exp = ATTN_SCALE * LOG2E

    for s in range(n_seg):
        _, k_ref, v_ref = seg_refs[s]
        lc = _pick(seg_lens[s], (256, 128, 64))

        def attend_chunk(c, carry, k_ref=k_ref, v_ref=v_ref, lc=lc, off=offs[s]):
            r0 = pl.multiple_of(c * lc, lc)
            kc = k_ref[pl.ds(r0, lc), :].astype(bf16)
            vc = v_ref[pl.ds(r0, lc), :].astype(bf16)
            sc = lax.dot_general(kc, qg, (((1,), (1,)), ((), ())), preferred_element_type=f32)
            bias = bias_ref[pl.ds(pl.multiple_of(off + r0, CHUNK), lc), :]
            sc = sc + jnp.concatenate([bias] * (wq // LANES), axis=1)
            m_old = m_ref[...]
            m_new = jnp.maximum(m_old, jnp.max(sc, axis=0, keepdims=True))
            alpha = jnp.exp2((m_old - m_new) * c_exp)
            p = jnp.exp2((sc - m_new) * c_exp)
            l_ref[...] = alpha * l_ref[...] + jnp.sum(p, axis=0, keepdims=True)
            pv = lax.dot_general(vc, p.astype(bf16), (((0,), (0,)), ((), ())), preferred_element_type=f32)
            acc_ref[...] = alpha * acc_ref[...] + pv
            m_ref[...] = m_new
            return carry

        lax.fori_loop(0, seg_lens[s] // lc, attend_chunk, 0)

    o_t = (acc_ref[...] / l_ref[...]).T
    for j in range(GQA_GROUP):
        o_ref[:, j * HEAD_DIM:(j + 1) * HEAD_DIM] = o_t[j * tq:(j + 1) * tq, :].astype(o_ref.dtype)


def _attn_core(q, qi, wi_t, segs, *, n_batch, t_seq, row0, q_lo, q_hi, n_sel, causal):
    tq = min(LANES, t_seq)
    nq = (q_hi - q_lo) // tq
    wq = GQA_GROUP * tq
    kv_w = GQA_GROUP * HEAD_DIM
    qrow = lambda b, i, g: (row0 // tq + b * (t_seq // tq) + q_lo // tq + i)
    in_specs = [
        pl.BlockSpec((N_IDX_HEADS, tq, IDX_DIM), lambda b, i, g: (0, qrow(b, i, g), 0)),
        pl.BlockSpec((None, N_IDX_HEADS, LANES), lambda b, i, g: (b, 0, (q_lo // tq + i) * tq // LANES)),
        pl.BlockSpec((tq, kv_w), lambda b, i, g: (qrow(b, i, g), g)),
    ]
    args = [qi, wi_t, q]
    seg_lens = []
    for ki, k, v, n_keys, imap in segs:
        lead = (None,) * (ki.ndim - 2)
        in_specs += [
            pl.BlockSpec(lead + (n_keys, IDX_DIM), functools.partial(imap, head=None)),
            pl.BlockSpec(lead + (n_keys, HEAD_DIM), functools.partial(imap, head=True)),
            pl.BlockSpec(lead + (n_keys, HEAD_DIM), functools.partial(imap, head=True)),
        ]
        args += [ki, k, v]
        seg_lens.append(n_keys)
    l_tot = sum(seg_lens)
    body = functools.partial(_attn_core_body, seg_lens=tuple(seg_lens), tq=tq, n_sel=n_sel, causal=causal, q_pos0=q_lo)
    return pl.pallas_call(
        body,
        grid=(n_batch, nq, N_KV_HEADS),
        in_specs=in_specs,
        out_specs=pl.BlockSpec((tq, kv_w), lambda b, i, g: (b * nq + i, g)),
        out_shape=jax.ShapeDtypeStruct((n_batch * (q_hi - q_lo), N_HEADS * HEAD_DIM), bf16),
        scratch_shapes=[
            pltpu.VMEM((l_tot, LANES), i32),
            pltpu.VMEM((l_tot, LANES), f32),
            pltpu.VMEM((1, wq), f32),
            pltpu.VMEM((1, wq), f32),
            pltpu.VMEM((HEAD_DIM, wq), f32),
        ],
        compiler_params=_cparams(3),
        name="attn_core",
    )(*args)


def _attn_group(q, qi, wi, ki_new, k_new, v_new, past, *, n_batch, t_seq, row0, offset):
    n_rows = n_batch * t_seq
    wi_g = wi[row0:row0 + n_rows].reshape(n_batch, t_seq, N_IDX_HEADS).transpose(0, 2, 1)
    if t_seq < LANES:
        wi_g = jnp.tile(wi_g, (1, 1, LANES // t_seq))
    ki_g = ki_new[row0:row0 + n_rows].reshape(n_batch, t_seq, IDX_DIM)
    k_g = k_new[row0:row0 + n_rows].reshape(n_batch, t_seq, N_KV_HEADS * HEAD_DIM)
    v_g = v_new[row0:row0 + n_rows].reshape(n_batch, t_seq, N_KV_HEADS * HEAD_DIM)
    l_all = offset + t_seq
    n_sel = min(TOPK_MAX, l_all // 4)

    def new_map(b, i, g, head):
        return (b, 0, g if head else 0)

    if past is None:
        outs = []
        bands = [(0, t_seq // 4), (t_seq // 4, t_seq // 2), (t_seq // 2, t_seq)] if t_seq >= 4 * LANES else [(0, t_seq)]
        for q_lo, q_hi in bands:
            segs = [(ki_g, k_g, v_g, q_hi, new_map)]
            o = _attn_core(q, qi, wi_g, segs, n_batch=n_batch, t_seq=t_seq, row0=row0, q_lo=q_lo, q_hi=q_hi,
                           n_sel=n_sel, causal=True)
            outs.append(o.reshape(n_batch, q_hi - q_lo, -1))
        return jnp.concatenate(outs, axis=1).reshape(n_rows, -1)

    assert offset % CHUNK == 0 and t_seq <= CHUNK
    pk, pv, pki, r = past
    n_past = pk.shape[2]
    pk = pk.reshape(pk.shape[0], n_batch, n_past, N_KV_HEADS * HEAD_DIM)
    pv = pv.reshape(pv.shape[0], n_batch, n_past, N_KV_HEADS * HEAD_DIM)

    def past_map(b, i, g, head):
        return (r, b, 0, g if head else 0)

    segs = [(pki, pk, pv, n_past, past_map), (ki_g, k_g, v_g, t_seq, new_map)]
    return _attn_core(q, qi, wi_g, segs, n_batch=n_batch, t_seq=t_seq, row0=row0, q_lo=0, q_hi=t_seq,
                      n_sel=n_sel, causal=False)


def _attn_layer(x, groups, past, gain, w_in, g_q, g_k, g_ki, w_o):
    q_w, kv_w, iq_w = N_HEADS * HEAD_DIM, N_KV_HEADS * HEAD_DIM, N_IDX_HEADS * IDX_DIM
    n_in = w_in.shape[1]
    n_pad = -(-n_in // 512) * 512
    w_in_p = jnp.pad(w_in, ((0, 0), (0, n_pad - n_in))).astype(bf16)
    proj = _matmul(x, [w_in_p], gain=gain, name="attn_in")
    pos = jnp.concatenate([off + jnp.tile(jnp.arange(t), b) for (row0, b, t, off) in groups])
    q, k, qi, ki = _attn_post(proj, pos, g_q, g_k, g_ki)
    v = proj[:, q_w + kv_w:q_w + 2 * kv_w]
    wi = proj[:, q_w + 2 * kv_w + iq_w + IDX_DIM:q_w + 2 * kv_w + iq_w + IDX_DIM + N_IDX_HEADS]
    outs = []
    for gi, (row0, b, t, off) in enumerate(groups):
        outs.append(_attn_group(q, qi, wi, ki, k, v, past[gi], n_batch=b, t_seq=t, row0=row0, offset=off))
    o = jnp.concatenate(outs, axis=0)
    x = _matmul(o, [w_o.astype(bf16)], residual=x, name="attn_out")
    return x, k, v, ki


def _halo_rows(x_ref, init, j, *, n_str, n_steps, chained):
    if not chained:
        return init
    prev = x_ref[(n_steps - j) * n_str:(n_steps - j + 1) * n_str, :]
    shifted = pltpu.roll(prev, 1, axis=0)
    first = lax.broadcasted_iota(i32, shifted.shape, 0) == 0
    return jnp.where(first, init, shifted)


def _chain_carries(h_end, p_end, c0):
    rows = [c0]
    for s in range(1, h_end.shape[0]):
        rows.append(p_end[s - 1:s] * rows[-1] + h_end[s - 1:s])
    return jnp.concatenate(rows, axis=0)


def _lru_core_body(xb_ref, gate_ref, cw_ref, cb_ref, wa_ref, ba_ref, wx_ref, bx_ref, lam_ref, buf_ref, h0_ref,
                   y_ref, hl_ref, a_ref, b_ref, *, n_str, n_steps, chained, lw):
    rows = n_str * n_steps
    ch = _pick(n_steps, (16, 8, 4, 2, 1)) * n_str
    n_halo = LRU_CONV - 1
    cw = [cw_ref[j:j + 1, :] for j in range(LRU_CONV)]
    cb = cb_ref[...]
    sp = jax.nn.softplus(-lam_ref[...])
    halo = jnp.concatenate(
        [_halo_rows(xb_ref, buf_ref[n_halo - j], j, n_str=n_str, n_steps=n_steps, chained=chained)
         for j in range(n_halo, 0, -1)], axis=0)

    def gates(xin, r0):
        xc = cb
        for j in range(LRU_CONV):
            xc = xc + xin[j * n_str:j * n_str + ch, :] * cw[j]
        xcb = xc.astype(bf16)
        for t in range(lw // LRU_BLOCK):
            sl = slice(t * LRU_BLOCK, (t + 1) * LRU_BLOCK)
            xt = xcb[:, sl]
            r = jax.nn.sigmoid(jnp.dot(xt, wa_ref[t], preferred_element_type=f32) + ba_ref[:, sl])
            ig = jax.nn.sigmoid(jnp.dot(xt, wx_ref[t], preferred_element_type=f32) + bx_ref[:, sl])
            log_a = (-LRU_C) * r * sp[:, sl]
            th = jnp.tanh(log_a)
            mult = jnp.sqrt(-2.0 * th / (1.0 - th))
            a_ref[pl.ds(r0, ch), sl] = jnp.exp(log_a)
            b_ref[pl.ds(r0, ch), sl] = mult * (ig * xc[:, sl])

    gates(jnp.concatenate([halo, xb_ref[0:ch, :]], axis=0), 0)

    def gate_chunk(c, carry):
        r0 = pl.multiple_of(c * ch, ch)
        gates(xb_ref[pl.ds(r0 - n_halo * n_str, ch + n_halo * n_str), :], r0)
        return carry

    lax.fori_loop(1, rows // ch, gate_chunk, 0)

    def scan_step(t, carry):
        r0 = pl.multiple_of(t * n_str, n_str)
        a = a_ref[pl.ds(r0, n_str), :]
        h = a * carry[0] + b_ref[pl.ds(r0, n_str), :]
        b_ref[pl.ds(r0, n_str), :] = h
        if not chained:
            return (h,)
        p = a * carry[1]
        a_ref[pl.ds(r0, n_str), :] = p
        return (h, p)

    if chained:
        init = (jnp.zeros((n_str, lw), f32), jnp.ones((n_str, lw), f32))
    else:
        init = (h0_ref[...],)
    fin = lax.fori_loop(0, n_steps, scan_step, init, unroll=8)

    if chained:
        carries = _chain_carries(fin[0], fin[1], h0_ref[...])
        c_rep = jnp.concatenate([carries] * (ch // n_str), axis=0)

    def out_chunk(c, carry):
        r0 = pl.multiple_of(c * ch, ch)
        h = b_ref[pl.ds(r0, ch), :]
        if chained:
            h = h + a_ref[pl.ds(r0, ch), :] * c_rep
            b_ref[pl.ds(r0, ch), :] = h
        y_ref[pl.ds(r0, ch), :] = (h * jax.nn.gelu(gate_ref[pl.ds(r0, ch), :])).astype(y_ref.dtype)
        return carry

    lax.fori_loop(0, rows // ch, out_chunk, 0)
    if chained:
        hl_ref[...] = b_ref[rows - 1:rows, :]
    else:
        hl_ref[...] = fin[0]


def _lru_core(proj, params, buf, h0, *, row0, n_seq, n_str, n_steps, chained):
    w = proj.shape[1] // 2
    lw = 256
    rows = n_str * n_steps
    rb0 = row0 // rows
    ns = 1 if chained else n_str
    cw, cb, wa, ba, wx, bx, lam = params
    nb = lw // LRU_BLOCK
    vec = lambda s, n: (0, n)
    in_specs = [
        pl.BlockSpec((rows, lw), lambda s, n: (rb0 + s, w // lw + n)),
        pl.BlockSpec((rows, lw), lambda s, n: (rb0 + s, n)),
        pl.BlockSpec((LRU_CONV, lw), vec),
        pl.BlockSpec((1, lw), vec),
        pl.BlockSpec((nb, LRU_BLOCK, LRU_BLOCK), lambda s, n: (n, 0, 0)),
        pl.BlockSpec((1, lw), vec),
        pl.BlockSpec((nb, LRU_BLOCK, LRU_BLOCK), lambda s, n: (n, 0, 0)),
        pl.BlockSpec((1, lw), vec),
        pl.BlockSpec((1, lw), vec),
        pl.BlockSpec((None, LRU_CONV - 1, ns, lw), lambda s, n: (s, 0, 0, n)),
        pl.BlockSpec((None, ns, lw), lambda s, n: (s, 0, n)),
    ]
    body = functools.partial(_lru_core_body, n_str=n_str, n_steps=n_steps, chained=chained, lw=lw)
    return pl.pallas_call(
        body,
        grid=(n_seq, w // lw),
        in_specs=in_specs,
        out_specs=(
            pl.BlockSpec((rows, lw), lambda s, n: (s, n)),
            pl.BlockSpec((None, ns, lw), lambda s, n: (s, 0, n)),
        ),
        out_shape=(
            jax.ShapeDtypeStruct((n_seq * rows, w), bf16),
            jax.ShapeDtypeStruct((n_seq, ns, w), f32),
        ),
        scratch_shapes=[pltpu.VMEM((rows, lw), f32), pltpu.VMEM((rows, lw), f32)],
        compiler_params=_cparams(2),
        name="lru_core",
    )(proj, proj, cw, cb.reshape(1, w), wa.astype(bf16), ba.reshape(1, w), wx.astype(bf16), bx.reshape(1, w),
      lam.reshape(1, w), buf, h0)


def _s5_core_body(u_ref, bm_ref, cm_ref, are_ref, aim_ref, d_ref, sre_ref, sim_ref,
                  y_ref, ore_ref, oim_ref, hs_ref, p_ref, *, n_str, n_steps, chained):
    rows = n_str * n_steps
    half = hs_ref.shape[1] // 2
    ch = _pick(n_steps, (32, 16, 8, 4, 2, 1)) * n_str

    def proj_chunk(c, carry):
        r0 = pl.multiple_of(c * ch, ch)
        hs_ref[pl.ds(r0, ch), :] = jnp.dot(u_ref[pl.ds(r0, ch), :].astype(bf16), bm_ref[...], preferred_element_type=f32)
        return carry

    lax.fori_loop(0, rows // ch, proj_chunk, 0)

    a_re = jnp.broadcast_to(are_ref[...], (n_str, half))
    a_im = jnp.broadcast_to(aim_ref[...], (n_str, half))

    def scan_step(t, carry):
        r0 = pl.multiple_of(t * n_str, n_str)
        h_re, h_im = carry[0], carry[1]
        n_re = a_re * h_re - a_im * h_im + hs_ref[pl.ds(r0, n_str), :half]
        n_im = a_re * h_im + a_im * h_re + hs_ref[pl.ds(r0, n_str), half:]
        hs_ref[pl.ds(r0, n_str), :half] = n_re
        hs_ref[pl.ds(r0, n_str), half:] = n_im
        if not chained:
            return (n_re, n_im)
        p_re, p_im = carry[2], carry[3]
        q_re = a_re[:1] * p_re - a_im[:1] * p_im
        q_im = a_re[:1] * p_im + a_im[:1] * p_re
        p_ref[pl.ds(t, 1), :half] = q_re
        p_ref[pl.ds(t, 1), half:] = q_im
        return (n_re, n_im, q_re, q_im)

    if chained:
        zero = jnp.zeros((n_str, half), f32)
        init = (zero, zero, jnp.ones((1, half), f32), jnp.zeros((1, half), f32))
    else:
        init = (sre_ref[...], sim_ref[...])
    fin = lax.fori_loop(0, n_steps, scan_step, init, unroll=2)

    if chained:
        pe_re, pe_im = fin[2], fin[3]
        c_re, c_im = [sre_ref[...]], [sim_ref[...]]
        for s in range(1, n_str):
            c_re.append(pe_re * c_re[-1] - pe_im * c_im[-1] + fin[0][s - 1:s])
            c_im.append(pe_re * c_im[-1] + pe_im * c_re[-2] + fin[1][s - 1:s])
        c_re = jnp.concatenate(c_re, axis=0)
        c_im = jnp.concatenate(c_im, axis=0)

        def fix_step(t, carry):
            r0 = pl.multiple_of(t * n_str, n_str)
            p_re = jnp.broadcast_to(p_ref[pl.ds(t, 1), :half], (n_str, half))
            p_im = jnp.broadcast_to(p_ref[pl.ds(t, 1), half:], (n_str, half))
            hs_ref[pl.ds(r0, n_str), :half] += p_re * c_re - p_im * c_im
            hs_ref[pl.ds(r0, n_str), half:] += p_re * c_im + p_im * c_re
            return carry

        lax.fori_loop(0, n_steps, fix_step, 0, unroll=2)
        ore_ref[...] = hs_ref[rows - 1:rows, :half]
        oim_ref[...] = hs_ref[rows - 1:rows, half:]
    else:
        ore_ref[...] = fin[0]
        oim_ref[...] = fin[1]

    def out_chunk(c, carry):
        r0 = pl.multiple_of(c * ch, ch)
        u = u_ref[pl.ds(r0, ch), :]
        y = jnp.dot(hs_ref[pl.ds(r0, ch), :].astype(bf16), cm_ref[...], preferred_element_type=f32) + d_ref[...] * u
        y_ref[pl.ds(r0, ch), :] = jax.nn.gelu(y).astype(y_ref.dtype)
        return carry

    lax.fori_loop(0, rows // ch, out_chunk, 0)


def _s5_core(u, mats, s_re, s_im, *, row0, n_seq, n_str, n_steps, chained):
    d = u.shape[1]
    bm, cm, a_re, a_im, dsk = mats
    n_gb = d // LANES
    sw = bm.shape[2] // 2
    rows = n_str * n_steps
    rb0 = row0 // rows
    ns = 1 if chained else n_str
    st_spec = pl.BlockSpec((None, ns, sw), lambda s, g: (s, 0, g))
    in_specs = [
        pl.BlockSpec((rows, LANES), lambda s, g: (rb0 + s, g)),
        pl.BlockSpec((None, LANES, 2 * sw), lambda s, g: (g, 0, 0)),
        pl.BlockSpec((None, 2 * sw, LANES), lambda s, g: (g, 0, 0)),
        pl.BlockSpec((1, sw), lambda s, g: (0, g)),
        pl.BlockSpec((1, sw), lambda s, g: (0, g)),
        pl.BlockSpec((1, LANES), lambda s, g: (0, g)),
        st_spec,
        st_spec,
    ]
    body = functools.partial(_s5_core_body, n_str=n_str, n_steps=n_steps, chained=chained)
    st_shape = jax.ShapeDtypeStruct((n_seq, ns, n_gb * sw), f32)
    return pl.pallas_call(
        body,
        grid=(n_seq, n_gb),
        in_specs=in_specs,
        out_specs=(pl.BlockSpec((rows, LANES), lambda s, g: (s, g)), st_spec, st_spec),
        out_shape=(jax.ShapeDtypeStruct((n_seq * rows, d), bf16), st_shape, st_shape),
        scratch_shapes=[pltpu.VMEM((rows, 2 * sw), f32), pltpu.VMEM((n_steps if chained else SUBLANES, 2 * sw), f32)],
        compiler_params=_cparams(2),
        name="s5_core",
    )(u, bm, cm, a_re, a_im, dsk, s_re, s_im)


def _s5_matrices(a_re, a_im, log_dt, b_re, b_im, c_re, c_im, d_skip):
    g, p = a_re.shape
    a = lax.complex(a_re.astype(f32), a_im.astype(f32))
    step = jnp.exp(log_dt.astype(f32))[:, None]
    a_bar = jnp.exp(a * step)
    b_bar = ((a_bar - 1.0) / a)[:, :, None] * lax.complex(b_re.astype(f32), b_im.astype(f32))
    gpb = LANES // S5_GROUP
    n_gb = g // gpb
    eye = jnp.eye(gpb, dtype=f32)

    def in_block(m):
        m = m.reshape(n_gb, gpb, p, S5_GROUP)
        return jnp.einsum('ngpc,gh->ngchp', m, eye).reshape(n_gb, gpb * S5_GROUP, gpb * p)

    def out_block(m):
        m = m.reshape(n_gb, gpb, S5_GROUP, p)
        return jnp.einsum('ngcp,gh->ngphc', m, eye).reshape(n_gb, gpb * p, gpb * S5_GROUP)

    bm = jnp.concatenate([in_block(jnp.real(b_bar)), in_block(jnp.imag(b_bar))], axis=2).astype(bf16)
    cm = jnp.concatenate([out_block(c_re.astype(f32)), -out_block(c_im.astype(f32))], axis=1).astype(bf16)
    return bm, cm, jnp.real(a_bar).reshape(1, g * p), jnp.imag(a_bar).reshape(1, g * p), d_skip.reshape(1, -1).astype(f32)


def _sconv_body(bg_ref, cg_ref, v_ref, cw_ref, buf_ref, u_ref, nb_ref, cv_ref, *, n_str, n_steps, chained):
    rows = n_str * n_steps
    ch = _pick(n_steps, (16, 8, 4, 2, 1)) * n_str
    n_halo = SCONV_W - 1
    cw = [cw_ref[j:j + 1, :] for j in range(SCONV_W)]

    def prod_chunk(c, carry):
        r0 = pl.multiple_of(c * ch, ch)
        cv_ref[pl.ds(r0, ch), :] = cg_ref[pl.ds(r0, ch), :] * v_ref[pl.ds(r0, ch), :]
        return carry

    lax.fori_loop(0, rows // ch, prod_chunk, 0)
    halo = jnp.concatenate(
        [_halo_rows(cv_ref, buf_ref[n_halo - j], j, n_str=n_str, n_steps=n_steps, chained=chained)
         for j in range(n_halo, 0, -1)], axis=0)

    def emit(xin, r0):
        z = xin[0:ch, :] * cw[0]
        for j in range(1, SCONV_W):
            z = z + xin[j * n_str:j * n_str + ch, :] * cw[j]
        u_ref[pl.ds(r0, ch), :] = (bg_ref[pl.ds(r0, ch), :] * z).astype(u_ref.dtype)

    emit(jnp.concatenate([halo, cv_ref[0:ch, :]], axis=0), 0)

    def out_chunk(c, carry):
        r0 = pl.multiple_of(c * ch, ch)
        emit(cv_ref[pl.ds(r0 - n_halo * n_str, ch + n_halo * n_str), :], r0)
        return carry

    lax.fori_loop(1, rows // ch, out_chunk, 0)
    for j in range(n_halo):
        step_rows = cv_ref[(n_steps - n_halo + j) * n_str:(n_steps - n_halo + j + 1) * n_str, :]
        nb_ref[j] = step_rows[n_str - 1:, :] if chained else step_rows


def _sconv_core(proj, conv_w, buf, *, row0, n_seq, n_str, n_steps, chained):
    d = proj.shape[1] // 3
    lw = 256
    rows = n_str * n_steps
    rb0 = row0 // rows
    ns = 1 if chained else n_str
    nl = d // lw
    body = functools.partial(_sconv_body, n_str=n_str, n_steps=n_steps, chained=chained)
    buf_spec = pl.BlockSpec((None, SCONV_W - 1, ns, lw), lambda s, n: (s, 0, 0, n))
    return pl.pallas_call(
        body,
        grid=(n_seq, nl),
        in_specs=[
            pl.BlockSpec((rows, lw), lambda s, n: (rb0 + s, n)),
            pl.BlockSpec((rows, lw), lambda s, n: (rb0 + s, nl + n)),
            pl.BlockSpec((rows, lw), lambda s, n: (rb0 + s, 2 * nl + n)),
            pl.BlockSpec((SCONV_W, lw), lambda s, n: (0, n)),
            buf_spec,
        ],
        out_specs=(pl.BlockSpec((rows, lw), lambda s, n: (s, n)), buf_spec),
        out_shape=(
            jax.ShapeDtypeStruct((n_seq * rows, d), bf16),
            jax.ShapeDtypeStruct((n_seq, SCONV_W - 1, ns, d), f32),
        ),
        scratch_shapes=[pltpu.VMEM((rows, lw), f32)],
        compiler_params=_cparams(2),
        name="sconv_core",
    )(proj, proj, proj, conv_w, buf)


class _Group:
    def __init__(self, row0, n_batch, t_seq, offset, has_past):
        self.row0, self.n_batch, self.t_seq, self.offset, self.has_past = row0, n_batch, t_seq, offset, has_past
        self.chained = not has_past
        if self.chained:
            self.n_seq, self.n_str, self.n_steps = n_batch, PROMPT_STREAMS, t_seq // PROMPT_STREAMS
        else:
            self.n_seq, self.n_str, self.n_steps = 1, n_batch, t_seq
        self.n_rows = n_batch * t_seq

    def to_streams(self, x):
        f = x.shape[-1]
        if self.chained:
            return x.reshape(self.n_batch, self.n_str, self.n_steps, f).transpose(0, 2, 1, 3).reshape(self.n_rows, f)
        return x.reshape(self.n_batch, self.t_seq, f).transpose(1, 0, 2).reshape(self.n_rows, f)

    def to_natural(self, x):
        f = x.shape[-1]
        if self.chained:
            return x.reshape(self.n_batch, self.n_steps, self.n_str, f).transpose(0, 2, 1, 3).reshape(self.n_rows, f)
        return x.reshape(self.t_seq, self.n_batch, f).transpose(1, 0, 2).reshape(self.n_rows, f)

    def last_steps(self, x, k):
        f = x.shape[-1]
        if self.chained:
            return x.reshape(self.n_batch, self.n_steps, self.n_str, f)[:, self.n_steps - k:, self.n_str - 1, :]
        return x.reshape(self.t_seq, self.n_batch, f)[self.t_seq - k:].transpose(1, 0, 2)

    def state_in(self, s, width):
        if s is None:
            return None
        if s.ndim == 2:
            return s.astype(f32)[None]
        return s.astype(f32).transpose(1, 0, 2)[None]

    def zeros_state(self, k, width):
        shape = (self.n_seq, 1, width) if k is None else (self.n_seq, k, 1, width)
        return jnp.zeros(shape, f32)

    def state_out(self, s):
        return s.reshape(self.n_batch, -1)


def _relayout(x, groups, fn):
    return jnp.concatenate([fn(g, x[g.row0:g.row0 + g.n_rows]) for g in groups], axis=0)


def kernel(x_prompt, x_sample, cache_attn_k, cache_attn_v, cache_idx_k, state_lru_conv, state_lru_h, state_s5_re, state_s5_im, state_sconv, norm_mix, norm_ffn, attn_w_in, attn_g_q, attn_g_k, attn_g_ki, attn_w_o, lru_w_in, lru_conv_w, lru_conv_b, lru_w_a, lru_b_a, lru_w_x, lru_b_x, lru_lambda, lru_w_out, s5_a_re, s5_a_im, s5_log_dt, s5_b_re, s5_b_im, s5_c_re, s5_c_im, s5_d, s5_w_glu, sc_w_in, sc_conv_w, sc_w_out, ffn_w_up, ffn_w_down):
    bp, tp, d = x_prompt.shape
    bs, ts, _ = x_sample.shape
    n_past = cache_attn_k.shape[2]
    depth = norm_mix.shape[0]
    n_mix = 4
    groups = [_Group(0, bp, tp, 0, False), _Group(bp * tp, bs, ts, n_past, True)]
    x = jnp.concatenate([x_prompt.reshape(bp * tp, d), x_sample.reshape(bs * ts, d)], axis=0).astype(f32)
    layout = "natural"
    outs = [[[] for _ in range(8)] for _ in groups]

    def want(target, x, layout):
        if layout == target:
            return x
        return _relayout(x, groups, _Group.to_streams if target == "streams" else _Group.to_natural)

    for i in range(depth):
        m, r = i % n_mix, i // n_mix
        if m == 0:
            x = want("natural", x, layout)
            layout = "natural"
            past = [None, (cache_attn_k, cache_attn_v, cache_idx_k, r)]
            att_groups = [(g.row0, g.n_batch, g.t_seq, g.offset) for g in groups]
            x, k, v, ki = _attn_layer(x, att_groups, past, norm_mix[i], attn_w_in[r], attn_g_q[r], attn_g_k[r],
                                      attn_g_ki[r], attn_w_o[r])
            for gi, g in enumerate(groups):
                sl = slice(g.row0, g.row0 + g.n_rows)
                outs[gi][0].append(k[sl].reshape(g.n_batch, g.t_seq, N_KV_HEADS, HEAD_DIM))
                outs[gi][1].append(v[sl].reshape(g.n_batch, g.t_seq, N_KV_HEADS, HEAD_DIM))
                outs[gi][2].append(ki[sl].reshape(g.n_batch, g.t_seq, IDX_DIM))
        else:
            x = want("streams", x, layout)
            layout = "streams"
            if m == 1:
                w = lru_w_in.shape[2] // 2
                proj = _matmul(x, [lru_w_in[r].astype(bf16)], gain=norm_mix[i], name="lru_in")
                params = (lru_conv_w[r], lru_conv_b[r], lru_w_a[r], lru_b_a[r], lru_w_x[r], lru_b_x[r], lru_lambda[r])
                ys = []
                for gi, g in enumerate(groups):
                    if g.has_past:
                        buf = g.state_in(state_lru_conv[r], w)
                        h0 = g.state_in(state_lru_h[r], w)
                    else:
                        buf, h0 = g.zeros_state(LRU_CONV - 1, w), g.zeros_state(None, w)
                    y, hl = _lru_core(proj, params, buf, h0, row0=g.row0, n_seq=g.n_seq, n_str=g.n_str,
                                      n_steps=g.n_steps, chained=g.chained)
                    ys.append(y)
                    outs[gi][3].append(g.last_steps(proj[g.row0:g.row0 + g.n_rows, w:], LRU_CONV - 1))
                    outs[gi][4].append(g.state_out(hl))
                x = _matmul(jnp.concatenate(ys, axis=0), [lru_w_out[r].astype(bf16)], residual=x, name="lru_out")
            elif m == 2:
                u = _norm(x, norm_mix[i])
                mats = _s5_matrices(s5_a_re[r], s5_a_im[r], s5_log_dt[r], s5_b_re[r], s5_b_im[r], s5_c_re[r],
                                    s5_c_im[r], s5_d[r])
                gp = s5_a_re.shape[1] * s5_a_re.shape[2]
                ys = []
                for gi, g in enumerate(groups):
                    if g.has_past:
                        s_re = g.state_in(state_s5_re[r].reshape(g.n_batch, gp), gp)
                        s_im = g.state_in(state_s5_im[r].reshape(g.n_batch, gp), gp)
                    else:
                        s_re, s_im = g.zeros_state(None, gp), g.zeros_state(None, gp)
                    y, o_re, o_im = _s5_core(u, mats, s_re, s_im, row0=g.row0, n_seq=g.n_seq, n_str=g.n_str,
                                             n_steps=g.n_steps, chained=g.chained)
                    ys.append(y)
                    outs[gi][5].append(g.state_out(o_re).reshape(g.n_batch, s5_a_re.shape[1], s5_a_re.shape[2]))
                    outs[gi][6].append(g.state_out(o_im).reshape(g.n_batch, s5_a_re.shape[1], s5_a_re.shape[2]))
                wg = s5_w_glu[r].astype(bf16)
                x = _matmul(jnp.concatenate(ys, axis=0), [wg[:, :d], wg[:, d:]], residual=x, glu=True, name="s5_glu")
            else:
                proj = _matmul(x, [sc_w_in[r].astype(bf16)], gain=norm_mix[i], name="sconv_in")
                us = []
                for gi, g in enumerate(groups):
                    buf = g.state_in(state_sconv[r], d) if g.has_past else g.zeros_state(SCONV_W - 1, d)
                    u, nb = _sconv_core(proj, sc_conv_w[r], buf, row0=g.row0, n_seq=g.n_seq, n_str=g.n_str,
                                        n_steps=g.n_steps, chained=g.chained)
                    us.append(u)
                    outs[gi][7].append(nb.transpose(0, 2, 1, 3).reshape(g.n_batch, SCONV_W - 1, d))
                x = _matmul(jnp.concatenate(us, axis=0), [sc_w_out[r].astype(bf16)], residual=x, name="sconv_out")
        x = _ffn(x, norm_ffn[i], ffn_w_up[i].astype(bf16), ffn_w_down[i].astype(bf16))

    x = want("natural", x, layout)
    y_prompt = x[:bp * tp].reshape(bp, tp, d)
    y_sample = x[bp * tp:].reshape(bs, ts, d)
    st = [[jnp.stack(l) for l in outs[gi]] for gi in range(2)]
    return (y_prompt, y_sample, *st[0], *st[1])
```

```python
import functools
import math

import jax
import jax.numpy as jnp
from jax import lax
from jax.experimental import pallas as pl
from jax.experimental.pallas import tpu as pltpu

f32 = jnp.float32
bf16 = jnp.bfloat16
i32 = jnp.int32

LANES = 128
SUBLANES = 8
VMEM_LIMIT = 56 * 1024 * 1024

CHUNK = 64
EPS = 1e-6
N_HEADS = 16
N_KV_HEADS = 4
HEAD_DIM = 128
GQA_GROUP = N_HEADS // N_KV_HEADS
ROT_DIM = HEAD_DIM // 4
ROPE_THETA = 500000.0
ATTN_SCALE = HEAD_DIM ** -0.5
N_IDX_HEADS = 16
IDX_DIM = 64
IDX_ROT_DIM = IDX_DIM // 4
TOPK_MAX = 256
LRU_BLOCK = 128
LRU_CONV = 4
LRU_C = 8.0
S5_GROUP = 16
S5_STATE = 64
SCONV_W = 3
PROMPT_STREAMS = 8

NEG_BIG = -1e30
INT_MIN = -(2 ** 31)
LOG2E = math.log2(math.e)


def _cparams(n_axes):
    return pltpu.CompilerParams(dimension_semantics=("arbitrary",) * n_axes, vmem_limit_bytes=VMEM_LIMIT)


def _pick(n, cands):
    for c in cands:
        if n % c == 0:
            return c
    raise ValueError(f"no tile in {cands} divides {n}")


def _rmsnorm_rows(x, g):
    ms = jnp.mean(x * x, axis=-1, keepdims=True)
    return x * lax.rsqrt(ms + EPS) * g


def _mm_body(*refs, norm, n_w, has_res, glu):
    it = iter(refs)
    x_ref = next(it)
    g_ref = next(it) if norm else None
    w_refs = [next(it) for _ in range(n_w)]
    r_ref = next(it) if has_res else None
    o_ref = next(it)
    h_ref = next(it) if norm else None

    if norm:
        @pl.when(pl.program_id(1) == 0)
        def _():
            h_ref[...] = _rmsnorm_rows(x_ref[...], g_ref[...]).astype(bf16)
        h = h_ref[...]
    else:
        h = x_ref[...]
    acc = jnp.dot(h, w_refs[0][...], preferred_element_type=f32)
    if glu:
        acc = acc * jax.nn.sigmoid(jnp.dot(h, w_refs[1][...], preferred_element_type=f32))
    if has_res:
        acc = acc + r_ref[...]
    o_ref[...] = acc.astype(o_ref.dtype)


def _matmul(x, ws, *, gain=None, residual=None, out_dtype=f32, glu=False, name="mm"):
    n, k = x.shape
    m = ws[0].shape[1]
    norm = gain is not None
    tm = _pick(n, (1024, 512, 256, 128, 64))
    tn = _pick(m, (1024, 768, 512, 256, 128))
    in_specs = [pl.BlockSpec((tm, k), lambda i, j: (i, 0))]
    args = [x]
    if norm:
        in_specs.append(pl.BlockSpec((1, k), lambda i, j: (0, 0)))
        args.append(gain.reshape(1, k).astype(f32))
    for w in ws:
        in_specs.append(pl.BlockSpec((k, tn), lambda i, j: (0, j)))
        args.append(w)
    if residual is not None:
        in_specs.append(pl.BlockSpec((tm, tn), lambda i, j: (i, j)))
        args.append(residual)
    body = functools.partial(_mm_body, norm=norm, n_w=len(ws), has_res=residual is not None, glu=glu)
    return pl.pallas_call(
        body,
        grid=(n // tm, m // tn),
        in_specs=in_specs,
        out_specs=pl.BlockSpec((tm, tn), lambda i, j: (i, j)),
        out_shape=jax.ShapeDtypeStruct((n, m), out_dtype),
        scratch_shapes=[pltpu.VMEM((tm, k), bf16)] if norm else [],
        compiler_params=_cparams(2),
        name=name,
    )(*args)


def _ffn_body(x_ref, g_ref, wu_ref, wd_ref, o_ref, h_ref):
    @pl.when(pl.program_id(1) == 0)
    def _():
        x = x_ref[...]
        h_ref[...] = _rmsnorm_rows(x, g_ref[...]).astype(bf16)
        o_ref[...] = x

    u = jnp.dot(h_ref[...], wu_ref[...], preferred_element_type=f32)
    u = jnp.square(jnp.maximum(u, 0.0)).astype(bf16)
    o_ref[...] += jnp.dot(u, wd_ref[...], preferred_element_type=f32)


def _ffn(x, gain, w_up, w_down, layer):
    n, d = x.shape
    ff = w_up.shape[2]
    tm = _pick(n, (512, 256, 128, 64))
    tf = _pick(ff, (1024, 512, 256, 128))
    return pl.pallas_call(
        _ffn_body,
        grid=(n // tm, ff // tf),
        in_specs=[
            pl.BlockSpec((tm, d), lambda i, j: (i, 0)),
            pl.BlockSpec((1, d), lambda i, j: (0, 0)),
            pl.BlockSpec((None, d, tf), lambda i, j: (layer, 0, j)),
            pl.BlockSpec((None, tf, d), lambda i, j: (layer, j, 0)),
        ],
        out_specs=pl.BlockSpec((tm, d), lambda i, j: (i, 0)),
        out_shape=jax.ShapeDtypeStruct((n, d), f32),
        scratch_shapes=[pltpu.VMEM((tm, d), bf16)],
        compiler_params=_cparams(2),
        name="ffn",
    )(x, gain.reshape(1, d).astype(f32), w_up, w_down)


def _norm_body(x_ref, g_ref, o_ref):
    o_ref[...] = _rmsnorm_rows(x_ref[...], g_ref[...])


def _norm(x, gain):
    n, d = x.shape
    tm = _pick(n, (512, 256, 128, 64))
    return pl.pallas_call(
        _norm_body,
        grid=(n // tm,),
        in_specs=[pl.BlockSpec((tm, d), lambda i: (i, 0)), pl.BlockSpec((1, d), lambda i: (0, 0))],
        out_specs=pl.BlockSpec((tm, d), lambda i: (i, 0)),
        out_shape=jax.ShapeDtypeStruct((n, d), f32),
        compiler_params=_cparams(1),
        name="rmsnorm",
    )(x, gain.reshape(1, d).astype(f32))


def _rope_tables(pos, width, rot):
    half = rot // 2
    inv = ROPE_THETA ** (-jnp.arange(half, dtype=f32) / half)
    ang = pos.astype(f32)[:, None] * inv[None, :]
    cos, sin = jnp.cos(ang), jnp.sin(ang)
    n = pos.shape[0]
    pad = jnp.zeros((n, width - rot), f32)
    zero = jnp.zeros((n, half), f32)
    c = jnp.concatenate([cos, cos, pad + 1.0], axis=1)
    s_up = jnp.concatenate([-sin, zero, pad], axis=1)
    s_dn = jnp.concatenate([zero, sin, pad], axis=1)
    reps = LANES // width
    return tuple(jnp.tile(t, (1, reps)) for t in (c, s_up, s_dn))


def _rope(x, c, s_up, s_dn, half):
    return x * c + pltpu.roll(x, LANES - half, axis=1) * s_up + pltpu.roll(x, half, axis=1) * s_dn


def _attn_post_body(q_ref, k_ref, qi_ref, kw_ref, c_ref, su_ref, sd_ref, ci_ref, sui_ref, sdi_ref,
                    gq_ref, gk_ref, gki_ref, qo_ref, ko_ref, qio_ref, kio_ref):
    c, su, sd = c_ref[...], su_ref[...], sd_ref[...]
    ci, sui, sdi = ci_ref[...], sui_ref[...], sdi_ref[...]
    for h in range(N_HEADS):
        sl = slice(h * HEAD_DIM, (h + 1) * HEAD_DIM)
        y = _rope(_rmsnorm_rows(q_ref[:, sl], gq_ref[...]), c, su, sd, ROT_DIM // 2)
        qo_ref[:, sl] = y.astype(qo_ref.dtype)
    for h in range(N_KV_HEADS):
        sl = slice(h * HEAD_DIM, (h + 1) * HEAD_DIM)
        ko_ref[:, sl] = _rope(_rmsnorm_rows(k_ref[:, sl], gk_ref[...]), c, su, sd, ROT_DIM // 2)
    for j in range(N_IDX_HEADS // 2):
        y = _rope(qi_ref[:, j * LANES:(j + 1) * LANES], ci, sui, sdi, IDX_ROT_DIM // 2).astype(qio_ref.dtype)
        qio_ref[2 * j] = y[:, :IDX_DIM]
        qio_ref[2 * j + 1] = y[:, IDX_DIM:]
    kw = kw_ref[:, :LANES]
    lane = lax.broadcasted_iota(i32, kw.shape, 1)
    ms = jnp.sum(jnp.where(lane < IDX_DIM, kw * kw, 0.0), axis=-1, keepdims=True) * (1.0 / IDX_DIM)
    y = _rope(kw * lax.rsqrt(ms + EPS) * gki_ref[...], ci, sui, sdi, IDX_ROT_DIM // 2)
    kio_ref[...] = y[:, :IDX_DIM]


def _attn_post(proj, pos, g_q, g_k, g_ki):
    n = proj.shape[0]
    tm = _pick(n, (256, 128, 64))
    q_w, kv_w, iq_w = N_HEADS * HEAD_DIM, N_KV_HEADS * HEAD_DIM, N_IDX_HEADS * IDX_DIM
    tabs = _rope_tables(pos, HEAD_DIM, ROT_DIM) + _rope_tables(pos, IDX_DIM, IDX_ROT_DIM)
    gki = jnp.concatenate([g_ki.astype(f32), jnp.zeros((LANES - IDX_DIM,), f32)]).reshape(1, LANES)
    row = lambda i: (i, 0)
    const = lambda i: (0, 0)
    in_specs = [
        pl.BlockSpec((tm, q_w), row),
        pl.BlockSpec((tm, kv_w), lambda i: (i, q_w // kv_w)),
        pl.BlockSpec((tm, iq_w), lambda i: (i, (q_w + 2 * kv_w) // iq_w)),
        pl.BlockSpec((tm, kv_w), lambda i: (i, (q_w + 2 * kv_w + iq_w) // kv_w)),
    ] + [pl.BlockSpec((tm, LANES), row)] * 6 + [pl.BlockSpec((1, LANES), const)] * 3
    out_shape = (
        jax.ShapeDtypeStruct((n, q_w), bf16),
        jax.ShapeDtypeStruct((n, kv_w), f32),
        jax.ShapeDtypeStruct((N_IDX_HEADS, n, IDX_DIM), bf16),
        jax.ShapeDtypeStruct((n, IDX_DIM), f32),
    )
    out_specs = (
        pl.BlockSpec((tm, q_w), row),
        pl.BlockSpec((tm, kv_w), row),
        pl.BlockSpec((N_IDX_HEADS, tm, IDX_DIM), lambda i: (0, i, 0)),
        pl.BlockSpec((tm, IDX_DIM), row),
    )
    return pl.pallas_call(
        _attn_post_body,
        grid=(n // tm,),
        in_specs=in_specs,
        out_specs=out_specs,
        out_shape=out_shape,
        compiler_params=_cparams(1),
        name="attn_post",
    )(proj, proj, proj, proj, *tabs, g_q.reshape(1, -1).astype(f32), g_k.reshape(1, -1).astype(f32), gki)


def _sortable(x):
    b = pltpu.bitcast(x, i32)
    return b ^ ((b >> 31) & 0x7FFFFFFF)


def _attn_core_body(*refs, segs, tq, n_sel, causal, q_pos0, aliased):
    n_seg = len(segs)
    seg_lens = [n for n, _ in segs]
    it = iter(refs)
    qi_ref, wi_ref, q_ref = next(it), next(it), next(it)
    seg_refs = [(next(it), next(it), next(it)) for _ in range(n_seg)]
    if aliased:
        next(it)
    o_ref = next(it)
    keys_ref, bias_ref, sc_ref, acc_ref = (next(it) for _ in range(4))

    l_tot = sum(seg_lens)
    offs = [sum(seg_lens[:s]) for s in range(n_seg)]
    dup = LANES // tq
    wq = GQA_GROUP * tq
    q_tile = pl.program_id(1)
    g_head = pl.program_id(2)
    if causal:
        assert n_seg == 1
        gran = _pick(l_tot, (1024, 512))
        n_gran = jnp.minimum((q_pos0 + (q_tile + 1) * tq + gran - 1) // gran, l_tot // gran)
        blk = 512
        n_blk = n_gran * (gran // blk)
    else:
        gran = None
        blk = _pick(l_tot, (512, 320, 256, 192, 128, 64))
        n_blk = l_tot // blk

    def n_chunks(s, lc):
        return n_gran * (gran // lc) if causal else seg_lens[s] // lc

    @pl.when(g_head == 0)
    def _select():
        wi_rows = [wi_ref[h:h + 1, :] for h in range(N_IDX_HEADS)]
        qi_all = jnp.concatenate([qi_ref[h] for h in range(N_IDX_HEADS) for _ in range(dup)], axis=0)

        for s in range(n_seg):
            ki_ref = seg_refs[s][0]
            lc = _pick(seg_lens[s], (256, 128, 64))

            def score_chunk(c, carry, ki_ref=ki_ref, lc=lc, off=offs[s]):
                r0 = pl.multiple_of(c * lc, lc)
                kic = ki_ref[pl.ds(r0, lc), :].astype(bf16)
                sc = lax.dot_general(kic, qi_all, (((1,), (1,)), ((), ())), preferred_element_type=f32)
                acc = jnp.zeros((lc, LANES), f32)
                for h in range(N_IDX_HEADS):
                    acc = acc + jnp.maximum(sc[:, h * LANES:(h + 1) * LANES], 0.0) * wi_rows[h]
                key = _sortable(acc)
                if causal:
                    kpos = r0 + lax.broadcasted_iota(i32, (lc, LANES), 0)
                    qpos = q_pos0 + q_tile * tq + lax.broadcasted_iota(i32, (lc, LANES), 1)
                    key = jnp.where((kpos // CHUNK) <= (qpos // CHUNK), key, INT_MIN)
                keys_ref[pl.ds(pl.multiple_of(off + r0, CHUNK), lc), :] = key
                return carry

            lax.fori_loop(0, n_chunks(s, lc), score_chunk, 0)

        def count(pred):
            def body(i, acc):
                r0 = pl.multiple_of(i * blk, blk)
                hit = jnp.where(pred(keys_ref[pl.ds(r0, blk), :], r0), 1, 0)
                return acc + jnp.sum(hit.reshape(blk // SUBLANES, SUBLANES, LANES), axis=0)
            acc = lax.fori_loop(0, n_blk, body, jnp.zeros((SUBLANES, LANES), i32))
            return jnp.sum(acc, axis=0, keepdims=True)

        def count_ge(cand):
            return count(lambda k, r0: k >= cand)

        ans = jnp.where(count_ge(jnp.zeros((1, LANES), i32)) >= n_sel, 0, INT_MIN)

        def bit_step(i, ans):
            cand = ans | (1 << (30 - i))
            return jnp.where(count_ge(cand) >= n_sel, cand, ans)

        ans = lax.fori_loop(0, 31, bit_step, ans)
        tau = jnp.maximum(ans, INT_MIN + 1)
        n_ge = count_ge(tau)
        ties = jnp.max(n_ge) > n_sel

        @pl.when(jnp.logical_not(ties))
        def _():
            def body(i, carry):
                r0 = pl.multiple_of(i * blk, blk)
                keep = keys_ref[pl.ds(r0, blk), :] >= tau
                bias_ref[pl.ds(r0, blk), :] = jnp.where(keep, 0.0, NEG_BIG).astype(bias_ref.dtype)
                return carry
            lax.fori_loop(0, n_blk, body, 0)

        @pl.when(ties)
        def _():
            def row_idx(r0):
                return r0 + lax.broadcasted_iota(i32, (blk, LANES), 0)
            need = n_sel - count(lambda k, r0: k > tau)

            def idx_step(i, m):
                cand = m | (1 << (14 - i))
                n_below = count(lambda k, r0: jnp.logical_and(k == tau, row_idx(r0) < cand))
                return jnp.where(n_below <= need, cand, m)

            m_cut = lax.fori_loop(0, 15, idx_step, jnp.zeros((1, LANES), i32))

            def body(i, carry):
                r0 = pl.multiple_of(i * blk, blk)
                k = keys_ref[pl.ds(r0, blk), :]
                keep = jnp.logical_or(k > tau, jnp.logical_and(k == tau, row_idx(r0) < m_cut))
                bias_ref[pl.ds(r0, blk), :] = jnp.where(keep, 0.0, NEG_BIG).astype(bias_ref.dtype)
                return carry
            lax.fori_loop(0, n_blk, body, 0)

    qg = jnp.concatenate([q_ref[:, j * HEAD_DIM:(j + 1) * HEAD_DIM] for j in range(GQA_GROUP)], axis=0)
    q_of_row = lax.broadcasted_iota(i32, (wq, LANES), 0) % tq
    one_hot = (lax.broadcasted_iota(i32, (wq, LANES), 1) == q_of_row).astype(bf16)
    q_aug = jnp.concatenate([qg, one_hot], axis=1)
    c_exp = ATTN_SCALE * LOG2E

    def read_kv(ref, interleaved, r0, lc):
        if interleaved:
            return ref[pl.ds(r0 * N_KV_HEADS + g_head, lc, stride=N_KV_HEADS), :].astype(bf16)
        return ref[pl.ds(r0, lc), :].astype(bf16)

    def fold_rows(x, op):
        return op(x.reshape(x.shape[0] // SUBLANES, SUBLANES, x.shape[1]), axis=0)

    m_acc = jnp.full((SUBLANES, wq), NEG_BIG, f32)
    for s in range(n_seg):
        k_ref = seg_refs[s][1]
        lc = _pick(seg_lens[s], (1024, 512, 256, 128, 64))

        def logits_chunk(c, m_acc, k_ref=k_ref, lc=lc, off=offs[s], inter=segs[s][1]):
            r0 = pl.multiple_of(c * lc, lc)
            row = pl.multiple_of(off + r0, CHUNK)
            k_aug = jnp.concatenate([read_kv(k_ref, inter, r0, lc), bias_ref[pl.ds(row, lc), :]], axis=1)
            sc = lax.dot_general(k_aug, q_aug, (((1,), (1,)), ((), ())), preferred_element_type=f32)
            sc_ref[pl.ds(row, lc), :] = sc
            return jnp.maximum(m_acc, fold_rows(sc, jnp.max))

        m_acc = lax.fori_loop(0, n_chunks(s, lc), logits_chunk, m_acc)
    m_row = jnp.max(m_acc, axis=0, keepdims=True)

    acc_ref[...] = jnp.zeros(acc_ref.shape, f32)
    l_acc = jnp.zeros((SUBLANES, wq), f32)
    for s in range(n_seg):
        v_ref = seg_refs[s][2]
        lc = _pick(seg_lens[s], (1024, 512, 256, 128, 64))

        def value_chunk(c, l_acc, v_ref=v_ref, lc=lc, off=offs[s], inter=segs[s][1]):
            r0 = pl.multiple_of(c * lc, lc)
            row = pl.multiple_of(off + r0, CHUNK)
            p = jnp.exp2((sc_ref[pl.ds(row, lc), :] - m_row) * c_exp)
            vc = read_kv(v_ref, inter, r0, lc)
            acc_ref[...] += lax.dot_general(vc, p.astype(bf16), (((0,), (0,)), ((), ())), preferred_element_type=f32)
            return l_acc + fold_rows(p, jnp.sum)

        l_acc = lax.fori_loop(0, n_chunks(s, lc), value_chunk, l_acc)

    o_t = (acc_ref[...] / jnp.sum(l_acc, axis=0, keepdims=True)).T
    for j in range(GQA_GROUP):
        o_ref[:, j * HEAD_DIM:(j + 1) * HEAD_DIM] = o_t[j * tq:(j + 1) * tq, :].astype(o_ref.dtype)


def _attn_core(q, qi, wi_t, segs, prev, *, n_batch, t_seq, row0, q_lo, q_hi, n_sel, causal):
    n = q.shape[0]
    tq = min(LANES, t_seq)
    assert tq == LANES or not causal
    nq = (q_hi - q_lo) // tq
    wq = GQA_GROUP * tq
    kv_w = GQA_GROUP * HEAD_DIM
    qrow = lambda b, i, g: (row0 // tq + b * (t_seq // tq) + q_lo // tq + i)
    in_specs = [
        pl.BlockSpec((N_IDX_HEADS, tq, IDX_DIM), lambda b, i, g: (0, qrow(b, i, g), 0)),
        pl.BlockSpec((None, N_IDX_HEADS, LANES), lambda b, i, g: (b, 0, (q_lo // tq + i) * tq // LANES)),
        pl.BlockSpec((tq, kv_w), lambda b, i, g: (qrow(b, i, g), g)),
    ]
    args = [qi, wi_t, q]
    seg_meta = []
    for ki, k, v, n_keys, imap, interleaved in segs:
        lead = (None,) * (ki.ndim - 2)
        kv_rows = n_keys * N_KV_HEADS if interleaved else n_keys
        in_specs += [
            pl.BlockSpec(lead + (n_keys, IDX_DIM), functools.partial(imap, head=None)),
            pl.BlockSpec(lead + (kv_rows, HEAD_DIM), functools.partial(imap, head=not interleaved)),
            pl.BlockSpec(lead + (kv_rows, HEAD_DIM), functools.partial(imap, head=not interleaved)),
        ]
        args += [ki, k, v]
        seg_meta.append((n_keys, interleaved))
    aliases = {}
    if prev is not None:
        aliases = {len(args): 0}
        in_specs.append(pl.BlockSpec(memory_space=pl.ANY))
        args.append(prev)
    l_tot = sum(m[0] for m in seg_meta)
    body = functools.partial(_attn_core_body, segs=tuple(seg_meta), tq=tq, n_sel=n_sel, causal=causal, q_pos0=q_lo,
                             aliased=prev is not None)
    return pl.pallas_call(
        body,
        grid=(n_batch, nq, N_KV_HEADS),
        in_specs=in_specs,
        out_specs=pl.BlockSpec((tq, kv_w), lambda b, i, g: (qrow(b, i, g), g)),
        out_shape=jax.ShapeDtypeStruct((n, N_HEADS * HEAD_DIM), bf16),
        scratch_shapes=[
            pltpu.VMEM((l_tot, LANES), i32),
            pltpu.VMEM((l_tot, LANES), bf16),
            pltpu.VMEM((l_tot, wq), f32),
            pltpu.VMEM((HEAD_DIM, wq), f32),
        ],
        input_output_aliases=aliases,
        compiler_params=_cparams(3),
        name="attn_core",
    )(*args)


def _attn_group(q, qi, wi, ki_new, k_new, v_new, past, prev, *, n_batch, t_seq, row0, offset):
    n_rows = n_batch * t_seq
    wi_g = wi[row0:row0 + n_rows].reshape(n_batch, t_seq, N_IDX_HEADS).transpose(0, 2, 1)
    if t_seq < LANES:
        wi_g = jnp.tile(wi_g, (1, 1, LANES // t_seq))
    ki_g = ki_new[row0:row0 + n_rows].reshape(n_batch, t_seq, IDX_DIM)
    k_g = k_new[row0:row0 + n_rows].reshape(n_batch, t_seq, N_KV_HEADS * HEAD_DIM)
    v_g = v_new[row0:row0 + n_rows].reshape(n_batch, t_seq, N_KV_HEADS * HEAD_DIM)
    l_all = offset + t_seq
    n_sel = min(TOPK_MAX, l_all // 4)

    def new_map(b, i, g, head):
        return (b, 0, g if head else 0)

    if past is None:
        segs = [(ki_g, k_g, v_g, t_seq, new_map, False)]
        return _attn_core(q, qi, wi_g, segs, prev, n_batch=n_batch, t_seq=t_seq, row0=row0, q_lo=0, q_hi=t_seq,
                          n_sel=n_sel, causal=True)

    assert offset % CHUNK == 0 and t_seq <= CHUNK
    pk, pv, pki, r = past
    n_past = pk.shape[2]
    pk = pk.reshape(pk.shape[0], n_batch, n_past * N_KV_HEADS, HEAD_DIM)
    pv = pv.reshape(pv.shape[0], n_batch, n_past * N_KV_HEADS, HEAD_DIM)

    def past_map(b, i, g, head):
        return (r, b, 0, 0)

    segs = [(pki, pk, pv, n_past, past_map, True), (ki_g, k_g, v_g, t_seq, new_map, False)]
    return _attn_core(q, qi, wi_g, segs, prev, n_batch=n_batch, t_seq=t_seq, row0=row0, q_lo=0, q_hi=t_seq,
                      n_sel=n_sel, causal=False)


def _attn_layer(x, groups, past, gain, w_in, g_q, g_k, g_ki, w_o):
    q_w, kv_w, iq_w = N_HEADS * HEAD_DIM, N_KV_HEADS * HEAD_DIM, N_IDX_HEADS * IDX_DIM
    n_in = w_in.shape[1]
    n_pad = -(-n_in // 768) * 768
    w_in_p = jnp.pad(w_in, ((0, 0), (0, n_pad - n_in))).astype(bf16)
    proj = _matmul(x, [w_in_p], gain=gain, name="attn_in")
    pos = jnp.concatenate([off + jnp.tile(jnp.arange(t), b) for (row0, b, t, off) in groups])
    q, k, qi, ki = _attn_post(proj, pos, g_q, g_k, g_ki)
    v = proj[:, q_w + kv_w:q_w + 2 * kv_w]
    wi = proj[:, q_w + 2 * kv_w + iq_w + IDX_DIM:q_w + 2 * kv_w + iq_w + IDX_DIM + N_IDX_HEADS]
    o = None
    for gi, (row0, b, t, off) in enumerate(groups):
        o = _attn_group(q, qi, wi, ki, k, v, past[gi], o, n_batch=b, t_seq=t, row0=row0, offset=off)
    x = _matmul(o, [w_o.astype(bf16)], residual=x, name="attn_out")
    return x, k, v, ki


def _shared_output(body, args, in_specs, prev):
    if prev is None:
        return body, args, in_specs, {}
    idx = len(args)

    def wrapped(*refs):
        return body(*refs[:idx], *refs[idx + 1:])

    return wrapped, args + [prev], in_specs + [pl.BlockSpec(memory_space=pl.ANY)], {idx: 0}


def _halo_rows(x_ref, init, j, *, n_str, n_steps, chained):
    if not chained:
        return init
    prev = x_ref[(n_steps - j) * n_str:(n_steps - j + 1) * n_str, :]
    shifted = pltpu.roll(prev, 1, axis=0)
    first = lax.broadcasted_iota(i32, shifted.shape, 0) == 0
    return jnp.where(first, init, shifted)


def _chain_carries(h_end, p_end, c0):
    rows = [c0]
    for s in range(1, h_end.shape[0]):
        rows.append(p_end[s - 1:s] * rows[-1] + h_end[s - 1:s])
    return jnp.concatenate(rows, axis=0)


def _lru_core_body(xb_ref, gate_ref, cw_ref, cb_ref, wa_ref, ba_ref, wx_ref, bx_ref, lam_ref, buf_ref, h0_ref,
                   y_ref, hl_ref, a_ref, b_ref, *, n_str, n_steps, chained, lw):
    rows = n_str * n_steps
    ch = _pick(n_steps, (16, 8, 4, 2, 1)) * n_str
    n_halo = LRU_CONV - 1
    cw = [cw_ref[j:j + 1, :] for j in range(LRU_CONV)]
    cb = cb_ref[...]
    sp = jax.nn.softplus(-lam_ref[...])
    halo = jnp.concatenate(
        [_halo_rows(xb_ref, buf_ref[n_halo - j], j, n_str=n_str, n_steps=n_steps, chained=chained)
         for j in range(n_halo, 0, -1)], axis=0)

    def gates(xin, r0):
        xc = cb
        for j in range(LRU_CONV):
            xc = xc + xin[j * n_str:j * n_str + ch, :] * cw[j]
        xcb = xc.astype(bf16)
        for t in range(lw // LRU_BLOCK):
            sl = slice(t * LRU_BLOCK, (t + 1) * LRU_BLOCK)
            xt = xcb[:, sl]
            r = jax.nn.sigmoid(jnp.dot(xt, wa_ref[t], preferred_element_type=f32) + ba_ref[:, sl])
            ig = jax.nn.sigmoid(jnp.dot(xt, wx_ref[t], preferred_element_type=f32) + bx_ref[:, sl])
            log_a = (-LRU_C) * r * sp[:, sl]
            th = jnp.tanh(log_a)
            mult = jnp.sqrt(-2.0 * th / (1.0 - th))
            a_ref[pl.ds(r0, ch), sl] = jnp.exp(log_a)
            b_ref[pl.ds(r0, ch), sl] = mult * (ig * xc[:, sl])

    gates(jnp.concatenate([halo, xb_ref[0:ch, :]], axis=0), 0)

    def gate_chunk(c, carry):
        r0 = pl.multiple_of(c * ch, ch)
        gates(xb_ref[pl.ds(r0 - n_halo * n_str, ch + n_halo * n_str), :], r0)
        return carry

    lax.fori_loop(1, rows // ch, gate_chunk, 0)

    def scan_step(t, carry):
        r0 = pl.multiple_of(t * n_str, n_str)
        a = a_ref[pl.ds(r0, n_str), :]
        h = a * carry[0] + b_ref[pl.ds(r0, n_str), :]
        b_ref[pl.ds(r0, n_str), :] = h
        if not chained:
            return (h,)
        p = a * carry[1]
        a_ref[pl.ds(r0, n_str), :] = p
        return (h, p)

    if chained:
        init = (jnp.zeros((n_str, lw), f32), jnp.ones((n_str, lw), f32))
    else:
        init = (h0_ref[...],)
    fin = lax.fori_loop(0, n_steps, scan_step, init, unroll=8)

    if chained:
        carries = _chain_carries(fin[0], fin[1], h0_ref[...])
        c_rep = jnp.concatenate([carries] * (ch // n_str), axis=0)

    def out_chunk(c, carry):
        r0 = pl.multiple_of(c * ch, ch)
        h = b_ref[pl.ds(r0, ch), :]
        if chained:
            h = h + a_ref[pl.ds(r0, ch), :] * c_rep
            b_ref[pl.ds(r0, ch), :] = h
        y_ref[pl.ds(r0, ch), :] = (h * jax.nn.gelu(gate_ref[pl.ds(r0, ch), :])).astype(y_ref.dtype)
        return carry

    lax.fori_loop(0, rows // ch, out_chunk, 0)
    if chained:
        hl_ref[...] = b_ref[rows - 1:rows, :]
    else:
        hl_ref[...] = fin[0]


def _lru_core(proj, params, buf, h0, prev, *, row0, n_seq, n_str, n_steps, chained):
    w = proj.shape[1] // 2
    lw = 256
    rows = n_str * n_steps
    rb0 = row0 // rows
    ns = 1 if chained else n_str
    cw, cb, wa, ba, wx, bx, lam = params
    nb = lw // LRU_BLOCK
    vec = lambda s, n: (0, n)
    in_specs = [
        pl.BlockSpec((rows, lw), lambda s, n: (rb0 + s, w // lw + n)),
        pl.BlockSpec((rows, lw), lambda s, n: (rb0 + s, n)),
        pl.BlockSpec((LRU_CONV, lw), vec),
        pl.BlockSpec((1, lw), vec),
        pl.BlockSpec((nb, LRU_BLOCK, LRU_BLOCK), lambda s, n: (n, 0, 0)),
        pl.BlockSpec((1, lw), vec),
        pl.BlockSpec((nb, LRU_BLOCK, LRU_BLOCK), lambda s, n: (n, 0, 0)),
        pl.BlockSpec((1, lw), vec),
        pl.BlockSpec((1, lw), vec),
        pl.BlockSpec((None, LRU_CONV - 1, ns, lw), lambda s, n: (s, 0, 0, n)),
        pl.BlockSpec((None, ns, lw), lambda s, n: (s, 0, n)),
    ]
    body = functools.partial(_lru_core_body, n_str=n_str, n_steps=n_steps, chained=chained, lw=lw)
    args = [proj, proj, cw, cb.reshape(1, w), wa.astype(bf16), ba.reshape(1, w), wx.astype(bf16), bx.reshape(1, w),
            lam.reshape(1, w), buf, h0]
    body, args, in_specs, aliases = _shared_output(body, args, in_specs, prev)
    return pl.pallas_call(
        body,
        grid=(n_seq, w // lw),
        in_specs=in_specs,
        out_specs=(
            pl.BlockSpec((rows, lw), lambda s, n: (rb0 + s, n)),
            pl.BlockSpec((None, ns, lw), lambda s, n: (s, 0, n)),
        ),
        out_shape=(
            jax.ShapeDtypeStruct((proj.shape[0], w), bf16),
            jax.ShapeDtypeStruct((n_seq, ns, w), f32),
        ),
        scratch_shapes=[pltpu.VMEM((rows, lw), f32), pltpu.VMEM((rows, lw), f32)],
        input_output_aliases=aliases,
        compiler_params=_cparams(2),
        name="lru_core",
    )(*args)


def _s5_core_body(u_ref, bm_ref, cm_ref, are_ref, aim_ref, d_ref, sre_ref, sim_ref,
                  y_ref, ore_ref, oim_ref, hs_ref, p_ref, *, n_str, n_steps, chained):
    rows = n_str * n_steps
    half = hs_ref.shape[1] // 2
    ch = _pick(n_steps, (32, 16, 8, 4, 2, 1)) * n_str

    def proj_chunk(c, carry):
        r0 = pl.multiple_of(c * ch, ch)
        hs_ref[pl.ds(r0, ch), :] = jnp.dot(u_ref[pl.ds(r0, ch), :].astype(bf16), bm_ref[...], preferred_element_type=f32)
        return carry

    lax.fori_loop(0, rows // ch, proj_chunk, 0)

    a_re = jnp.broadcast_to(are_ref[...], (n_str, half))
    a_im = jnp.broadcast_to(aim_ref[...], (n_str, half))

    def scan_step(t, carry):
        r0 = pl.multiple_of(t * n_str, n_str)
        h_re, h_im = carry[0], carry[1]
        n_re = a_re * h_re - a_im * h_im + hs_ref[pl.ds(r0, n_str), :half]
        n_im = a_re * h_im + a_im * h_re + hs_ref[pl.ds(r0, n_str), half:]
        hs_ref[pl.ds(r0, n_str), :half] = n_re
        hs_ref[pl.ds(r0, n_str), half:] = n_im
        if not chained:
            return (n_re, n_im)
        p_re, p_im = carry[2], carry[3]
        q_re = a_re[:1] * p_re - a_im[:1] * p_im
        q_im = a_re[:1] * p_im + a_im[:1] * p_re
        p_ref[pl.ds(t, 1), :half] = q_re
        p_ref[pl.ds(t, 1), half:] = q_im
        return (n_re, n_im, q_re, q_im)

    if chained:
        zero = jnp.zeros((n_str, half), f32)
        init = (zero, zero, jnp.ones((1, half), f32), jnp.zeros((1, half), f32))
    else:
        init = (sre_ref[...], sim_ref[...])
    fin = lax.fori_loop(0, n_steps, scan_step, init, unroll=2)

    if chained:
        pe_re, pe_im = fin[2], fin[3]
        c_re, c_im = [sre_ref[...]], [sim_ref[...]]
        for s in range(1, n_str):
            c_re.append(pe_re * c_re[-1] - pe_im * c_im[-1] + fin[0][s - 1:s])
            c_im.append(pe_re * c_im[-1] + pe_im * c_re[-2] + fin[1][s - 1:s])
        c_re = jnp.concatenate(c_re, axis=0)
        c_im = jnp.concatenate(c_im, axis=0)

        def fix_step(t, carry):
            r0 = pl.multiple_of(t * n_str, n_str)
            p_re = jnp.broadcast_to(p_ref[pl.ds(t, 1), :half], (n_str, half))
            p_im = jnp.broadcast_to(p_ref[pl.ds(t, 1), half:], (n_str, half))
            hs_ref[pl.ds(r0, n_str), :half] += p_re * c_re - p_im * c_im
            hs_ref[pl.ds(r0, n_str), half:] += p_re * c_im + p_im * c_re
            return carry

        lax.fori_loop(0, n_steps, fix_step, 0, unroll=2)
        ore_ref[...] = hs_ref[rows - 1:rows, :half]
        oim_ref[...] = hs_ref[rows - 1:rows, half:]
    else:
        ore_ref[...] = fin[0]
        oim_ref[...] = fin[1]

    def out_chunk(c, carry):
        r0 = pl.multiple_of(c * ch, ch)
        u = u_ref[pl.ds(r0, ch), :]
        y = jnp.dot(hs_ref[pl.ds(r0, ch), :].astype(bf16), cm_ref[...], preferred_element_type=f32) + d_ref[...] * u
        y_ref[pl.ds(r0, ch), :] = jax.nn.gelu(y).astype(y_ref.dtype)
        return carry

    lax.fori_loop(0, rows // ch, out_chunk, 0)


def _s5_core(u, mats, s_re, s_im, prev, *, row0, n_seq, n_str, n_steps, chained):
    d = u.shape[1]
    bm, cm, a_re, a_im, dsk = mats
    n_gb = d // LANES
    sw = bm.shape[2] // 2
    rows = n_str * n_steps
    rb0 = row0 // rows
    ns = 1 if chained else n_str
    st_spec = pl.BlockSpec((None, ns, sw), lambda s, g: (s, 0, g))
    in_specs = [
        pl.BlockSpec((rows, LANES), lambda s, g: (rb0 + s, g)),
        pl.BlockSpec((None, LANES, 2 * sw), lambda s, g: (g, 0, 0)),
        pl.BlockSpec((None, 2 * sw, LANES), lambda s, g: (g, 0, 0)),
        pl.BlockSpec((1, sw), lambda s, g: (0, g)),
        pl.BlockSpec((1, sw), lambda s, g: (0, g)),
        pl.BlockSpec((1, LANES), lambda s, g: (0, g)),
        st_spec,
        st_spec,
    ]
    body = functools.partial(_s5_core_body, n_str=n_str, n_steps=n_steps, chained=chained)
    st_shape = jax.ShapeDtypeStruct((n_seq, ns, n_gb * sw), f32)
    args = [u, bm, cm, a_re, a_im, dsk, s_re, s_im]
    body, args, in_specs, aliases = _shared_output(body, args, in_specs, prev)
    return pl.pallas_call(
        body,
        grid=(n_seq, n_gb),
        in_specs=in_specs,
        out_specs=(pl.BlockSpec((rows, LANES), lambda s, g: (rb0 + s, g)), st_spec, st_spec),
        out_shape=(jax.ShapeDtypeStruct((u.shape[0], d), bf16), st_shape, st_shape),
        scratch_shapes=[pltpu.VMEM((rows, 2 * sw), f32), pltpu.VMEM((n_steps if chained else SUBLANES, 2 * sw), f32)],
        input_output_aliases=aliases,
        compiler_params=_cparams(2),
        name="s5_core",
    )(*args)


def _s5_matrices(a_re, a_im, log_dt, b_re, b_im, c_re, c_im, d_skip):
    g, p = a_re.shape
    a = lax.complex(a_re.astype(f32), a_im.astype(f32))
    step = jnp.exp(log_dt.astype(f32))[:, None]
    a_bar = jnp.exp(a * step)
    b_bar = ((a_bar - 1.0) / a)[:, :, None] * lax.complex(b_re.astype(f32), b_im.astype(f32))
    gpb = LANES // S5_GROUP
    n_gb = g // gpb
    eye = jnp.eye(gpb, dtype=f32)

    def in_block(m):
        m = m.reshape(n_gb, gpb, p, S5_GROUP)
        return jnp.einsum('ngpc,gh->ngchp', m, eye).reshape(n_gb, gpb * S5_GROUP, gpb * p)

    def out_block(m):
        m = m.reshape(n_gb, gpb, S5_GROUP, p)
        return jnp.einsum('ngcp,gh->ngphc', m, eye).reshape(n_gb, gpb * p, gpb * S5_GROUP)

    bm = jnp.concatenate([in_block(jnp.real(b_bar)), in_block(jnp.imag(b_bar))], axis=2).astype(bf16)
    cm = jnp.concatenate([out_block(c_re.astype(f32)), -out_block(c_im.astype(f32))], axis=1).astype(bf16)
    return bm, cm, jnp.real(a_bar).reshape(1, g * p), jnp.imag(a_bar).reshape(1, g * p), d_skip.reshape(1, -1).astype(f32)


def _sconv_body(bg_ref, cg_ref, v_ref, cw_ref, buf_ref, u_ref, nb_ref, cv_ref, *, n_str, n_steps, chained):
    rows = n_str * n_steps
    ch = _pick(n_steps, (16, 8, 4, 2, 1)) * n_str
    n_halo = SCONV_W - 1
    cw = [cw_ref[j:j + 1, :] for j in range(SCONV_W)]

    def prod_chunk(c, carry):
        r0 = pl.multiple_of(c * ch, ch)
        cv_ref[pl.ds(r0, ch), :] = cg_ref[pl.ds(r0, ch), :] * v_ref[pl.ds(r0, ch), :]
        return carry

    lax.fori_loop(0, rows // ch, prod_chunk, 0)
    halo = jnp.concatenate(
        [_halo_rows(cv_ref, buf_ref[n_halo - j], j, n_str=n_str, n_steps=n_steps, chained=chained)
         for j in range(n_halo, 0, -1)], axis=0)

    def emit(xin, r0):
        z = xin[0:ch, :] * cw[0]
        for j in range(1, SCONV_W):
            z = z + xin[j * n_str:j * n_str + ch, :] * cw[j]
        u_ref[pl.ds(r0, ch), :] = (bg_ref[pl.ds(r0, ch), :] * z).astype(u_ref.dtype)

    emit(jnp.concatenate([halo, cv_ref[0:ch, :]], axis=0), 0)

    def out_chunk(c, carry):
        r0 = pl.multiple_of(c * ch, ch)
        emit(cv_ref[pl.ds(r0 - n_halo * n_str, ch + n_halo * n_str), :], r0)
        return carry

    lax.fori_loop(1, rows // ch, out_chunk, 0)
    for j in range(n_halo):
        step_rows = cv_ref[(n_steps - n_halo + j) * n_str:(n_steps - n_halo + j + 1) * n_str, :]
        nb_ref[j] = step_rows[n_str - 1:, :] if chained else step_rows


def _sconv_core(proj, conv_w, buf, prev, *, row0, n_seq, n_str, n_steps, chained):
    d = proj.shape[1] // 3
    lw = 256
    rows = n_str * n_steps
    rb0 = row0 // rows
    ns = 1 if chained else n_str
    nl = d // lw
    body = functools.partial(_sconv_body, n_str=n_str, n_steps=n_steps, chained=chained)
    buf_spec = pl.BlockSpec((None, SCONV_W - 1, ns, lw), lambda s, n: (s, 0, 0, n))
    in_specs = [
        pl.BlockSpec((rows, lw), lambda s, n: (rb0 + s, n)),
        pl.BlockSpec((rows, lw), lambda s, n: (rb0 + s, nl + n)),
        pl.BlockSpec((rows, lw), lambda s, n: (rb0 + s, 2 * nl + n)),
        pl.BlockSpec((SCONV_W, lw), lambda s, n: (0, n)),
        buf_spec,
    ]
    args = [proj, proj, proj, conv_w, buf]
    body, args, in_specs, aliases = _shared_output(body, args, in_specs, prev)
    return pl.pallas_call(
        body,
        grid=(n_seq, nl),
        in_specs=in_specs,
        out_specs=(pl.BlockSpec((rows, lw), lambda s, n: (rb0 + s, n)), buf_spec),
        out_shape=(
            jax.ShapeDtypeStruct((proj.shape[0], d), bf16),
            jax.ShapeDtypeStruct((n_seq, SCONV_W - 1, ns, d), f32),
        ),
        scratch_shapes=[pltpu.VMEM((rows, lw), f32)],
        input_output_aliases=aliases,
        compiler_params=_cparams(2),
        name="sconv_core",
    )(*args)


class _Group:
    def __init__(self, row0, n_batch, t_seq, offset, has_past):
        self.row0, self.n_batch, self.t_seq, self.offset, self.has_past = row0, n_batch, t_seq, offset, has_past
        self.chained = not has_past
        if self.chained:
            self.n_seq, self.n_str, self.n_steps = n_batch, PROMPT_STREAMS, t_seq // PROMPT_STREAMS
        else:
            self.n_seq, self.n_str, self.n_steps = 1, n_batch, t_seq
        self.n_rows = n_batch * t_seq

    def to_streams(self, x):
        f = x.shape[-1]
        if self.chained:
            return x.reshape(self.n_batch, self.n_str, self.n_steps, f).transpose(0, 2, 1, 3).reshape(self.n_rows, f)
        return x.reshape(self.n_batch, self.t_seq, f).transpose(1, 0, 2).reshape(self.n_rows, f)

    def to_natural(self, x):
        f = x.shape[-1]
        if self.chained:
            return x.reshape(self.n_batch, self.n_steps, self.n_str, f).transpose(0, 2, 1, 3).reshape(self.n_rows, f)
        return x.reshape(self.t_seq, self.n_batch, f).transpose(1, 0, 2).reshape(self.n_rows, f)

    def last_steps(self, x, k, c0, c1):
        step_rows = self.n_rows // self.n_steps
        r0 = self.row0 + (self.n_steps - k) * step_rows
        if self.chained:
            rows = [x[self.row0 + b * self.t_seq + (self.n_steps - k + j) * self.n_str + self.n_str - 1, c0:c1]
                    for b in range(self.n_batch) for j in range(k)]
            return jnp.stack(rows).reshape(self.n_batch, k, c1 - c0)
        return x[r0:r0 + k * step_rows, c0:c1].reshape(k, self.n_batch, c1 - c0).transpose(1, 0, 2)

    def state_in(self, s, width):
        if s is None:
            return None
        if s.ndim == 2:
            return s.astype(f32)[None]
        return s.astype(f32).transpose(1, 0, 2)[None]

    def zeros_state(self, k, width):
        shape = (self.n_seq, 1, width) if k is None else (self.n_seq, k, 1, width)
        return jnp.zeros(shape, f32)

    def state_out(self, s):
        return s.reshape(self.n_batch, -1)


def _relayout(x, groups, fn):
    return jnp.concatenate([fn(g, x[g.row0:g.row0 + g.n_rows]) for g in groups], axis=0)


def kernel(x_prompt, x_sample, cache_attn_k, cache_attn_v, cache_idx_k, state_lru_conv, state_lru_h, state_s5_re, state_s5_im, state_sconv, norm_mix, norm_ffn, attn_w_in, attn_g_q, attn_g_k, attn_g_ki, attn_w_o, lru_w_in, lru_conv_w, lru_conv_b, lru_w_a, lru_b_a, lru_w_x, lru_b_x, lru_lambda, lru_w_out, s5_a_re, s5_a_im, s5_log_dt, s5_b_re, s5_b_im, s5_c_re, s5_c_im, s5_d, s5_w_glu, sc_w_in, sc_conv_w, sc_w_out, ffn_w_up, ffn_w_down):
    bp, tp, d = x_prompt.shape
    bs, ts, _ = x_sample.shape
    n_past = cache_attn_k.shape[2]
    depth = norm_mix.shape[0]
    n_mix = 4
    groups = [_Group(0, bp, tp, 0, False), _Group(bp * tp, bs, ts, n_past, True)]
    x = jnp.concatenate([x_prompt.reshape(bp * tp, d), x_sample.reshape(bs * ts, d)], axis=0).astype(f32)
    layout = "natural"
    outs = [[[] for _ in range(8)] for _ in groups]

    def want(target, x, layout):
        if layout == target:
            return x
        return _relayout(x, groups, _Group.to_streams if target == "streams" else _Group.to_natural)

    w_up_bf, w_down_bf = ffn_w_up.astype(bf16), ffn_w_down.astype(bf16)
    for i in range(depth):
        m, r = i % n_mix, i // n_mix
        if m == 0:
            x = want("natural", x, layout)
            layout = "natural"
            past = [None, (cache_attn_k, cache_attn_v, cache_idx_k, r)]
            att_groups = [(g.row0, g.n_batch, g.t_seq, g.offset) for g in groups]
            x, k, v, ki = _attn_layer(x, att_groups, past, norm_mix[i], attn_w_in[r], attn_g_q[r], attn_g_k[r],
                                      attn_g_ki[r], attn_w_o[r])
            for gi, g in enumerate(groups):
                sl = slice(g.row0, g.row0 + g.n_rows)
                outs[gi][0].append(k[sl].reshape(g.n_batch, g.t_seq, N_KV_HEADS, HEAD_DIM))
                outs[gi][1].append(v[sl].reshape(g.n_batch, g.t_seq, N_KV_HEADS, HEAD_DIM))
                outs[gi][2].append(ki[sl].reshape(g.n_batch, g.t_seq, IDX_DIM))
        else:
            x = want("streams", x, layout)
            layout = "streams"
            if m == 1:
                w = lru_w_in.shape[2] // 2
                proj = _matmul(x, [lru_w_in[r].astype(bf16)], gain=norm_mix[i], name="lru_in")
                params = (lru_conv_w[r], lru_conv_b[r], lru_w_a[r], lru_b_a[r], lru_w_x[r], lru_b_x[r], lru_lambda[r])
                y = None
                for gi, g in enumerate(groups):
                    if g.has_past:
                        buf = g.state_in(state_lru_conv[r], w)
                        h0 = g.state_in(state_lru_h[r], w)
                    else:
                        buf, h0 = g.zeros_state(LRU_CONV - 1, w), g.zeros_state(None, w)
                    y, hl = _lru_core(proj, params, buf, h0, y, row0=g.row0, n_seq=g.n_seq, n_str=g.n_str,
                                      n_steps=g.n_steps, chained=g.chained)
                    outs[gi][3].append(g.last_steps(proj, LRU_CONV - 1, w, 2 * w))
                    outs[gi][4].append(g.state_out(hl))
                x = _matmul(y, [lru_w_out[r].astype(bf16)], residual=x, name="lru_out")
            elif m == 2:
                u = _norm(x, norm_mix[i])
                mats = _s5_matrices(s5_a_re[r], s5_a_im[r], s5_log_dt[r], s5_b_re[r], s5_b_im[r], s5_c_re[r],
                                    s5_c_im[r], s5_d[r])
                gp = s5_a_re.shape[1] * s5_a_re.shape[2]
                y = None
                for gi, g in enumerate(groups):
                    if g.has_past:
                        s_re = g.state_in(state_s5_re[r].reshape(g.n_batch, gp), gp)
                        s_im = g.state_in(state_s5_im[r].reshape(g.n_batch, gp), gp)
                    else:
                        s_re, s_im = g.zeros_state(None, gp), g.zeros_state(None, gp)
                    y, o_re, o_im = _s5_core(u, mats, s_re, s_im, y, row0=g.row0, n_seq=g.n_seq, n_str=g.n_str,
                                             n_steps=g.n_steps, chained=g.chained)
                    outs[gi][5].append(g.state_out(o_re).reshape(g.n_batch, s5_a_re.shape[1], s5_a_re.shape[2]))
                    outs[gi][6].append(g.state_out(o_im).reshape(g.n_batch, s5_a_re.shape[1], s5_a_re.shape[2]))
                wg = s5_w_glu[r].astype(bf16)
                x = _matmul(y, [wg[:, :d], wg[:, d:]], residual=x, glu=True, name="s5_glu")
            else:
                proj = _matmul(x, [sc_w_in[r].astype(bf16)], gain=norm_mix[i], name="sconv_in")
                u = None
                for gi, g in enumerate(groups):
                    buf = g.state_in(state_sconv[r], d) if g.has_past else g.zeros_state(SCONV_W - 1, d)
                    u, nb = _sconv_core(proj, sc_conv_w[r], buf, u, row0=g.row0, n_seq=g.n_seq, n_str=g.n_str,
                                        n_steps=g.n_steps, chained=g.chained)
                    outs[gi][7].append(nb.transpose(0, 2, 1, 3).reshape(g.n_batch, SCONV_W - 1, d))
                x = _matmul(u, [sc_w_out[r].astype(bf16)], residual=x, name="sconv_out")
        x = _ffn(x, norm_ffn[i], w_up_bf, w_down_bf, i)

    x = want("natural", x, layout)
    y_prompt = x[:bp * tp].reshape(bp, tp, d)
    y_sample = x[bp * tp:].reshape(bs, ts, d)
    st = [[jnp.stack(l) for l in outs[gi]] for gi in range(2)]
    return (y_prompt, y_sample, *st[0], *st[1])
```

```python
import functools
import math

import jax
import jax.numpy as jnp
from jax import lax
from jax.experimental import pallas as pl
from jax.experimental.pallas import tpu as pltpu

f32 = jnp.float32
bf16 = jnp.bfloat16
i32 = jnp.int32

LANES = 128
SUBLANES = 8
VMEM_LIMIT = 56 * 1024 * 1024

CHUNK = 64
EPS = 1e-6
N_HEADS = 16
N_KV_HEADS = 4
HEAD_DIM = 128
GQA_GROUP = N_HEADS // N_KV_HEADS
ROT_DIM = HEAD_DIM // 4
ROPE_THETA = 500000.0
ATTN_SCALE = HEAD_DIM ** -0.5
N_IDX_HEADS = 16
IDX_DIM = 64
IDX_ROT_DIM = IDX_DIM // 4
TOPK_MAX = 256
LRU_BLOCK = 128
LRU_CONV = 4
LRU_C = 8.0
S5_GROUP = 16
S5_STATE = 64
SCONV_W = 3
PROMPT_STREAMS = 8

NEG_BIG = -1e30
INT_MIN = -(2 ** 31)
LOG2E = math.log2(math.e)


def _cparams(n_axes):
    return pltpu.CompilerParams(dimension_semantics=("arbitrary",) * n_axes, vmem_limit_bytes=VMEM_LIMIT)


def _pick(n, cands):
    for c in cands:
        if n % c == 0:
            return c
    raise ValueError(f"no tile in {cands} divides {n}")


def _rmsnorm_rows(x, g):
    ms = jnp.mean(x * x, axis=-1, keepdims=True)
    return x * lax.rsqrt(ms + EPS) * g


def _mm_body(*refs, norm, n_w, has_res, glu):
    it = iter(refs)
    x_ref = next(it)
    g_ref = next(it) if norm else None
    w_refs = [next(it) for _ in range(n_w)]
    r_ref = next(it) if has_res else None
    o_ref = next(it)
    h_ref = next(it) if norm else None

    if norm:
        @pl.when(pl.program_id(1) == 0)
        def _():
            h_ref[...] = _rmsnorm_rows(x_ref[...], g_ref[...]).astype(bf16)
        h = h_ref[...]
    else:
        h = x_ref[...]
    acc = jnp.dot(h, w_refs[0][...], preferred_element_type=f32)
    if glu:
        acc = acc * jax.nn.sigmoid(jnp.dot(h, w_refs[1][...], preferred_element_type=f32))
    if has_res:
        acc = acc + r_ref[...]
    o_ref[...] = acc.astype(o_ref.dtype)


def _matmul(x, ws, *, gain=None, residual=None, out_dtype=f32, glu=False, name="mm"):
    n, k = x.shape
    m = ws[0].shape[1]
    norm = gain is not None
    tm = _pick(n, (1024, 512, 256, 128, 64))
    tn = _pick(m, (1024, 768, 512, 256, 128))
    in_specs = [pl.BlockSpec((tm, k), lambda i, j: (i, 0))]
    args = [x]
    if norm:
        in_specs.append(pl.BlockSpec((1, k), lambda i, j: (0, 0)))
        args.append(gain.reshape(1, k).astype(f32))
    for w in ws:
        in_specs.append(pl.BlockSpec((k, tn), lambda i, j: (0, j)))
        args.append(w)
    if residual is not None:
        in_specs.append(pl.BlockSpec((tm, tn), lambda i, j: (i, j)))
        args.append(residual)
    body = functools.partial(_mm_body, norm=norm, n_w=len(ws), has_res=residual is not None, glu=glu)
    return pl.pallas_call(
        body,
        grid=(n // tm, m // tn),
        in_specs=in_specs,
        out_specs=pl.BlockSpec((tm, tn), lambda i, j: (i, j)),
        out_shape=jax.ShapeDtypeStruct((n, m), out_dtype),
        scratch_shapes=[pltpu.VMEM((tm, k), bf16)] if norm else [],
        compiler_params=_cparams(2),
        name=name,
    )(*args)


def _ffn_body(x_ref, g_ref, wu_ref, wd_ref, o_ref, h_ref):
    @pl.when(pl.program_id(1) == 0)
    def _():
        x = x_ref[...]
        h_ref[...] = _rmsnorm_rows(x, g_ref[...]).astype(bf16)
        o_ref[...] = x

    u = jnp.dot(h_ref[...], wu_ref[...], preferred_element_type=f32)
    u = jnp.square(jnp.maximum(u, 0.0)).astype(bf16)
    o_ref[...] += jnp.dot(u, wd_ref[...], preferred_element_type=f32)


def _ffn(x, gain, w_up, w_down, layer, *, streams=None, in_streams=False, out_streams=False):
    n, d = x.shape
    ff = w_up.shape[2]
    tf = _pick(ff, (1024, 512, 256, 128))
    if streams is None or not (in_streams or out_streams):
        tm = _pick(n, (512, 256, 128, 64))
        nat_map = lambda i, j: (i, 0)
        str_map = nat_map
        str_shape = (n, d)
    else:
        n_str, n_steps = streams
        tm = _pick(n_steps, (512, 256, 128, 64))
        parts = n_steps // tm
        nat_map = lambda i, j: (i, 0)
        str_map = lambda i, j: ((i // (n_str * parts)) * parts + i % parts, (i // parts) % n_str)
        str_shape = (n // n_str, n_str * d)
    x_in = x.reshape(str_shape) if in_streams else x
    out_shape = str_shape if out_streams else (n, d)
    out = pl.pallas_call(
        _ffn_body,
        grid=(n // tm, ff // tf),
        in_specs=[
            pl.BlockSpec((tm, d), str_map if in_streams else nat_map),
            pl.BlockSpec((1, d), lambda i, j: (0, 0)),
            pl.BlockSpec((None, d, tf), lambda i, j: (layer, 0, j)),
            pl.BlockSpec((None, tf, d), lambda i, j: (layer, j, 0)),
        ],
        out_specs=pl.BlockSpec((tm, d), str_map if out_streams else nat_map),
        out_shape=jax.ShapeDtypeStruct(out_shape, f32),
        scratch_shapes=[pltpu.VMEM((tm, d), bf16)],
        compiler_params=_cparams(2),
        name="ffn",
    )(x_in, gain.reshape(1, d).astype(f32), w_up, w_down)
    return out.reshape(n, d)


def _norm_body(x_ref, g_ref, o_ref):
    o_ref[...] = _rmsnorm_rows(x_ref[...], g_ref[...])


def _norm(x, gain):
    n, d = x.shape
    tm = _pick(n, (512, 256, 128, 64))
    return pl.pallas_call(
        _norm_body,
        grid=(n // tm,),
        in_specs=[pl.BlockSpec((tm, d), lambda i: (i, 0)), pl.BlockSpec((1, d), lambda i: (0, 0))],
        out_specs=pl.BlockSpec((tm, d), lambda i: (i, 0)),
        out_shape=jax.ShapeDtypeStruct((n, d), f32),
        compiler_params=_cparams(1),
        name="rmsnorm",
    )(x, gain.reshape(1, d).astype(f32))


def _rope_tables(pos, width, rot):
    half = rot // 2
    inv = ROPE_THETA ** (-jnp.arange(half, dtype=f32) / half)
    ang = pos.astype(f32)[:, None] * inv[None, :]
    cos, sin = jnp.cos(ang), jnp.sin(ang)
    n = pos.shape[0]
    pad = jnp.zeros((n, width - rot), f32)
    zero = jnp.zeros((n, half), f32)
    c = jnp.concatenate([cos, cos, pad + 1.0], axis=1)
    s_up = jnp.concatenate([-sin, zero, pad], axis=1)
    s_dn = jnp.concatenate([zero, sin, pad], axis=1)
    reps = LANES // width
    return tuple(jnp.tile(t, (1, reps)) for t in (c, s_up, s_dn))


def _rope(x, c, s_up, s_dn, half):
    return x * c + pltpu.roll(x, LANES - half, axis=1) * s_up + pltpu.roll(x, half, axis=1) * s_dn


def _attn_post_body(q_ref, k_ref, qi_ref, kw_ref, c_ref, su_ref, sd_ref, ci_ref, sui_ref, sdi_ref,
                    gq_ref, gk_ref, gki_ref, qo_ref, ko_ref, qio_ref, kio_ref):
    c, su, sd = c_ref[...], su_ref[...], sd_ref[...]
    ci, sui, sdi = ci_ref[...], sui_ref[...], sdi_ref[...]
    for h in range(N_HEADS):
        sl = slice(h * HEAD_DIM, (h + 1) * HEAD_DIM)
        y = _rope(_rmsnorm_rows(q_ref[:, sl], gq_ref[...]), c, su, sd, ROT_DIM // 2)
        qo_ref[:, sl] = y.astype(qo_ref.dtype)
    for h in range(N_KV_HEADS):
        sl = slice(h * HEAD_DIM, (h + 1) * HEAD_DIM)
        ko_ref[:, sl] = _rope(_rmsnorm_rows(k_ref[:, sl], gk_ref[...]), c, su, sd, ROT_DIM // 2)
    for j in range(N_IDX_HEADS // 2):
        y = _rope(qi_ref[:, j * LANES:(j + 1) * LANES], ci, sui, sdi, IDX_ROT_DIM // 2).astype(qio_ref.dtype)
        qio_ref[2 * j] = y[:, :IDX_DIM]
        qio_ref[2 * j + 1] = y[:, IDX_DIM:]
    kw = kw_ref[:, :LANES]
    lane = lax.broadcasted_iota(i32, kw.shape, 1)
    ms = jnp.sum(jnp.where(lane < IDX_DIM, kw * kw, 0.0), axis=-1, keepdims=True) * (1.0 / IDX_DIM)
    y = _rope(kw * lax.rsqrt(ms + EPS) * gki_ref[...], ci, sui, sdi, IDX_ROT_DIM // 2)
    kio_ref[...] = y[:, :IDX_DIM]


def _attn_post(proj, pos, g_q, g_k, g_ki):
    n = proj.shape[0]
    tm = _pick(n, (256, 128, 64))
    q_w, kv_w, iq_w = N_HEADS * HEAD_DIM, N_KV_HEADS * HEAD_DIM, N_IDX_HEADS * IDX_DIM
    tabs = _rope_tables(pos, HEAD_DIM, ROT_DIM) + _rope_tables(pos, IDX_DIM, IDX_ROT_DIM)
    gki = jnp.concatenate([g_ki.astype(f32), jnp.zeros((LANES - IDX_DIM,), f32)]).reshape(1, LANES)
    row = lambda i: (i, 0)
    const = lambda i: (0, 0)
    in_specs = [
        pl.BlockSpec((tm, q_w), row),
        pl.BlockSpec((tm, kv_w), lambda i: (i, q_w // kv_w)),
        pl.BlockSpec((tm, iq_w), lambda i: (i, (q_w + 2 * kv_w) // iq_w)),
        pl.BlockSpec((tm, kv_w), lambda i: (i, (q_w + 2 * kv_w + iq_w) // kv_w)),
    ] + [pl.BlockSpec((tm, LANES), row)] * 6 + [pl.BlockSpec((1, LANES), const)] * 3
    out_shape = (
        jax.ShapeDtypeStruct((n, q_w), bf16),
        jax.ShapeDtypeStruct((n, kv_w), f32),
        jax.ShapeDtypeStruct((N_IDX_HEADS, n, IDX_DIM), bf16),
        jax.ShapeDtypeStruct((n, IDX_DIM), f32),
    )
    out_specs = (
        pl.BlockSpec((tm, q_w), row),
        pl.BlockSpec((tm, kv_w), row),
        pl.BlockSpec((N_IDX_HEADS, tm, IDX_DIM), lambda i: (0, i, 0)),
        pl.BlockSpec((tm, IDX_DIM), row),
    )
    return pl.pallas_call(
        _attn_post_body,
        grid=(n // tm,),
        in_specs=in_specs,
        out_specs=out_specs,
        out_shape=out_shape,
        compiler_params=_cparams(1),
        name="attn_post",
    )(proj, proj, proj, proj, *tabs, g_q.reshape(1, -1).astype(f32), g_k.reshape(1, -1).astype(f32), gki)


def _sortable(x):
    b = pltpu.bitcast(x, i32)
    return b ^ ((b >> 31) & 0x7FFFFFFF)


def _attn_core_body(*refs, segs, tq, n_sel, causal):
    n_seg = len(segs)
    n_pack = LANES // tq
    seg_lens = [n for n, _ in segs]
    it = iter(refs)
    qi_refs = [next(it) for _ in range(n_pack)]
    wi_ref, q_ref = next(it), next(it)
    seg_refs = [([next(it) for _ in range(n_pack)], next(it), next(it)) for _ in range(n_seg)]
    o_ref = next(it)
    keys_ref, bias_ref, sc_ref, acc_ref = (next(it) for _ in range(4))

    l_tot = sum(seg_lens)
    offs = [sum(seg_lens[:s]) for s in range(n_seg)]
    wq = GQA_GROUP * tq
    q_tile = pl.program_id(1)
    in_pack = pl.program_id(2) // N_KV_HEADS
    g_head = pl.program_id(2) % N_KV_HEADS
    if causal:
        assert n_seg == 1 and n_pack == 1
        gran = _pick(l_tot, (1024, 512))
        n_gran = jnp.minimum(((q_tile + 1) * tq + gran - 1) // gran, l_tot // gran)
        blk = 512
        n_blk = n_gran * (gran // blk)
    else:
        gran = None
        blk = _pick(l_tot, (512, 320, 256, 192, 128, 64))
        n_blk = l_tot // blk

    def n_chunks(s, lc):
        return n_gran * (gran // lc) if causal else seg_lens[s] // lc

    @pl.when(pl.program_id(2) == 0)
    def _select():
        wi_rows = [wi_ref[h:h + 1, :] for h in range(N_IDX_HEADS)]
        qi_rows = []
        for h in range(N_IDX_HEADS):
            for w in range(n_pack):
                qh = qi_refs[w][h]
                parts = [qh if v == w else jnp.zeros_like(qh) for v in range(n_pack)]
                qi_rows.append(jnp.concatenate(parts, axis=1) if n_pack > 1 else qh)
        qi_all = jnp.concatenate(qi_rows, axis=0)

        for s in range(n_seg):
            ki_refs = seg_refs[s][0]
            lc = _pick(seg_lens[s], (256, 128, 64))

            def score_chunk(c, carry, ki_refs=ki_refs, lc=lc, off=offs[s]):
                r0 = pl.multiple_of(c * lc, lc)
                kic = [r[pl.ds(r0, lc), :].astype(bf16) for r in ki_refs]
                kic = jnp.concatenate(kic, axis=1) if n_pack > 1 else kic[0]
                sc = lax.dot_general(kic, qi_all, (((1,), (1,)), ((), ())), preferred_element_type=f32)
                acc = jnp.zeros((lc, LANES), f32)
                for h in range(N_IDX_HEADS):
                    acc = acc + jnp.maximum(sc[:, h * LANES:(h + 1) * LANES], 0.0) * wi_rows[h]
                key = _sortable(acc)
                if causal:
                    kpos = r0 + lax.broadcasted_iota(i32, (lc, LANES), 0)
                    qpos = q_tile * tq + lax.broadcasted_iota(i32, (lc, LANES), 1)
                    key = jnp.where((kpos // CHUNK) <= (qpos // CHUNK), key, INT_MIN)
                keys_ref[pl.ds(pl.multiple_of(off + r0, CHUNK), lc), :] = key
                return carry

            lax.fori_loop(0, n_chunks(s, lc), score_chunk, 0)

        def count(pred):
            def body(i, acc):
                r0 = pl.multiple_of(i * blk, blk)
                hit = jnp.where(pred(keys_ref[pl.ds(r0, blk), :], r0), 1, 0)
                return acc + jnp.sum(hit.reshape(blk // SUBLANES, SUBLANES, LANES), axis=0)
            acc = lax.fori_loop(0, n_blk, body, jnp.zeros((SUBLANES, LANES), i32))
            return jnp.sum(acc, axis=0, keepdims=True)

        def count_ge(cand):
            return count(lambda k, r0: k >= cand)

        ans = jnp.where(count_ge(jnp.zeros((1, LANES), i32)) >= n_sel, 0, INT_MIN)

        def bit_step(i, ans):
            cand = ans | (1 << (30 - i))
            return jnp.where(count_ge(cand) >= n_sel, cand, ans)

        ans = lax.fori_loop(0, 31, bit_step, ans)
        tau = jnp.maximum(ans, INT_MIN + 1)
        n_ge = count_ge(tau)
        ties = jnp.max(n_ge) > n_sel

        @pl.when(jnp.logical_not(ties))
        def _():
            def body(i, carry):
                r0 = pl.multiple_of(i * blk, blk)
                keep = keys_ref[pl.ds(r0, blk), :] >= tau
                bias_ref[pl.ds(r0, blk), :] = jnp.where(keep, 0.0, NEG_BIG).astype(bias_ref.dtype)
                return carry
            lax.fori_loop(0, n_blk, body, 0)

        @pl.when(ties)
        def _():
            def row_idx(r0):
                return r0 + lax.broadcasted_iota(i32, (blk, LANES), 0)
            need = n_sel - count(lambda k, r0: k > tau)

            def idx_step(i, m):
                cand = m | (1 << (14 - i))
                n_below = count(lambda k, r0: jnp.logical_and(k == tau, row_idx(r0) < cand))
                return jnp.where(n_below <= need, cand, m)

            m_cut = lax.fori_loop(0, 15, idx_step, jnp.zeros((1, LANES), i32))

            def body(i, carry):
                r0 = pl.multiple_of(i * blk, blk)
                k = keys_ref[pl.ds(r0, blk), :]
                keep = jnp.logical_or(k > tau, jnp.logical_and(k == tau, row_idx(r0) < m_cut))
                bias_ref[pl.ds(r0, blk), :] = jnp.where(keep, 0.0, NEG_BIG).astype(bias_ref.dtype)
                return carry
            lax.fori_loop(0, n_blk, body, 0)

    qg = jnp.concatenate([q_ref[:, j * HEAD_DIM:(j + 1) * HEAD_DIM] for j in range(GQA_GROUP)], axis=0)
    bias_lane = in_pack * tq + lax.broadcasted_iota(i32, (wq, LANES), 0) % tq
    one_hot = (lax.broadcasted_iota(i32, (wq, LANES), 1) == bias_lane).astype(bf16)
    q_aug = jnp.concatenate([qg, one_hot], axis=1)
    c_exp = ATTN_SCALE * LOG2E

    def read_kv(ref, interleaved, r0, lc):
        if interleaved:
            return ref[pl.ds(r0 * N_KV_HEADS + g_head, lc, stride=N_KV_HEADS), :].astype(bf16)
        return ref[pl.ds(r0, lc), :].astype(bf16)

    def fold_rows(x, op):
        return op(x.reshape(x.shape[0] // SUBLANES, SUBLANES, x.shape[1]), axis=0)

    m_acc = jnp.full((SUBLANES, wq), NEG_BIG, f32)
    for s in range(n_seg):
        k_ref = seg_refs[s][1]
        lc = _pick(seg_lens[s], (1024, 512, 256, 128, 64))

        def logits_chunk(c, m_acc, k_ref=k_ref, lc=lc, off=offs[s], inter=segs[s][1]):
            r0 = pl.multiple_of(c * lc, lc)
            row = pl.multiple_of(off + r0, CHUNK)
            k_aug = jnp.concatenate([read_kv(k_ref, inter, r0, lc), bias_ref[pl.ds(row, lc), :]], axis=1)
            sc = lax.dot_general(k_aug, q_aug, (((1,), (1,)), ((), ())), preferred_element_type=f32)
            sc_ref[pl.ds(row, lc), :] = sc
            return jnp.maximum(m_acc, fold_rows(sc, jnp.max))

        m_acc = lax.fori_loop(0, n_chunks(s, lc), logits_chunk, m_acc)
    m_row = jnp.max(m_acc, axis=0, keepdims=True)

    acc_ref[...] = jnp.zeros(acc_ref.shape, f32)
    l_acc = jnp.zeros((SUBLANES, wq), f32)
    for s in range(n_seg):
        v_ref = seg_refs[s][2]
        lc = _pick(seg_lens[s], (1024, 512, 256, 128, 64))

        def value_chunk(c, l_acc, v_ref=v_ref, lc=lc, off=offs[s], inter=segs[s][1]):
            r0 = pl.multiple_of(c * lc, lc)
            row = pl.multiple_of(off + r0, CHUNK)
            p = jnp.exp2((sc_ref[pl.ds(row, lc), :] - m_row) * c_exp)
            vc = read_kv(v_ref, inter, r0, lc)
            acc_ref[...] += lax.dot_general(vc, p.astype(bf16), (((0,), (0,)), ((), ())), preferred_element_type=f32)
            return l_acc + fold_rows(p, jnp.sum)

        l_acc = lax.fori_loop(0, n_chunks(s, lc), value_chunk, l_acc)

    o_t = (acc_ref[...] / jnp.sum(l_acc, axis=0, keepdims=True)).T
    for j in range(GQA_GROUP):
        o_ref[:, j * HEAD_DIM:(j + 1) * HEAD_DIM] = o_t[j * tq:(j + 1) * tq, :].astype(o_ref.dtype)


def _attn_core(q, qi, wi_t, segs, *, n_batch, t_seq, n_sel, causal):
    n = q.shape[0]
    tq = min(LANES, t_seq)
    n_pack = LANES // tq
    assert n_batch % n_pack == 0 and t_seq % tq == 0
    nq = t_seq // tq
    wq = GQA_GROUP * tq
    kv_w = GQA_GROUP * HEAD_DIM
    batch_of = lambda bp, e: bp * n_pack + e // N_KV_HEADS
    head_of = lambda e: e % N_KV_HEADS
    in_specs, args = [], []
    for w in range(n_pack):
        in_specs.append(pl.BlockSpec((N_IDX_HEADS, tq, IDX_DIM), lambda bp, i, e, w=w: (0, (bp * n_pack + w) * nq + i, 0)))
        args.append(qi)
    in_specs += [
        pl.BlockSpec((None, N_IDX_HEADS, LANES), lambda bp, i, e: (bp, 0, i * tq // LANES)),
        pl.BlockSpec((tq, kv_w), lambda bp, i, e: (batch_of(bp, e) * nq + i, head_of(e))),
    ]
    args += [wi_t, q]
    seg_meta = []
    for ki, k, v, n_keys, imap, interleaved in segs:
        lead = (None,) * (ki.ndim - 2)
        kv_rows = n_keys * N_KV_HEADS if interleaved else n_keys
        kv_map = lambda bp, i, e, imap=imap, inter=interleaved: imap(batch_of(bp, e), 0 if inter else head_of(e))
        for w in range(n_pack):
            in_specs.append(pl.BlockSpec(lead + (n_keys, IDX_DIM), lambda bp, i, e, imap=imap, w=w: imap(bp * n_pack + w, 0)))
            args.append(ki)
        in_specs += [pl.BlockSpec(lead + (kv_rows, HEAD_DIM), kv_map)] * 2
        args += [k, v]
        seg_meta.append((n_keys, interleaved))
    l_tot = sum(m[0] for m in seg_meta)
    body = functools.partial(_attn_core_body, segs=tuple(seg_meta), tq=tq, n_sel=n_sel, causal=causal)
    return pl.pallas_call(
        body,
        grid=(n_batch // n_pack, nq, n_pack * N_KV_HEADS),
        in_specs=in_specs,
        out_specs=pl.BlockSpec((tq, kv_w), lambda bp, i, e: (batch_of(bp, e) * nq + i, head_of(e))),
        out_shape=jax.ShapeDtypeStruct((n, N_HEADS * HEAD_DIM), bf16),
        scratch_shapes=[
            pltpu.VMEM((l_tot, LANES), i32),
            pltpu.VMEM((l_tot, LANES), bf16),
            pltpu.VMEM((l_tot, wq), f32),
            pltpu.VMEM((HEAD_DIM, wq), f32),
        ],
        compiler_params=_cparams(3),
        name="attn_core",
    )(*args)


def _attn_group(q, qi, wi, ki_new, k_new, v_new, past, *, n_batch, t_seq, offset):
    tq = min(LANES, t_seq)
    n_pack = LANES // tq
    wi_g = wi.reshape(n_batch // n_pack, n_pack, t_seq, N_IDX_HEADS).transpose(0, 3, 1, 2)
    wi_g = wi_g.reshape(n_batch // n_pack, N_IDX_HEADS, n_pack * t_seq)
    ki_g = ki_new.reshape(n_batch, t_seq, IDX_DIM)
    k_g = k_new.reshape(n_batch, t_seq, N_KV_HEADS * HEAD_DIM)
    v_g = v_new.reshape(n_batch, t_seq, N_KV_HEADS * HEAD_DIM)
    n_sel = min(TOPK_MAX, (offset + t_seq) // 4)
    new_map = lambda b, hb: (b, 0, hb)

    if past is None:
        segs = [(ki_g, k_g, v_g, t_seq, new_map, False)]
        return _attn_core(q, qi, wi_g, segs, n_batch=n_batch, t_seq=t_seq, n_sel=n_sel, causal=True)

    assert offset % CHUNK == 0 and t_seq <= CHUNK
    pk, pv, pki, r = past
    n_past = pk.shape[2]
    pk = pk.reshape(pk.shape[0], n_batch, n_past * N_KV_HEADS, HEAD_DIM)
    pv = pv.reshape(pv.shape[0], n_batch, n_past * N_KV_HEADS, HEAD_DIM)
    past_map = lambda b, hb: (r, b, 0, 0)
    segs = [(pki, pk, pv, n_past, past_map, True), (ki_g, k_g, v_g, t_seq, new_map, False)]
    return _attn_core(q, qi, wi_g, segs, n_batch=n_batch, t_seq=t_seq, n_sel=n_sel, causal=False)


def _attn_layer(x, group, past, gain, w_in_p, g_q, g_k, g_ki, w_o):
    q_w, kv_w, iq_w = N_HEADS * HEAD_DIM, N_KV_HEADS * HEAD_DIM, N_IDX_HEADS * IDX_DIM
    proj = _matmul(x, [w_in_p], gain=gain, name="attn_in")
    pos = group.offset + jnp.tile(jnp.arange(group.t_seq), group.n_batch)
    q, k, qi, ki = _attn_post(proj, pos, g_q, g_k, g_ki)
    v = proj[:, q_w + kv_w:q_w + 2 * kv_w]
    wi = proj[:, q_w + 2 * kv_w + iq_w + IDX_DIM:q_w + 2 * kv_w + iq_w + IDX_DIM + N_IDX_HEADS]
    o = _attn_group(q, qi, wi, ki, k, v, past, n_batch=group.n_batch, t_seq=group.t_seq, offset=group.offset)
    x = _matmul(o, [w_o], residual=x, name="attn_out")
    return x, k, v, ki


def _halo_rows(x_ref, init, j, *, n_str, n_steps, chained):
    if not chained:
        return init
    prev = x_ref[(n_steps - j) * n_str:(n_steps - j + 1) * n_str, :]
    shifted = pltpu.roll(prev, 1, axis=0)
    first = lax.broadcasted_iota(i32, shifted.shape, 0) == 0
    return jnp.where(first, init, shifted)


def _chain_carries(h_end, p_end, c0):
    rows = [c0]
    for s in range(1, h_end.shape[0]):
        rows.append(p_end[s - 1:s] * rows[-1] + h_end[s - 1:s])
    return jnp.concatenate(rows, axis=0)


def _lru_core_body(xb_ref, gate_ref, cw_ref, cb_ref, wa_ref, ba_ref, wx_ref, bx_ref, lam_ref, buf_ref, h0_ref,
                   y_ref, hl_ref, a_ref, b_ref, *, n_str, n_steps, chained, lw):
    rows = n_str * n_steps
    ch = _pick(n_steps, (16, 8, 4, 2, 1)) * n_str
    n_halo = LRU_CONV - 1
    cw = [cw_ref[j:j + 1, :] for j in range(LRU_CONV)]
    cb = cb_ref[...]
    sp = jax.nn.softplus(-lam_ref[...])
    halo = jnp.concatenate(
        [_halo_rows(xb_ref, buf_ref[n_halo - j], j, n_str=n_str, n_steps=n_steps, chained=chained)
         for j in range(n_halo, 0, -1)], axis=0)

    def gates(xin, r0):
        xc = cb
        for j in range(LRU_CONV):
            xc = xc + xin[j * n_str:j * n_str + ch, :] * cw[j]
        xcb = xc.astype(bf16)
        for t in range(lw // LRU_BLOCK):
            sl = slice(t * LRU_BLOCK, (t + 1) * LRU_BLOCK)
            xt = xcb[:, sl]
            r = jax.nn.sigmoid(jnp.dot(xt, wa_ref[t], preferred_element_type=f32) + ba_ref[:, sl])
            ig = jax.nn.sigmoid(jnp.dot(xt, wx_ref[t], preferred_element_type=f32) + bx_ref[:, sl])
            log_a = (-LRU_C) * r * sp[:, sl]
            th = jnp.tanh(log_a)
            mult = jnp.sqrt(-2.0 * th / (1.0 - th))
            a_ref[pl.ds(r0, ch), sl] = jnp.exp(log_a)
            b_ref[pl.ds(r0, ch), sl] = mult * (ig * xc[:, sl])

    gates(jnp.concatenate([halo, xb_ref[0:ch, :]], axis=0), 0)

    def gate_chunk(c, carry):
        r0 = pl.multiple_of(c * ch, ch)
        gates(xb_ref[pl.ds(r0 - n_halo * n_str, ch + n_halo * n_str), :], r0)
        return carry

    lax.fori_loop(1, rows // ch, gate_chunk, 0)

    def scan_step(t, carry):
        r0 = pl.multiple_of(t * n_str, n_str)
        a = a_ref[pl.ds(r0, n_str), :]
        h = a * carry[0] + b_ref[pl.ds(r0, n_str), :]
        b_ref[pl.ds(r0, n_str), :] = h
        if not chained:
            return (h,)
        p = a * carry[1]
        a_ref[pl.ds(r0, n_str), :] = p
        return (h, p)

    if chained:
        init = (jnp.zeros((n_str, lw), f32), jnp.ones((n_str, lw), f32))
    else:
        init = (h0_ref[...],)
    fin = lax.fori_loop(0, n_steps, scan_step, init, unroll=8)

    if chained:
        carries = _chain_carries(fin[0], fin[1], h0_ref[...])
        c_rep = jnp.concatenate([carries] * (ch // n_str), axis=0)

    def out_chunk(c, carry):
        r0 = pl.multiple_of(c * ch, ch)
        h = b_ref[pl.ds(r0, ch), :]
        if chained:
            h = h + a_ref[pl.ds(r0, ch), :] * c_rep
            b_ref[pl.ds(r0, ch), :] = h
        y_ref[pl.ds(r0, ch), :] = (h * jax.nn.gelu(gate_ref[pl.ds(r0, ch), :])).astype(y_ref.dtype)
        return carry

    lax.fori_loop(0, rows // ch, out_chunk, 0)
    if chained:
        hl_ref[...] = b_ref[rows - 1:rows, :]
    else:
        hl_ref[...] = fin[0]


def _lru_core(proj, params, buf, h0, *, n_seq, n_str, n_steps, chained):
    w = proj.shape[1] // 2
    lw = 256
    rows = n_str * n_steps
    ns = 1 if chained else n_str
    cw, cb, wa, ba, wx, bx, lam = params
    nb = lw // LRU_BLOCK
    vec = lambda s, n: (0, n)
    in_specs = [
        pl.BlockSpec((rows, lw), lambda s, n: (s, w // lw + n)),
        pl.BlockSpec((rows, lw), lambda s, n: (s, n)),
        pl.BlockSpec((LRU_CONV, lw), vec),
        pl.BlockSpec((1, lw), vec),
        pl.BlockSpec((nb, LRU_BLOCK, LRU_BLOCK), lambda s, n: (n, 0, 0)),
        pl.BlockSpec((1, lw), vec),
        pl.BlockSpec((nb, LRU_BLOCK, LRU_BLOCK), lambda s, n: (n, 0, 0)),
        pl.BlockSpec((1, lw), vec),
        pl.BlockSpec((1, lw), vec),
        pl.BlockSpec((None, LRU_CONV - 1, ns, lw), lambda s, n: (s, 0, 0, n)),
        pl.BlockSpec((None, ns, lw), lambda s, n: (s, 0, n)),
    ]
    body = functools.partial(_lru_core_body, n_str=n_str, n_steps=n_steps, chained=chained, lw=lw)
    return pl.pallas_call(
        body,
        grid=(n_seq, w // lw),
        in_specs=in_specs,
        out_specs=(
            pl.BlockSpec((rows, lw), lambda s, n: (s, n)),
            pl.BlockSpec((None, ns, lw), lambda s, n: (s, 0, n)),
        ),
        out_shape=(
            jax.ShapeDtypeStruct((n_seq * rows, w), bf16),
            jax.ShapeDtypeStruct((n_seq, ns, w), f32),
        ),
        scratch_shapes=[pltpu.VMEM((rows, lw), f32), pltpu.VMEM((rows, lw), f32)],
        compiler_params=_cparams(2),
        name="lru_core",
    )(proj, proj, cw, cb.reshape(1, w), wa, ba.reshape(1, w), wx, bx.reshape(1, w), lam.reshape(1, w), buf, h0)


def _s5_core_body(u_ref, bm_ref, cm_ref, are_ref, aim_ref, d_ref, sre_ref, sim_ref,
                  y_ref, ore_ref, oim_ref, hs_ref, p_ref, *, n_str, n_steps, chained):
    rows = n_str * n_steps
    half = hs_ref.shape[1] // 2
    ch = _pick(n_steps, (32, 16, 8, 4, 2, 1)) * n_str

    def proj_chunk(c, carry):
        r0 = pl.multiple_of(c * ch, ch)
        hs_ref[pl.ds(r0, ch), :] = jnp.dot(u_ref[pl.ds(r0, ch), :].astype(bf16), bm_ref[...], preferred_element_type=f32)
        return carry

    lax.fori_loop(0, rows // ch, proj_chunk, 0)

    a_re = jnp.broadcast_to(are_ref[...], (n_str, half))
    a_im = jnp.broadcast_to(aim_ref[...], (n_str, half))

    def scan_step(t, carry):
        r0 = pl.multiple_of(t * n_str, n_str)
        h_re, h_im = carry[0], carry[1]
        n_re = a_re * h_re - a_im * h_im + hs_ref[pl.ds(r0, n_str), :half]
        n_im = a_re * h_im + a_im * h_re + hs_ref[pl.ds(r0, n_str), half:]
        hs_ref[pl.ds(r0, n_str), :half] = n_re
        hs_ref[pl.ds(r0, n_str), half:] = n_im
        if not chained:
            return (n_re, n_im)
        p_re, p_im = carry[2], carry[3]
        q_re = a_re[:1] * p_re - a_im[:1] * p_im
        q_im = a_re[:1] * p_im + a_im[:1] * p_re
        p_ref[pl.ds(t, 1), :half] = q_re
        p_ref[pl.ds(t, 1), half:] = q_im
        return (n_re, n_im, q_re, q_im)

    if chained:
        zero = jnp.zeros((n_str, half), f32)
        init = (zero, zero, jnp.ones((1, half), f32), jnp.zeros((1, half), f32))
    else:
        init = (sre_ref[...], sim_ref[...])
    fin = lax.fori_loop(0, n_steps, scan_step, init, unroll=2)

    if chained:
        pe_re, pe_im = fin[2], fin[3]
        c_re, c_im = [sre_ref[...]], [sim_ref[...]]
        for s in range(1, n_str):
            prev_re, prev_im = c_re[-1], c_im[-1]
            c_re.append(pe_re * prev_re - pe_im * prev_im + fin[0][s - 1:s])
            c_im.append(pe_re * prev_im + pe_im * prev_re + fin[1][s - 1:s])
        c_re = jnp.concatenate(c_re, axis=0)
        c_im = jnp.concatenate(c_im, axis=0)

        def fix_step(t, carry):
            r0 = pl.multiple_of(t * n_str, n_str)
            p_re = jnp.broadcast_to(p_ref[pl.ds(t, 1), :half], (n_str, half))
            p_im = jnp.broadcast_to(p_ref[pl.ds(t, 1), half:], (n_str, half))
            hs_ref[pl.ds(r0, n_str), :half] += p_re * c_re - p_im * c_im
            hs_ref[pl.ds(r0, n_str), half:] += p_re * c_im + p_im * c_re
            return carry

        lax.fori_loop(0, n_steps, fix_step, 0, unroll=4)
        ore_ref[...] = hs_ref[rows - 1:rows, :half]
        oim_ref[...] = hs_ref[rows - 1:rows, half:]
    else:
        ore_ref[...] = fin[0]
        oim_ref[...] = fin[1]

    def out_chunk(c, carry):
        r0 = pl.multiple_of(c * ch, ch)
        u = u_ref[pl.ds(r0, ch), :]
        y = jnp.dot(hs_ref[pl.ds(r0, ch), :].astype(bf16), cm_ref[...], preferred_element_type=f32) + d_ref[...] * u
        y_ref[pl.ds(r0, ch), :] = jax.nn.gelu(y).astype(y_ref.dtype)
        return carry

    lax.fori_loop(0, rows // ch, out_chunk, 0)


def _s5_core(u, mats, s_re, s_im, *, n_seq, n_str, n_steps, chained):
    d = u.shape[1]
    bm, cm, a_re, a_im, dsk = mats
    n_gb = d // LANES
    sw = bm.shape[2] // 2
    rows = n_str * n_steps
    ns = 1 if chained else n_str
    st_spec = pl.BlockSpec((None, ns, sw), lambda s, g: (s, 0, g))
    in_specs = [
        pl.BlockSpec((rows, LANES), lambda s, g: (s, g)),
        pl.BlockSpec((None, LANES, 2 * sw), lambda s, g: (g, 0, 0)),
        pl.BlockSpec((None, 2 * sw, LANES), lambda s, g: (g, 0, 0)),
        pl.BlockSpec((1, sw), lambda s, g: (0, g)),
        pl.BlockSpec((1, sw), lambda s, g: (0, g)),
        pl.BlockSpec((1, LANES), lambda s, g: (0, g)),
        st_spec,
        st_spec,
    ]
    body = functools.partial(_s5_core_body, n_str=n_str, n_steps=n_steps, chained=chained)
    st_shape = jax.ShapeDtypeStruct((n_seq, ns, n_gb * sw), f32)
    return pl.pallas_call(
        body,
        grid=(n_seq, n_gb),
        in_specs=in_specs,
        out_specs=(pl.BlockSpec((rows, LANES), lambda s, g: (s, g)), st_spec, st_spec),
        out_shape=(jax.ShapeDtypeStruct((n_seq * rows, d), bf16), st_shape, st_shape),
        scratch_shapes=[pltpu.VMEM((rows, 2 * sw), f32), pltpu.VMEM((n_steps if chained else SUBLANES, 2 * sw), f32)],
        compiler_params=_cparams(2),
        name="s5_core",
    )(u, bm, cm, a_re, a_im, dsk, s_re, s_im)


def _s5_matrices(a_re, a_im, log_dt, b_re, b_im, c_re, c_im, d_skip):
    g, p = a_re.shape
    a = lax.complex(a_re.astype(f32), a_im.astype(f32))
    step = jnp.exp(log_dt.astype(f32))[:, None]
    a_bar = jnp.exp(a * step)
    b_bar = ((a_bar - 1.0) / a)[:, :, None] * lax.complex(b_re.astype(f32), b_im.astype(f32))
    gpb = LANES // S5_GROUP
    n_gb = g // gpb
    eye = jnp.eye(gpb, dtype=f32)

    def in_block(m):
        m = m.reshape(n_gb, gpb, p, S5_GROUP)
        return jnp.einsum('ngpc,gh->ngchp', m, eye).reshape(n_gb, gpb * S5_GROUP, gpb * p)

    def out_block(m):
        m = m.reshape(n_gb, gpb, S5_GROUP, p)
        return jnp.einsum('ngcp,gh->ngphc', m, eye).reshape(n_gb, gpb * p, gpb * S5_GROUP)

    bm = jnp.concatenate([in_block(jnp.real(b_bar)), in_block(jnp.imag(b_bar))], axis=2).astype(bf16)
    cm = jnp.concatenate([out_block(c_re.astype(f32)), -out_block(c_im.astype(f32))], axis=1).astype(bf16)
    return bm, cm, jnp.real(a_bar).reshape(1, g * p), jnp.imag(a_bar).reshape(1, g * p), d_skip.reshape(1, -1).astype(f32)


def _sconv_body(bg_ref, cg_ref, v_ref, cw_ref, buf_ref, u_ref, nb_ref, cv_ref, *, n_str, n_steps, chained):
    rows = n_str * n_steps
    ch = _pick(n_steps, (16, 8, 4, 2, 1)) * n_str
    n_halo = SCONV_W - 1
    cw = [cw_ref[j:j + 1, :] for j in range(SCONV_W)]

    def prod_chunk(c, carry):
        r0 = pl.multiple_of(c * ch, ch)
        cv_ref[pl.ds(r0, ch), :] = cg_ref[pl.ds(r0, ch), :] * v_ref[pl.ds(r0, ch), :]
        return carry

    lax.fori_loop(0, rows // ch, prod_chunk, 0)
    halo = jnp.concatenate(
        [_halo_rows(cv_ref, buf_ref[n_halo - j], j, n_str=n_str, n_steps=n_steps, chained=chained)
         for j in range(n_halo, 0, -1)], axis=0)

    def emit(xin, r0):
        z = xin[0:ch, :] * cw[0]
        for j in range(1, SCONV_W):
            z = z + xin[j * n_str:j * n_str + ch, :] * cw[j]
        u_ref[pl.ds(r0, ch), :] = (bg_ref[pl.ds(r0, ch), :] * z).astype(u_ref.dtype)

    emit(jnp.concatenate([halo, cv_ref[0:ch, :]], axis=0), 0)

    def out_chunk(c, carry):
        r0 = pl.multiple_of(c * ch, ch)
        emit(cv_ref[pl.ds(r0 - n_halo * n_str, ch + n_halo * n_str), :], r0)
        return carry

    lax.fori_loop(1, rows // ch, out_chunk, 0)
    for j in range(n_halo):
        step_rows = cv_ref[(n_steps - n_halo + j) * n_str:(n_steps - n_halo + j + 1) * n_str, :]
        nb_ref[j] = step_rows[n_str - 1:, :] if chained else step_rows


def _sconv_core(proj, conv_w, buf, *, n_seq, n_str, n_steps, chained):
    d = proj.shape[1] // 3
    lw = 256
    rows = n_str * n_steps
    ns = 1 if chained else n_str
    nl = d // lw
    body = functools.partial(_sconv_body, n_str=n_str, n_steps=n_steps, chained=chained)
    buf_spec = pl.BlockSpec((None, SCONV_W - 1, ns, lw), lambda s, n: (s, 0, 0, n))
    return pl.pallas_call(
        body,
        grid=(n_seq, nl),
        in_specs=[
            pl.BlockSpec((rows, lw), lambda s, n: (s, n)),
            pl.BlockSpec((rows, lw), lambda s, n: (s, nl + n)),
            pl.BlockSpec((rows, lw), lambda s, n: (s, 2 * nl + n)),
            pl.BlockSpec((SCONV_W, lw), lambda s, n: (0, n)),
            buf_spec,
        ],
        out_specs=(pl.BlockSpec((rows, lw), lambda s, n: (s, n)), buf_spec),
        out_shape=(
            jax.ShapeDtypeStruct((n_seq * rows, d), bf16),
            jax.ShapeDtypeStruct((n_seq, SCONV_W - 1, ns, d), f32),
        ),
        scratch_shapes=[pltpu.VMEM((rows, lw), f32)],
        compiler_params=_cparams(2),
        name="sconv_core",
    )(proj, proj, proj, conv_w, buf)


class _Group:
    def __init__(self, n_batch, t_seq, offset, has_past):
        self.n_batch, self.t_seq, self.offset, self.has_past = n_batch, t_seq, offset, has_past
        self.chained = not has_past
        if self.chained:
            self.n_seq, self.n_str, self.n_steps = n_batch, PROMPT_STREAMS, t_seq // PROMPT_STREAMS
        else:
            self.n_seq, self.n_str, self.n_steps = 1, n_batch, t_seq
        self.n_rows = n_batch * t_seq
        self.dims = dict(n_seq=self.n_seq, n_str=self.n_str, n_steps=self.n_steps, chained=self.chained)

    def to_streams(self, x):
        f = x.shape[-1]
        if self.chained:
            return x.reshape(self.n_batch, self.n_str, self.n_steps, f).transpose(0, 2, 1, 3).reshape(self.n_rows, f)
        return x.reshape(self.n_batch, self.t_seq, f).transpose(1, 0, 2).reshape(self.n_rows, f)

    def to_natural(self, x):
        f = x.shape[-1]
        if self.chained:
            return x.reshape(self.n_batch, self.n_steps, self.n_str, f).transpose(0, 2, 1, 3).reshape(self.n_rows, f)
        return x.reshape(self.t_seq, self.n_batch, f).transpose(1, 0, 2).reshape(self.n_rows, f)

    def last_steps(self, x, k, c0, c1):
        if self.chained:
            rows = [x[b * self.t_seq + (self.n_steps - k + j) * self.n_str + self.n_str - 1, c0:c1]
                    for b in range(self.n_batch) for j in range(k)]
            return jnp.stack(rows).reshape(self.n_batch, k, c1 - c0)
        r0 = (self.t_seq - k) * self.n_batch
        return x[r0:, c0:c1].reshape(k, self.n_batch, c1 - c0).transpose(1, 0, 2)

    def state_in(self, s):
        if s.ndim == 2:
            return s.astype(f32)[None]
        return s.astype(f32).transpose(1, 0, 2)[None]

    def zeros_state(self, k, width):
        shape = (self.n_seq, 1, width) if k is None else (self.n_seq, k, 1, width)
        return jnp.zeros(shape, f32)

    def state_out(self, s):
        return s.reshape(self.n_batch, -1)


def kernel(x_prompt, x_sample, cache_attn_k, cache_attn_v, cache_idx_k, state_lru_conv, state_lru_h, state_s5_re, state_s5_im, state_sconv, norm_mix, norm_ffn, attn_w_in, attn_g_q, attn_g_k, attn_g_ki, attn_w_o, lru_w_in, lru_conv_w, lru_conv_b, lru_w_a, lru_b_a, lru_w_x, lru_b_x, lru_lambda, lru_w_out, s5_a_re, s5_a_im, s5_log_dt, s5_b_re, s5_b_im, s5_c_re, s5_c_im, s5_d, s5_w_glu, sc_w_in, sc_conv_w, sc_w_out, ffn_w_up, ffn_w_down):
    bp, tp, d = x_prompt.shape
    bs, ts, _ = x_sample.shape
    n_past = cache_attn_k.shape[2]
    depth = norm_mix.shape[0]
    n_mix = 4
    groups = [_Group(bp, tp, 0, False), _Group(bs, ts, n_past, True)]
    xs = [x_prompt.reshape(bp * tp, d).astype(f32), x_sample.reshape(bs * ts, d).astype(f32)]
    layouts = ["natural", "natural"]
    outs = [[[] for _ in range(8)] for _ in groups]
    w_up_bf, w_down_bf = ffn_w_up.astype(bf16), ffn_w_down.astype(bf16)

    def wanted(i):
        return "natural" if i >= depth or i % n_mix == 0 else "streams"

    for i in range(depth):
        m, r = i % n_mix, i // n_mix
        for gi, g in enumerate(groups):
            if layouts[gi] != wanted(i):
                xs[gi] = g.to_natural(xs[gi]) if wanted(i) == "natural" else g.to_streams(xs[gi])
                layouts[gi] = wanted(i)
        if m == 0:
            n_in = attn_w_in.shape[2]
            n_pad = -(-n_in // 768) * 768
            w_in_p = jnp.pad(attn_w_in[r], ((0, 0), (0, n_pad - n_in))).astype(bf16)
            w_o = attn_w_o[r].astype(bf16)
            pasts = [None, (cache_attn_k, cache_attn_v, cache_idx_k, r)]
            for gi, g in enumerate(groups):
                xs[gi], k, v, ki = _attn_layer(xs[gi], g, pasts[gi], norm_mix[i], w_in_p, attn_g_q[r], attn_g_k[r],
                                               attn_g_ki[r], w_o)
                outs[gi][0].append(k.reshape(g.n_batch, g.t_seq, N_KV_HEADS, HEAD_DIM))
                outs[gi][1].append(v.reshape(g.n_batch, g.t_seq, N_KV_HEADS, HEAD_DIM))
                outs[gi][2].append(ki.reshape(g.n_batch, g.t_seq, IDX_DIM))
        elif m == 1:
            w = lru_w_in.shape[2] // 2
            w_in, w_out = lru_w_in[r].astype(bf16), lru_w_out[r].astype(bf16)
            params = (lru_conv_w[r], lru_conv_b[r], lru_w_a[r].astype(bf16), lru_b_a[r], lru_w_x[r].astype(bf16),
                      lru_b_x[r], lru_lambda[r])
            for gi, g in enumerate(groups):
                proj = _matmul(xs[gi], [w_in], gain=norm_mix[i], name="lru_in")
                if g.has_past:
                    buf, h0 = g.state_in(state_lru_conv[r]), g.state_in(state_lru_h[r])
                else:
                    buf, h0 = g.zeros_state(LRU_CONV - 1, w), g.zeros_state(None, w)
                y, hl = _lru_core(proj, params, buf, h0, **g.dims)
                outs[gi][3].append(g.last_steps(proj, LRU_CONV - 1, w, 2 * w))
                outs[gi][4].append(g.state_out(hl))
                xs[gi] = _matmul(y, [w_out], residual=xs[gi], name="lru_out")
        elif m == 2:
            mats = _s5_matrices(s5_a_re[r], s5_a_im[r], s5_log_dt[r], s5_b_re[r], s5_b_im[r], s5_c_re[r],
                                s5_c_im[r], s5_d[r])
            n_grp, n_state = s5_a_re.shape[1], s5_a_re.shape[2]
            gp = n_grp * n_state
            wg = s5_w_glu[r].astype(bf16)
            wg = [wg[:, :d], wg[:, d:]]
            for gi, g in enumerate(groups):
                u = _norm(xs[gi], norm_mix[i])
                if g.has_past:
                    s_re = g.state_in(state_s5_re[r].reshape(g.n_batch, gp))
                    s_im = g.state_in(state_s5_im[r].reshape(g.n_batch, gp))
                else:
                    s_re, s_im = g.zeros_state(None, gp), g.zeros_state(None, gp)
                y, o_re, o_im = _s5_core(u, mats, s_re, s_im, **g.dims)
                outs[gi][5].append(g.state_out(o_re).reshape(g.n_batch, n_grp, n_state))
                outs[gi][6].append(g.state_out(o_im).reshape(g.n_batch, n_grp, n_state))
                xs[gi] = _matmul(y, wg, residual=xs[gi], glu=True, name="s5_glu")
        else:
            w_in, w_out = sc_w_in[r].astype(bf16), sc_w_out[r].astype(bf16)
            for gi, g in enumerate(groups):
                proj = _matmul(xs[gi], [w_in], gain=norm_mix[i], name="sconv_in")
                buf = g.state_in(state_sconv[r]) if g.has_past else g.zeros_state(SCONV_W - 1, d)
                u, nb = _sconv_core(proj, sc_conv_w[r], buf, **g.dims)
                outs[gi][7].append(nb.transpose(0, 2, 1, 3).reshape(g.n_batch, SCONV_W - 1, d))
                xs[gi] = _matmul(u, [w_out], residual=xs[gi], name="sconv_out")
        for gi, g in enumerate(groups):
            if g.chained:
                xs[gi] = _ffn(xs[gi], norm_ffn[i], w_up_bf, w_down_bf, i, streams=(g.n_str, g.n_steps),
                              in_streams=layouts[gi] == "streams", out_streams=wanted(i + 1) == "streams")
                layouts[gi] = wanted(i + 1)
            else:
                xs[gi] = _ffn(xs[gi], norm_ffn[i], w_up_bf, w_down_bf, i)

    for gi, g in enumerate(groups):
        if layouts[gi] != "natural":
            xs[gi] = g.to_natural(xs[gi])
    y_prompt = xs[0].reshape(bp, tp, d)
    y_sample = xs[1].reshape(bs, ts, d)
    st = [[jnp.stack(l) for l in outs[gi]] for gi in range(2)]
    return (y_prompt, y_sample, *st[0], *st[1])
```

```python
import functools
import math

import jax
import jax.numpy as jnp
from jax import lax
from jax.experimental import pallas as pl
from jax.experimental.pallas import tpu as pltpu

f32 = jnp.float32
bf16 = jnp.bfloat16
i32 = jnp.int32

LANES = 128
SUBLANES = 8
VMEM_LIMIT = 56 * 1024 * 1024

CHUNK = 64
EPS = 1e-6
N_HEADS = 16
N_KV_HEADS = 4
HEAD_DIM = 128
GQA_GROUP = N_HEADS // N_KV_HEADS
ROT_DIM = HEAD_DIM // 4
ROPE_THETA = 500000.0
ATTN_SCALE = HEAD_DIM ** -0.5
N_IDX_HEADS = 16
IDX_DIM = 64
IDX_ROT_DIM = IDX_DIM // 4
TOPK_MAX = 256
LRU_BLOCK = 128
LRU_CONV = 4
LRU_C = 8.0
S5_GROUP = 16
S5_STATE = 64
SCONV_W = 3
PROMPT_STREAMS = 8

NEG_BIG = -1e30
INT_MIN = -(2 ** 31)
LOG2E = math.log2(math.e)


def _cparams(n_axes):
    return pltpu.CompilerParams(dimension_semantics=("arbitrary",) * n_axes, vmem_limit_bytes=VMEM_LIMIT)


def _pick(n, cands):
    for c in cands:
        if n % c == 0:
            return c
    raise ValueError(f"no tile in {cands} divides {n}")


def _sigmoid(x):
    return 0.5 + 0.5 * jnp.tanh(0.5 * x)


def _rmsnorm_rows(x, g):
    ms = jnp.mean(x * x, axis=-1, keepdims=True)
    return x * lax.rsqrt(ms + EPS) * g


def _mm_body(*refs, norm, n_w, has_res, glu):
    it = iter(refs)
    x_ref = next(it)
    g_ref = next(it) if norm else None
    w_refs = [next(it) for _ in range(n_w)]
    r_ref = next(it) if has_res else None
    o_ref = next(it)
    h_ref = next(it) if norm else None

    if norm:
        @pl.when(pl.program_id(1) == 0)
        def _():
            h_ref[...] = _rmsnorm_rows(x_ref[...], g_ref[...]).astype(bf16)
        h = h_ref[...]
    else:
        h = x_ref[...]
    acc = jnp.dot(h, w_refs[0][...], preferred_element_type=f32)
    if glu:
        acc = acc * jax.nn.sigmoid(jnp.dot(h, w_refs[1][...], preferred_element_type=f32))
    if has_res:
        acc = acc + r_ref[...]
    o_ref[...] = acc.astype(o_ref.dtype)


def _matmul(x, ws, *, gain=None, residual=None, out_dtype=f32, glu=False, name="mm"):
    n, k = x.shape
    m = ws[0].shape[1]
    norm = gain is not None
    tm = _pick(n, (1024, 512, 256, 128, 64))
    tn = _pick(m, (1024, 768, 512, 256, 128))
    in_specs = [pl.BlockSpec((tm, k), lambda i, j: (i, 0))]
    args = [x]
    if norm:
        in_specs.append(pl.BlockSpec((1, k), lambda i, j: (0, 0)))
        args.append(gain.reshape(1, k).astype(f32))
    for w in ws:
        in_specs.append(pl.BlockSpec((k, tn), lambda i, j: (0, j)))
        args.append(w)
    if residual is not None:
        in_specs.append(pl.BlockSpec((tm, tn), lambda i, j: (i, j)))
        args.append(residual)
    body = functools.partial(_mm_body, norm=norm, n_w=len(ws), has_res=residual is not None, glu=glu)
    return pl.pallas_call(
        body,
        grid=(n // tm, m // tn),
        in_specs=in_specs,
        out_specs=pl.BlockSpec((tm, tn), lambda i, j: (i, j)),
        out_shape=jax.ShapeDtypeStruct((n, m), out_dtype),
        scratch_shapes=[pltpu.VMEM((tm, k), bf16)] if norm else [],
        compiler_params=_cparams(2),
        name=name,
    )(*args)


def _ffn_body(x_ref, g_ref, wu_ref, wd_ref, o_ref, h_ref, *, relayout):
    tm, d = h_ref.shape

    @pl.when(pl.program_id(1) == 0)
    def _():
        x = x_ref[...]
        if relayout == "to_streams":
            x = jnp.swapaxes(x, 0, 1).reshape(tm, d)
        elif relayout == "to_natural":
            n_str = o_ref.shape[0]
            x = jnp.swapaxes(x.reshape(tm // n_str, n_str, d), 0, 1)
        h_ref[...] = _rmsnorm_rows(x.reshape(tm, d), g_ref[...]).astype(bf16)
        o_ref[...] = x

    u = jnp.dot(h_ref[...], wu_ref[...], preferred_element_type=f32)
    u = jnp.square(jnp.maximum(u, 0.0)).astype(bf16)
    o_ref[...] += jnp.dot(u, wd_ref[...], preferred_element_type=f32).reshape(o_ref.shape)


def _ffn(x, gain, w_up, w_down, layer, *, streams=None, relayout=None):
    n, d = x.shape
    ff = w_up.shape[2]
    tf = _pick(ff, (1024, 512, 256, 128))
    plain = pl.BlockSpec((_pick(n, (512, 256, 128, 64)), d), lambda i, j: (i, 0))
    in_spec = out_spec = plain
    out_shape = (n, d)
    tm = plain.block_shape[0]
    if relayout is not None:
        n_str, n_steps = streams
        tj = _pick(n_steps, (64, 32, 16, 8))
        tm = n_str * tj
        n_run = n_steps // tj
        nat_shape = (n // n_steps, n_steps, d)
        nat_spec = pl.BlockSpec((n_str, tj, d), lambda i, j: (i // n_run, i % n_run, 0))
        str_spec = pl.BlockSpec((tm, d), lambda i, j: (i, 0))
        if relayout == "to_streams":
            x, in_spec, out_spec = x.reshape(nat_shape), nat_spec, str_spec
        else:
            in_spec, out_spec, out_shape = str_spec, nat_spec, nat_shape
    out = pl.pallas_call(
        functools.partial(_ffn_body, relayout=relayout),
        grid=(n // tm, ff // tf),
        in_specs=[
            in_spec,
            pl.BlockSpec((1, d), lambda i, j: (0, 0)),
            pl.BlockSpec((None, d, tf), lambda i, j: (layer, 0, j)),
            pl.BlockSpec((None, tf, d), lambda i, j: (layer, j, 0)),
        ],
        out_specs=out_spec,
        out_shape=jax.ShapeDtypeStruct(out_shape, f32),
        scratch_shapes=[pltpu.VMEM((tm, d), bf16)],
        compiler_params=_cparams(2),
        name="ffn",
    )(x, gain.reshape(1, d).astype(f32), w_up, w_down)
    return out.reshape(n, d)


def _norm_body(x_ref, g_ref, o_ref):
    o_ref[...] = _rmsnorm_rows(x_ref[...], g_ref[...])


def _norm(x, gain):
    n, d = x.shape
    tm = _pick(n, (512, 256, 128, 64))
    return pl.pallas_call(
        _norm_body,
        grid=(n // tm,),
        in_specs=[pl.BlockSpec((tm, d), lambda i: (i, 0)), pl.BlockSpec((1, d), lambda i: (0, 0))],
        out_specs=pl.BlockSpec((tm, d), lambda i: (i, 0)),
        out_shape=jax.ShapeDtypeStruct((n, d), f32),
        compiler_params=_cparams(1),
        name="rmsnorm",
    )(x, gain.reshape(1, d).astype(f32))


def _rope_tables(pos, width, rot):
    half = rot // 2
    inv = ROPE_THETA ** (-jnp.arange(half, dtype=f32) / half)
    ang = pos.astype(f32)[:, None] * inv[None, :]
    cos, sin = jnp.cos(ang), jnp.sin(ang)
    n = pos.shape[0]
    pad = jnp.zeros((n, width - rot), f32)
    zero = jnp.zeros((n, half), f32)
    c = jnp.concatenate([cos, cos, pad + 1.0], axis=1)
    s_up = jnp.concatenate([-sin, zero, pad], axis=1)
    s_dn = jnp.concatenate([zero, sin, pad], axis=1)
    reps = LANES // width
    return tuple(jnp.tile(t, (1, reps)) for t in (c, s_up, s_dn))


def _rope(x, c, s_up, s_dn, half):
    return x * c + pltpu.roll(x, LANES - half, axis=1) * s_up + pltpu.roll(x, half, axis=1) * s_dn


def _attn_post_body(q_ref, k_ref, qi_ref, kw_ref, c_ref, su_ref, sd_ref, ci_ref, sui_ref, sdi_ref,
                    gq_ref, gk_ref, gki_ref, qo_ref, ko_ref, qio_ref, kio_ref):
    c, su, sd = c_ref[...], su_ref[...], sd_ref[...]
    ci, sui, sdi = ci_ref[...], sui_ref[...], sdi_ref[...]
    for h in range(N_HEADS):
        sl = slice(h * HEAD_DIM, (h + 1) * HEAD_DIM)
        y = _rope(_rmsnorm_rows(q_ref[:, sl], gq_ref[...]), c, su, sd, ROT_DIM // 2)
        qo_ref[:, sl] = y.astype(qo_ref.dtype)
    for h in range(N_KV_HEADS):
        sl = slice(h * HEAD_DIM, (h + 1) * HEAD_DIM)
        ko_ref[:, sl] = _rope(_rmsnorm_rows(k_ref[:, sl], gk_ref[...]), c, su, sd, ROT_DIM // 2)
    for j in range(N_IDX_HEADS // 2):
        y = _rope(qi_ref[:, j * LANES:(j + 1) * LANES], ci, sui, sdi, IDX_ROT_DIM // 2).astype(qio_ref.dtype)
        qio_ref[2 * j] = y[:, :IDX_DIM]
        qio_ref[2 * j + 1] = y[:, IDX_DIM:]
    kw = kw_ref[:, :LANES]
    lane = lax.broadcasted_iota(i32, kw.shape, 1)
    ms = jnp.sum(jnp.where(lane < IDX_DIM, kw * kw, 0.0), axis=-1, keepdims=True) * (1.0 / IDX_DIM)
    y = _rope(kw * lax.rsqrt(ms + EPS) * gki_ref[...], ci, sui, sdi, IDX_ROT_DIM // 2)
    kio_ref[...] = y[:, :IDX_DIM]


def _attn_post(proj, pos, g_q, g_k, g_ki):
    n = proj.shape[0]
    tm = _pick(n, (256, 128, 64))
    q_w, kv_w, iq_w = N_HEADS * HEAD_DIM, N_KV_HEADS * HEAD_DIM, N_IDX_HEADS * IDX_DIM
    tabs = _rope_tables(pos, HEAD_DIM, ROT_DIM) + _rope_tables(pos, IDX_DIM, IDX_ROT_DIM)
    gki = jnp.concatenate([g_ki.astype(f32), jnp.zeros((LANES - IDX_DIM,), f32)]).reshape(1, LANES)
    row = lambda i: (i, 0)
    const = lambda i: (0, 0)
    in_specs = [
        pl.BlockSpec((tm, q_w), row),
        pl.BlockSpec((tm, kv_w), lambda i: (i, q_w // kv_w)),
        pl.BlockSpec((tm, iq_w), lambda i: (i, (q_w + 2 * kv_w) // iq_w)),
        pl.BlockSpec((tm, kv_w), lambda i: (i, (q_w + 2 * kv_w + iq_w) // kv_w)),
    ] + [pl.BlockSpec((tm, LANES), row)] * 6 + [pl.BlockSpec((1, LANES), const)] * 3
    out_shape = (
        jax.ShapeDtypeStruct((n, q_w), bf16),
        jax.ShapeDtypeStruct((n, kv_w), f32),
        jax.ShapeDtypeStruct((N_IDX_HEADS, n, IDX_DIM), bf16),
        jax.ShapeDtypeStruct((n, IDX_DIM), f32),
    )
    out_specs = (
        pl.BlockSpec((tm, q_w), row),
        pl.BlockSpec((tm, kv_w), row),
        pl.BlockSpec((N_IDX_HEADS, tm, IDX_DIM), lambda i: (0, i, 0)),
        pl.BlockSpec((tm, IDX_DIM), row),
    )
    return pl.pallas_call(
        _attn_post_body,
        grid=(n // tm,),
        in_specs=in_specs,
        out_specs=out_specs,
        out_shape=out_shape,
        compiler_params=_cparams(1),
        name="attn_post",
    )(proj, proj, proj, proj, *tabs, g_q.reshape(1, -1).astype(f32), g_k.reshape(1, -1).astype(f32), gki)


def _sortable(x):
    b = pltpu.bitcast(x, i32)
    return b ^ ((b >> 31) & 0x7FFFFFFF)


def _attn_core_body(*refs, segs, tq, n_sel, causal):
    n_seg = len(segs)
    n_pack = LANES // tq
    seg_lens = [n for n, _ in segs]
    it = iter(refs)
    qi_refs = [next(it) for _ in range(n_pack)]
    wi_ref, q_ref = next(it), next(it)
    seg_refs = [([next(it) for _ in range(n_pack)], next(it), next(it)) for _ in range(n_seg)]
    o_ref = next(it)
    keys_ref, bias_ref, sc_ref, acc_ref = (next(it) for _ in range(4))

    l_tot = sum(seg_lens)
    offs = [sum(seg_lens[:s]) for s in range(n_seg)]
    wq = GQA_GROUP * tq
    q_tile = pl.program_id(1)
    in_pack = pl.program_id(2) // N_KV_HEADS
    g_head = pl.program_id(2) % N_KV_HEADS
    if causal:
        assert n_seg == 1 and n_pack == 1
        gran = _pick(l_tot, (1024, 512))
        n_gran = jnp.minimum(((q_tile + 1) * tq + gran - 1) // gran, l_tot // gran)
        blk = 512
        n_blk = n_gran * (gran // blk)
    else:
        gran = None
        blk = _pick(l_tot, (512, 320, 256, 192, 128, 64))
        n_blk = l_tot // blk

    def n_chunks(s, lc):
        return n_gran * (gran // lc) if causal else seg_lens[s] // lc

    @pl.when(pl.program_id(2) == 0)
    def _select():
        wi_rows = [wi_ref[h:h + 1, :] for h in range(N_IDX_HEADS)]
        qi_rows = []
        for h in range(N_IDX_HEADS):
            for w in range(n_pack):
                qh = qi_refs[w][h]
                parts = [qh if v == w else jnp.zeros_like(qh) for v in range(n_pack)]
                qi_rows.append(jnp.concatenate(parts, axis=1) if n_pack > 1 else qh)
        qi_all = jnp.concatenate(qi_rows, axis=0)

        for s in range(n_seg):
            ki_refs = seg_refs[s][0]
            lc = _pick(seg_lens[s], (256, 128, 64))

            def score_chunk(c, carry, ki_refs=ki_refs, lc=lc, off=offs[s]):
                r0 = pl.multiple_of(c * lc, lc)
                kic = [r[pl.ds(r0, lc), :].astype(bf16) for r in ki_refs]
                kic = jnp.concatenate(kic, axis=1) if n_pack > 1 else kic[0]
                sc = lax.dot_general(kic, qi_all, (((1,), (1,)), ((), ())), preferred_element_type=f32)
                acc = jnp.zeros((lc, LANES), f32)
                for h in range(N_IDX_HEADS):
                    acc = acc + jnp.maximum(sc[:, h * LANES:(h + 1) * LANES], 0.0) * wi_rows[h]
                key = _sortable(acc)
                if causal:
                    kpos = r0 + lax.broadcasted_iota(i32, (lc, LANES), 0)
                    qpos = q_tile * tq + lax.broadcasted_iota(i32, (lc, LANES), 1)
                    key = jnp.where((kpos // CHUNK) <= (qpos // CHUNK), key, INT_MIN)
                keys_ref[pl.ds(pl.multiple_of(off + r0, CHUNK), lc), :] = key
                return carry

            lax.fori_loop(0, n_chunks(s, lc), score_chunk, 0)

        def count(pred):
            def body(i, acc):
                r0 = pl.multiple_of(i * blk, blk)
                hit = jnp.where(pred(keys_ref[pl.ds(r0, blk), :], r0), 1, 0)
                return acc + jnp.sum(hit.reshape(blk // SUBLANES, SUBLANES, LANES), axis=0)
            acc = lax.fori_loop(0, n_blk, body, jnp.zeros((SUBLANES, LANES), i32))
            return jnp.sum(acc, axis=0, keepdims=True)

        def count_ge(cand):
            return count(lambda k, r0: k >= cand)

        ans = jnp.where(count_ge(jnp.zeros((1, LANES), i32)) >= n_sel, 0, INT_MIN)

        def bit_step(i, ans):
            cand = ans | (1 << (30 - i))
            return jnp.where(count_ge(cand) >= n_sel, cand, ans)

        ans = lax.fori_loop(0, 31, bit_step, ans)
        tau = jnp.maximum(ans, INT_MIN + 1)
        n_ge = count_ge(tau)
        ties = jnp.max(n_ge) > n_sel

        @pl.when(jnp.logical_not(ties))
        def _():
            def body(i, carry):
                r0 = pl.multiple_of(i * blk, blk)
                keep = keys_ref[pl.ds(r0, blk), :] >= tau
                bias_ref[pl.ds(r0, blk), :] = jnp.where(keep, 0.0, NEG_BIG).astype(bias_ref.dtype)
                return carry
            lax.fori_loop(0, n_blk, body, 0)

        @pl.when(ties)
        def _():
            def row_idx(r0):
                return r0 + lax.broadcasted_iota(i32, (blk, LANES), 0)
            need = n_sel - count(lambda k, r0: k > tau)

            def idx_step(i, m):
                cand = m | (1 << (14 - i))
                n_below = count(lambda k, r0: jnp.logical_and(k == tau, row_idx(r0) < cand))
                return jnp.where(n_below <= need, cand, m)

            m_cut = lax.fori_loop(0, 15, idx_step, jnp.zeros((1, LANES), i32))

            def body(i, carry):
                r0 = pl.multiple_of(i * blk, blk)
                k = keys_ref[pl.ds(r0, blk), :]
                keep = jnp.logical_or(k > tau, jnp.logical_and(k == tau, row_idx(r0) < m_cut))
                bias_ref[pl.ds(r0, blk), :] = jnp.where(keep, 0.0, NEG_BIG).astype(bias_ref.dtype)
                return carry
            lax.fori_loop(0, n_blk, body, 0)

    qg = jnp.concatenate([q_ref[:, j * HEAD_DIM:(j + 1) * HEAD_DIM] for j in range(GQA_GROUP)], axis=0)
    bias_lane = in_pack * tq + lax.broadcasted_iota(i32, (wq, LANES), 0) % tq
    one_hot = (lax.broadcasted_iota(i32, (wq, LANES), 1) == bias_lane).astype(bf16)
    q_aug = jnp.concatenate([qg, one_hot], axis=1)
    c_exp = ATTN_SCALE * LOG2E

    def read_kv(ref, interleaved, r0, lc):
        if interleaved:
            return ref[pl.ds(r0 * N_KV_HEADS + g_head, lc, stride=N_KV_HEADS), :].astype(bf16)
        return ref[pl.ds(r0, lc), :].astype(bf16)

    def fold_rows(x, op):
        return op(x.reshape(x.shape[0] // SUBLANES, SUBLANES, x.shape[1]), axis=0)

    m_acc = jnp.full((SUBLANES, wq), NEG_BIG, f32)
    for s in range(n_seg):
        k_ref = seg_refs[s][1]
        lc = _pick(seg_lens[s], (1024, 512, 256, 128, 64))

        def logits_chunk(c, m_acc, k_ref=k_ref, lc=lc, off=offs[s], inter=segs[s][1]):
            r0 = pl.multiple_of(c * lc, lc)
            row = pl.multiple_of(off + r0, CHUNK)
            k_aug = jnp.concatenate([read_kv(k_ref, inter, r0, lc), bias_ref[pl.ds(row, lc), :]], axis=1)
            sc = lax.dot_general(k_aug, q_aug, (((1,), (1,)), ((), ())), preferred_element_type=f32)
            sc_ref[pl.ds(row, lc), :] = sc
            return jnp.maximum(m_acc, fold_rows(sc, jnp.max))

        m_acc = lax.fori_loop(0, n_chunks(s, lc), logits_chunk, m_acc)
    m_row = jnp.max(m_acc, axis=0, keepdims=True)

    acc_ref[...] = jnp.zeros(acc_ref.shape, f32)
    l_acc = jnp.zeros((SUBLANES, wq), f32)
    for s in range(n_seg):
        v_ref = seg_refs[s][2]
        lc = _pick(seg_lens[s], (1024, 512, 256, 128, 64))

        def value_chunk(c, l_acc, v_ref=v_ref, lc=lc, off=offs[s], inter=segs[s][1]):
            r0 = pl.multiple_of(c * lc, lc)
            row = pl.multiple_of(off + r0, CHUNK)
            p = jnp.exp2((sc_ref[pl.ds(row, lc), :] - m_row) * c_exp)
            vc = read_kv(v_ref, inter, r0, lc)
            acc_ref[...] += lax.dot_general(vc, p.astype(bf16), (((0,), (0,)), ((), ())), preferred_element_type=f32)
            return l_acc + fold_rows(p, jnp.sum)

        l_acc = lax.fori_loop(0, n_chunks(s, lc), value_chunk, l_acc)

    o_t = (acc_ref[...] / jnp.sum(l_acc, axis=0, keepdims=True)).T
    for j in range(GQA_GROUP):
        o_ref[:, j * HEAD_DIM:(j + 1) * HEAD_DIM] = o_t[j * tq:(j + 1) * tq, :].astype(o_ref.dtype)


def _attn_core(q, qi, wi_t, segs, *, n_batch, t_seq, n_sel, causal):
    n = q.shape[0]
    tq = min(LANES, t_seq)
    n_pack = LANES // tq
    assert n_batch % n_pack == 0 and t_seq % tq == 0
    nq = t_seq // tq
    wq = GQA_GROUP * tq
    kv_w = GQA_GROUP * HEAD_DIM
    batch_of = lambda bp, e: bp * n_pack + e // N_KV_HEADS
    head_of = lambda e: e % N_KV_HEADS
    in_specs, args = [], []
    for w in range(n_pack):
        in_specs.append(pl.BlockSpec((N_IDX_HEADS, tq, IDX_DIM), lambda bp, i, e, w=w: (0, (bp * n_pack + w) * nq + i, 0)))
        args.append(qi)
    in_specs += [
        pl.BlockSpec((None, N_IDX_HEADS, LANES), lambda bp, i, e: (bp, 0, i * tq // LANES)),
        pl.BlockSpec((tq, kv_w), lambda bp, i, e: (batch_of(bp, e) * nq + i, head_of(e))),
    ]
    args += [wi_t, q]
    seg_meta = []
    for ki, k, v, n_keys, imap, interleaved in segs:
        lead = (None,) * (ki.ndim - 2)
        kv_rows = n_keys * N_KV_HEADS if interleaved else n_keys
        kv_map = lambda bp, i, e, imap=imap, inter=interleaved: imap(batch_of(bp, e), 0 if inter else head_of(e))
        for w in range(n_pack):
            in_specs.append(pl.BlockSpec(lead + (n_keys, IDX_DIM), lambda bp, i, e, imap=imap, w=w: imap(bp * n_pack + w, 0)))
            args.append(ki)
        in_specs += [pl.BlockSpec(lead + (kv_rows, HEAD_DIM), kv_map)] * 2
        args += [k, v]
        seg_meta.append((n_keys, interleaved))
    l_tot = sum(m[0] for m in seg_meta)
    body = functools.partial(_attn_core_body, segs=tuple(seg_meta), tq=tq, n_sel=n_sel, causal=causal)
    return pl.pallas_call(
        body,
        grid=(n_batch // n_pack, nq, n_pack * N_KV_HEADS),
        in_specs=in_specs,
        out_specs=pl.BlockSpec((tq, kv_w), lambda bp, i, e: (batch_of(bp, e) * nq + i, head_of(e))),
        out_shape=jax.ShapeDtypeStruct((n, N_HEADS * HEAD_DIM), bf16),
        scratch_shapes=[
            pltpu.VMEM((l_tot, LANES), i32),
            pltpu.VMEM((l_tot, LANES), bf16),
            pltpu.VMEM((l_tot, wq), f32),
            pltpu.VMEM((HEAD_DIM, wq), f32),
        ],
        compiler_params=_cparams(3),
        name="attn_core",
    )(*args)


def _attn_group(q, qi, wi, ki_new, k_new, v_new, past, *, n_batch, t_seq, offset):
    tq = min(LANES, t_seq)
    n_pack = LANES // tq
    wi_g = wi.reshape(n_batch // n_pack, n_pack, t_seq, N_IDX_HEADS).transpose(0, 3, 1, 2)
    wi_g = wi_g.reshape(n_batch // n_pack, N_IDX_HEADS, n_pack * t_seq)
    ki_g = ki_new.reshape(n_batch, t_seq, IDX_DIM)
    k_g = k_new.reshape(n_batch, t_seq, N_KV_HEADS * HEAD_DIM)
    v_g = v_new.reshape(n_batch, t_seq, N_KV_HEADS * HEAD_DIM)
    n_sel = min(TOPK_MAX, (offset + t_seq) // 4)
    new_map = lambda b, hb: (b, 0, hb)

    if past is None:
        segs = [(ki_g, k_g, v_g, t_seq, new_map, False)]
        return _attn_core(q, qi, wi_g, segs, n_batch=n_batch, t_seq=t_seq, n_sel=n_sel, causal=True)

    assert offset % CHUNK == 0 and t_seq <= CHUNK
    pk, pv, pki, r = past
    n_past = pk.shape[2]
    pk = pk.reshape(pk.shape[0], n_batch, n_past * N_KV_HEADS, HEAD_DIM)
    pv = pv.reshape(pv.shape[0], n_batch, n_past * N_KV_HEADS, HEAD_DIM)
    past_map = lambda b, hb: (r, b, 0, 0)
    segs = [(pki, pk, pv, n_past, past_map, True), (ki_g, k_g, v_g, t_seq, new_map, False)]
    return _attn_core(q, qi, wi_g, segs, n_batch=n_batch, t_seq=t_seq, n_sel=n_sel, causal=False)


def _attn_layer(x, group, past, gain, w_in_p, g_q, g_k, g_ki, w_o):
    q_w, kv_w, iq_w = N_HEADS * HEAD_DIM, N_KV_HEADS * HEAD_DIM, N_IDX_HEADS * IDX_DIM
    proj = _matmul(x, [w_in_p], gain=gain, name="attn_in")
    pos = group.offset + jnp.tile(jnp.arange(group.t_seq), group.n_batch)
    q, k, qi, ki = _attn_post(proj, pos, g_q, g_k, g_ki)
    v = proj[:, q_w + kv_w:q_w + 2 * kv_w]
    wi = proj[:, q_w + 2 * kv_w + iq_w + IDX_DIM:q_w + 2 * kv_w + iq_w + IDX_DIM + N_IDX_HEADS]
    o = _attn_group(q, qi, wi, ki, k, v, past, n_batch=group.n_batch, t_seq=group.t_seq, offset=group.offset)
    x = _matmul(o, [w_o], residual=x, name="attn_out")
    return x, k, v, ki


def _halo_rows(x_ref, init, j, *, n_str, n_steps, chained):
    if not chained:
        return init
    prev = x_ref[(n_steps - j) * n_str:(n_steps - j + 1) * n_str, :]
    shifted = pltpu.roll(prev, 1, axis=0)
    first = lax.broadcasted_iota(i32, shifted.shape, 0) == 0
    return jnp.where(first, init, shifted)


def _chain_carries(h_end, p_end, c0):
    rows = [c0]
    for s in range(1, h_end.shape[0]):
        rows.append(p_end[s - 1:s] * rows[-1] + h_end[s - 1:s])
    return jnp.concatenate(rows, axis=0)


def _lru_core_body(xb_ref, gate_ref, cw_ref, cb_ref, wa_ref, ba_ref, wx_ref, bx_ref, lam_ref, buf_ref, h0_ref,
                   y_ref, hl_ref, a_ref, b_ref, *, n_str, n_steps, chained, lw):
    rows = n_str * n_steps
    ch = _pick(n_steps, (16, 8, 4, 2, 1)) * n_str
    n_halo = LRU_CONV - 1
    cw = [cw_ref[j:j + 1, :] for j in range(LRU_CONV)]
    cb = cb_ref[...]
    sp = jax.nn.softplus(-lam_ref[...])
    halo = jnp.concatenate(
        [_halo_rows(xb_ref, buf_ref[n_halo - j], j, n_str=n_str, n_steps=n_steps, chained=chained)
         for j in range(n_halo, 0, -1)], axis=0)

    def gates(xin, r0):
        xc = cb
        for j in range(LRU_CONV):
            xc = xc + xin[j * n_str:j * n_str + ch, :] * cw[j]
        xcb = xc.astype(bf16)
        for t in range(lw // LRU_BLOCK):
            sl = slice(t * LRU_BLOCK, (t + 1) * LRU_BLOCK)
            xt = xcb[:, sl]
            r = _sigmoid(jnp.dot(xt, wa_ref[t], preferred_element_type=f32) + ba_ref[:, sl])
            ig = _sigmoid(jnp.dot(xt, wx_ref[t], preferred_element_type=f32) + bx_ref[:, sl])
            log_a = (-LRU_C) * r * sp[:, sl]
            th = jnp.tanh(log_a)
            mult = jnp.sqrt(-2.0 * th / (1.0 - th))
            a_ref[pl.ds(r0, ch), sl] = jnp.exp(log_a)
            b_ref[pl.ds(r0, ch), sl] = mult * (ig * xc[:, sl])

    gates(jnp.concatenate([halo, xb_ref[0:ch, :]], axis=0), 0)

    def gate_chunk(c, carry):
        r0 = pl.multiple_of(c * ch, ch)
        gates(xb_ref[pl.ds(r0 - n_halo * n_str, ch + n_halo * n_str), :], r0)
        return carry

    lax.fori_loop(1, rows // ch, gate_chunk, 0)

    def scan_step(t, carry):
        r0 = pl.multiple_of(t * n_str, n_str)
        a = a_ref[pl.ds(r0, n_str), :]
        h = a * carry[0] + b_ref[pl.ds(r0, n_str), :]
        b_ref[pl.ds(r0, n_str), :] = h
        if not chained:
            return (h,)
        p = a * carry[1]
        a_ref[pl.ds(r0, n_str), :] = p
        return (h, p)

    if chained:
        init = (jnp.zeros((n_str, lw), f32), jnp.ones((n_str, lw), f32))
    else:
        init = (h0_ref[...],)
    fin = lax.fori_loop(0, n_steps, scan_step, init, unroll=8)

    if chained:
        carries = _chain_carries(fin[0], fin[1], h0_ref[...])
        c_rep = jnp.concatenate([carries] * (ch // n_str), axis=0)

    def out_chunk(c, carry):
        r0 = pl.multiple_of(c * ch, ch)
        h = b_ref[pl.ds(r0, ch), :]
        if chained:
            h = h + a_ref[pl.ds(r0, ch), :] * c_rep
            b_ref[pl.ds(r0, ch), :] = h
        y_ref[pl.ds(r0, ch), :] = (h * jax.nn.gelu(gate_ref[pl.ds(r0, ch), :])).astype(y_ref.dtype)
        return carry

    lax.fori_loop(0, rows // ch, out_chunk, 0)
    if chained:
        hl_ref[...] = b_ref[rows - 1:rows, :]
    else:
        hl_ref[...] = fin[0]


def _lru_core(proj, params, buf, h0, *, n_seq, n_str, n_steps, chained):
    w = proj.shape[1] // 2
    lw = 256
    rows = n_str * n_steps
    ns = 1 if chained else n_str
    cw, cb, wa, ba, wx, bx, lam = params
    nb = lw // LRU_BLOCK
    vec = lambda s, n: (0, n)
    in_specs = [
        pl.BlockSpec((rows, lw), lambda s, n: (s, w // lw + n)),
        pl.BlockSpec((rows, lw), lambda s, n: (s, n)),
        pl.BlockSpec((LRU_CONV, lw), vec),
        pl.BlockSpec((1, lw), vec),
        pl.BlockSpec((nb, LRU_BLOCK, LRU_BLOCK), lambda s, n: (n, 0, 0)),
        pl.BlockSpec((1, lw), vec),
        pl.BlockSpec((nb, LRU_BLOCK, LRU_BLOCK), lambda s, n: (n, 0, 0)),
        pl.BlockSpec((1, lw), vec),
        pl.BlockSpec((1, lw), vec),
        pl.BlockSpec((None, LRU_CONV - 1, ns, lw), lambda s, n: (s, 0, 0, n)),
        pl.BlockSpec((None, ns, lw), lambda s, n: (s, 0, n)),
    ]
    body = functools.partial(_lru_core_body, n_str=n_str, n_steps=n_steps, chained=chained, lw=lw)
    return pl.pallas_call(
        body,
        grid=(n_seq, w // lw),
        in_specs=in_specs,
        out_specs=(
            pl.BlockSpec((rows, lw), lambda s, n: (s, n)),
            pl.BlockSpec((None, ns, lw), lambda s, n: (s, 0, n)),
        ),
        out_shape=(
            jax.ShapeDtypeStruct((n_seq * rows, w), bf16),
            jax.ShapeDtypeStruct((n_seq, ns, w), f32),
        ),
        scratch_shapes=[pltpu.VMEM((rows, lw), f32), pltpu.VMEM((rows, lw), f32)],
        compiler_params=_cparams(2),
        name="lru_core",
    )(proj, proj, cw, cb.reshape(1, w), wa, ba.reshape(1, w), wx, bx.reshape(1, w), lam.reshape(1, w), buf, h0)


def _s5_core_body(u_ref, bm_ref, cm_ref, are_ref, aim_ref, d_ref, sre_ref, sim_ref,
                  y_ref, ore_ref, oim_ref, hs_ref, p_ref, *, n_str, n_steps, chained):
    rows = n_str * n_steps
    half = hs_ref.shape[1] // 2
    spc = _pick(n_steps, tuple(s for s in (64, 32, 16, 8, 4, 2, 1) if s * n_str <= 512))
    ch = spc * n_str

    def proj_chunk(c, carry):
        r0 = pl.multiple_of(c * ch, ch)
        hs_ref[pl.ds(r0, ch), :] = jnp.dot(u_ref[pl.ds(r0, ch), :].astype(bf16), bm_ref[...], preferred_element_type=f32)
        return carry

    lax.fori_loop(0, rows // ch, proj_chunk, 0)

    a_re = jnp.broadcast_to(are_ref[...], (n_str, half))
    a_im = jnp.broadcast_to(aim_ref[...], (n_str, half))

    def scan_step(t, carry):
        r0 = pl.multiple_of(t * n_str, n_str)
        h_re, h_im = carry[0], carry[1]
        n_re = a_re * h_re - a_im * h_im + hs_ref[pl.ds(r0, n_str), :half]
        n_im = a_re * h_im + a_im * h_re + hs_ref[pl.ds(r0, n_str), half:]
        hs_ref[pl.ds(r0, n_str), :half] = n_re
        hs_ref[pl.ds(r0, n_str), half:] = n_im
        if not chained:
            return (n_re, n_im)
        p_re, p_im = carry[2], carry[3]
        q_re = a_re[:1] * p_re - a_im[:1] * p_im
        q_im = a_re[:1] * p_im + a_im[:1] * p_re
        p_ref[pl.ds(t, 1), :half] = q_re
        p_ref[pl.ds(t, 1), half:] = q_im
        return (n_re, n_im, q_re, q_im)

    if chained:
        zero = jnp.zeros((n_str, half), f32)
        init = (zero, zero, jnp.ones((1, half), f32), jnp.zeros((1, half), f32))
    else:
        init = (sre_ref[...], sim_ref[...])
    fin = lax.fori_loop(0, n_steps, scan_step, init, unroll=2)

    if chained:
        pe_re, pe_im = fin[2], fin[3]
        c_re, c_im = [sre_ref[...]], [sim_ref[...]]
        for s in range(1, n_str):
            prev_re, prev_im = c_re[-1], c_im[-1]
            c_re.append(pe_re * prev_re - pe_im * prev_im + fin[0][s - 1:s])
            c_im.append(pe_re * prev_im + pe_im * prev_re + fin[1][s - 1:s])
        c_re = jnp.concatenate(c_re, axis=0)
        c_im = jnp.concatenate(c_im, axis=0)
        ore_ref[...] = fin[0][n_str - 1:] + pe_re * c_re[n_str - 1:] - pe_im * c_im[n_str - 1:]
        oim_ref[...] = fin[1][n_str - 1:] + pe_re * c_im[n_str - 1:] + pe_im * c_re[n_str - 1:]
        c_re = jnp.concatenate([c_re] * spc, axis=0)
        c_im = jnp.concatenate([c_im] * spc, axis=0)
    else:
        ore_ref[...] = fin[0]
        oim_ref[...] = fin[1]

    def per_stream(p):
        return jnp.broadcast_to(p[:, None, :], (spc, n_str, p.shape[1])).reshape(ch, p.shape[1])

    def out_chunk(c, carry):
        r0 = pl.multiple_of(c * ch, ch)
        u = u_ref[pl.ds(r0, ch), :]
        hs = hs_ref[pl.ds(r0, ch), :]
        if chained:
            t0 = pl.multiple_of(c * spc, spc)
            p_re = per_stream(p_ref[pl.ds(t0, spc), :half])
            p_im = per_stream(p_ref[pl.ds(t0, spc), half:])
            hs = jnp.concatenate([hs[:, :half] + p_re * c_re - p_im * c_im,
                                  hs[:, half:] + p_re * c_im + p_im * c_re], axis=1)
        y = jnp.dot(hs.astype(bf16), cm_ref[...], preferred_element_type=f32) + d_ref[...] * u
        y_ref[pl.ds(r0, ch), :] = jax.nn.gelu(y).astype(y_ref.dtype)
        return carry

    lax.fori_loop(0, rows // ch, out_chunk, 0)


def _s5_core(u, mats, s_re, s_im, *, n_seq, n_str, n_steps, chained):
    d = u.shape[1]
    bm, cm, a_re, a_im, dsk = mats
    n_gb = d // LANES
    sw = bm.shape[2] // 2
    rows = n_str * n_steps
    ns = 1 if chained else n_str
    st_spec = pl.BlockSpec((None, ns, sw), lambda s, g: (s, 0, g))
    in_specs = [
        pl.BlockSpec((rows, LANES), lambda s, g: (s, g)),
        pl.BlockSpec((None, LANES, 2 * sw), lambda s, g: (g, 0, 0)),
        pl.BlockSpec((None, 2 * sw, LANES), lambda s, g: (g, 0, 0)),
        pl.BlockSpec((1, sw), lambda s, g: (0, g)),
        pl.BlockSpec((1, sw), lambda s, g: (0, g)),
        pl.BlockSpec((1, LANES), lambda s, g: (0, g)),
        st_spec,
        st_spec,
    ]
    body = functools.partial(_s5_core_body, n_str=n_str, n_steps=n_steps, chained=chained)
    st_shape = jax.ShapeDtypeStruct((n_seq, ns, n_gb * sw), f32)
    return pl.pallas_call(
        body,
        grid=(n_seq, n_gb),
        in_specs=in_specs,
        out_specs=(pl.BlockSpec((rows, LANES), lambda s, g: (s, g)), st_spec, st_spec),
        out_shape=(jax.ShapeDtypeStruct((n_seq * rows, d), bf16), st_shape, st_shape),
        scratch_shapes=[pltpu.VMEM((rows, 2 * sw), f32), pltpu.VMEM((n_steps if chained else SUBLANES, 2 * sw), f32)],
        compiler_params=_cparams(2),
        name="s5_core",
    )(u, bm, cm, a_re, a_im, dsk, s_re, s_im)


def _s5_matrices(a_re, a_im, log_dt, b_re, b_im, c_re, c_im, d_skip):
    g, p = a_re.shape
    a = lax.complex(a_re.astype(f32), a_im.astype(f32))
    step = jnp.exp(log_dt.astype(f32))[:, None]
    a_bar = jnp.exp(a * step)
    b_bar = ((a_bar - 1.0) / a)[:, :, None] * lax.complex(b_re.astype(f32), b_im.astype(f32))
    gpb = LANES // S5_GROUP
    n_gb = g // gpb
    eye = jnp.eye(gpb, dtype=f32)

    def in_block(m):
        m = m.reshape(n_gb, gpb, p, S5_GROUP)
        return jnp.einsum('ngpc,gh->ngchp', m, eye).reshape(n_gb, gpb * S5_GROUP, gpb * p)

    def out_block(m):
        m = m.reshape(n_gb, gpb, S5_GROUP, p)
        return jnp.einsum('ngcp,gh->ngphc', m, eye).reshape(n_gb, gpb * p, gpb * S5_GROUP)

    bm = jnp.concatenate([in_block(jnp.real(b_bar)), in_block(jnp.imag(b_bar))], axis=2).astype(bf16)
    cm = jnp.concatenate([out_block(c_re.astype(f32)), -out_block(c_im.astype(f32))], axis=1).astype(bf16)
    return bm, cm, jnp.real(a_bar).reshape(1, g * p), jnp.imag(a_bar).reshape(1, g * p), d_skip.reshape(1, -1).astype(f32)


def _sconv_body(bg_ref, cg_ref, v_ref, cw_ref, buf_ref, u_ref, nb_ref, cv_ref, *, n_str, n_steps, chained):
    rows = n_str * n_steps
    ch = _pick(n_steps, (16, 8, 4, 2, 1)) * n_str
    n_halo = SCONV_W - 1
    cw = [cw_ref[j:j + 1, :] for j in range(SCONV_W)]

    def prod_chunk(c, carry):
        r0 = pl.multiple_of(c * ch, ch)
        cv_ref[pl.ds(r0, ch), :] = cg_ref[pl.ds(r0, ch), :] * v_ref[pl.ds(r0, ch), :]
        return carry

    lax.fori_loop(0, rows // ch, prod_chunk, 0)
    halo = jnp.concatenate(
        [_halo_rows(cv_ref, buf_ref[n_halo - j], j, n_str=n_str, n_steps=n_steps, chained=chained)
         for j in range(n_halo, 0, -1)], axis=0)

    def emit(xin, r0):
        z = xin[0:ch, :] * cw[0]
        for j in range(1, SCONV_W):
            z = z + xin[j * n_str:j * n_str + ch, :] * cw[j]
        u_ref[pl.ds(r0, ch), :] = (bg_ref[pl.ds(r0, ch), :] * z).astype(u_ref.dtype)

    emit(jnp.concatenate([halo, cv_ref[0:ch, :]], axis=0), 0)

    def out_chunk(c, carry):
        r0 = pl.multiple_of(c * ch, ch)
        emit(cv_ref[pl.ds(r0 - n_halo * n_str, ch + n_halo * n_str), :], r0)
        return carry

    lax.fori_loop(1, rows // ch, out_chunk, 0)
    for j in range(n_halo):
        step_rows = cv_ref[(n_steps - n_halo + j) * n_str:(n_steps - n_halo + j + 1) * n_str, :]
        nb_ref[j] = step_rows[n_str - 1:, :] if chained else step_rows


def _sconv_core(proj, conv_w, buf, *, n_seq, n_str, n_steps, chained):
    d = proj.shape[1] // 3
    lw = 256
    rows = n_str * n_steps
    ns = 1 if chained else n_str
    nl = d // lw
    body = functools.partial(_sconv_body, n_str=n_str, n_steps=n_steps, chained=chained)
    buf_spec = pl.BlockSpec((None, SCONV_W - 1, ns, lw), lambda s, n: (s, 0, 0, n))
    return pl.pallas_call(
        body,
        grid=(n_seq, nl),
        in_specs=[
            pl.BlockSpec((rows, lw), lambda s, n: (s, n)),
            pl.BlockSpec((rows, lw), lambda s, n: (s, nl + n)),
            pl.BlockSpec((rows, lw), lambda s, n: (s, 2 * nl + n)),
            pl.BlockSpec((SCONV_W, lw), lambda s, n: (0, n)),
            buf_spec,
        ],
        out_specs=(pl.BlockSpec((rows, lw), lambda s, n: (s, n)), buf_spec),
        out_shape=(
            jax.ShapeDtypeStruct((n_seq * rows, d), bf16),
            jax.ShapeDtypeStruct((n_seq, SCONV_W - 1, ns, d), f32),
        ),
        scratch_shapes=[pltpu.VMEM((rows, lw), f32)],
        compiler_params=_cparams(2),
        name="sconv_core",
    )(proj, proj, proj, conv_w, buf)


class _Group:
    def __init__(self, n_batch, t_seq, offset, has_past):
        self.n_batch, self.t_seq, self.offset, self.has_past = n_batch, t_seq, offset, has_past
        self.chained = not has_past
        if self.chained:
            self.n_seq, self.n_str, self.n_steps = n_batch, PROMPT_STREAMS, t_seq // PROMPT_STREAMS
        else:
            self.n_seq, self.n_str, self.n_steps = 1, n_batch, t_seq
        self.n_rows = n_batch * t_seq
        self.dims = dict(n_seq=self.n_seq, n_str=self.n_str, n_steps=self.n_steps, chained=self.chained)

    def to_streams(self, x):
        f = x.shape[-1]
        if self.chained:
            return x.reshape(self.n_batch, self.n_str, self.n_steps, f).transpose(0, 2, 1, 3).reshape(self.n_rows, f)
        return x.reshape(self.n_batch, self.t_seq, f).transpose(1, 0, 2).reshape(self.n_rows, f)

    def to_natural(self, x):
        f = x.shape[-1]
        if self.chained:
            return x.reshape(self.n_batch, self.n_steps, self.n_str, f).transpose(0, 2, 1, 3).reshape(self.n_rows, f)
        return x.reshape(self.t_seq, self.n_batch, f).transpose(1, 0, 2).reshape(self.n_rows, f)

    def last_steps(self, x, k, c0, c1):
        if self.chained:
            rows = [x[b * self.t_seq + (self.n_steps - k + j) * self.n_str + self.n_str - 1, c0:c1]
                    for b in range(self.n_batch) for j in range(k)]
            return jnp.stack(rows).reshape(self.n_batch, k, c1 - c0)
        r0 = (self.t_seq - k) * self.n_batch
        return x[r0:, c0:c1].reshape(k, self.n_batch, c1 - c0).transpose(1, 0, 2)

    def state_in(self, s):
        if s.ndim == 2:
            return s.astype(f32)[None]
        return s.astype(f32).transpose(1, 0, 2)[None]

    def zeros_state(self, k, width):
        shape = (self.n_seq, 1, width) if k is None else (self.n_seq, k, 1, width)
        return jnp.zeros(shape, f32)

    def state_out(self, s):
        return s.reshape(self.n_batch, -1)


def kernel(x_prompt, x_sample, cache_attn_k, cache_attn_v, cache_idx_k, state_lru_conv, state_lru_h, state_s5_re, state_s5_im, state_sconv, norm_mix, norm_ffn, attn_w_in, attn_g_q, attn_g_k, attn_g_ki, attn_w_o, lru_w_in, lru_conv_w, lru_conv_b, lru_w_a, lru_b_a, lru_w_x, lru_b_x, lru_lambda, lru_w_out, s5_a_re, s5_a_im, s5_log_dt, s5_b_re, s5_b_im, s5_c_re, s5_c_im, s5_d, s5_w_glu, sc_w_in, sc_conv_w, sc_w_out, ffn_w_up, ffn_w_down):
    bp, tp, d = x_prompt.shape
    bs, ts, _ = x_sample.shape
    n_past = cache_attn_k.shape[2]
    depth = norm_mix.shape[0]
    n_mix = 4
    groups = [_Group(bp, tp, 0, False), _Group(bs, ts, n_past, True)]
    xs = [x_prompt.reshape(bp * tp, d).astype(f32), x_sample.reshape(bs * ts, d).astype(f32)]
    layouts = ["natural", "natural"]
    outs = [[[] for _ in range(8)] for _ in groups]
    w_up_bf, w_down_bf = ffn_w_up.astype(bf16), ffn_w_down.astype(bf16)

    def wanted(i):
        return "natural" if i >= depth or i % n_mix == 0 else "streams"

    for i in range(depth):
        m, r = i % n_mix, i // n_mix
        for gi, g in enumerate(groups):
            if layouts[gi] != wanted(i):
                xs[gi] = g.to_natural(xs[gi]) if wanted(i) == "natural" else g.to_streams(xs[gi])
                layouts[gi] = wanted(i)
        if m == 0:
            n_in = attn_w_in.shape[2]
            n_pad = -(-n_in // 768) * 768
            w_in_p = jnp.pad(attn_w_in[r], ((0, 0), (0, n_pad - n_in))).astype(bf16)
            w_o = attn_w_o[r].astype(bf16)
            pasts = [None, (cache_attn_k, cache_attn_v, cache_idx_k, r)]
            for gi, g in enumerate(groups):
                xs[gi], k, v, ki = _attn_layer(xs[gi], g, pasts[gi], norm_mix[i], w_in_p, attn_g_q[r], attn_g_k[r],
                                               attn_g_ki[r], w_o)
                outs[gi][0].append(k.reshape(g.n_batch, g.t_seq, N_KV_HEADS, HEAD_DIM))
                outs[gi][1].append(v.reshape(g.n_batch, g.t_seq, N_KV_HEADS, HEAD_DIM))
                outs[gi][2].append(ki.reshape(g.n_batch, g.t_seq, IDX_DIM))
        elif m == 1:
            w = lru_w_in.shape[2] // 2
            w_in, w_out = lru_w_in[r].astype(bf16), lru_w_out[r].astype(bf16)
            params = (lru_conv_w[r], lru_conv_b[r], lru_w_a[r].astype(bf16), lru_b_a[r], lru_w_x[r].astype(bf16),
                      lru_b_x[r], lru_lambda[r])
            for gi, g in enumerate(groups):
                proj = _matmul(xs[gi], [w_in], gain=norm_mix[i], name="lru_in")
                if g.has_past:
                    buf, h0 = g.state_in(state_lru_conv[r]), g.state_in(state_lru_h[r])
                else:
                    buf, h0 = g.zeros_state(LRU_CONV - 1, w), g.zeros_state(None, w)
                y, hl = _lru_core(proj, params, buf, h0, **g.dims)
                outs[gi][3].append(g.last_steps(proj, LRU_CONV - 1, w, 2 * w))
                outs[gi][4].append(g.state_out(hl))
                xs[gi] = _matmul(y, [w_out], residual=xs[gi], name="lru_out")
        elif m == 2:
            mats = _s5_matrices(s5_a_re[r], s5_a_im[r], s5_log_dt[r], s5_b_re[r], s5_b_im[r], s5_c_re[r],
                                s5_c_im[r], s5_d[r])
            n_grp, n_state = s5_a_re.shape[1], s5_a_re.shape[2]
            gp = n_grp * n_state
            wg = s5_w_glu[r].astype(bf16)
            wg = [wg[:, :d], wg[:, d:]]
            for gi, g in enumerate(groups):
                u = _norm(xs[gi], norm_mix[i])
                if g.has_past:
                    s_re = g.state_in(state_s5_re[r].reshape(g.n_batch, gp))
                    s_im = g.state_in(state_s5_im[r].reshape(g.n_batch, gp))
                else:
                    s_re, s_im = g.zeros_state(None, gp), g.zeros_state(None, gp)
                y, o_re, o_im = _s5_core(u, mats, s_re, s_im, **g.dims)
                outs[gi][5].append(g.state_out(o_re).reshape(g.n_batch, n_grp, n_state))
                outs[gi][6].append(g.state_out(o_im).reshape(g.n_batch, n_grp, n_state))
                xs[gi] = _matmul(y, wg, residual=xs[gi], glu=True, name="s5_glu")
        else:
            w_in, w_out = sc_w_in[r].astype(bf16), sc_w_out[r].astype(bf16)
            for gi, g in enumerate(groups):
                proj = _matmul(xs[gi], [w_in], gain=norm_mix[i], name="sconv_in")
                buf = g.state_in(state_sconv[r]) if g.has_past else g.zeros_state(SCONV_W - 1, d)
                u, nb = _sconv_core(proj, sc_conv_w[r], buf, **g.dims)
                outs[gi][7].append(nb.transpose(0, 2, 1, 3).reshape(g.n_batch, SCONV_W - 1, d))
                xs[gi] = _matmul(u, [w_out], residual=xs[gi], name="sconv_out")
        for gi, g in enumerate(groups):
            relayout = None
            if g.chained and layouts[gi] != wanted(i + 1):
                relayout = "to_" + wanted(i + 1)
                layouts[gi] = wanted(i + 1)
            xs[gi] = _ffn(xs[gi], norm_ffn[i], w_up_bf, w_down_bf, i, streams=(g.n_str, g.n_steps), relayout=relayout)

    for gi, g in enumerate(groups):
        if layouts[gi] != "natural":
            xs[gi] = g.to_natural(xs[gi])
    y_prompt = xs[0].reshape(bp, tp, d)
    y_sample = xs[1].reshape(bs, ts, d)
    st = [[jnp.stack(l) for l in outs[gi]] for gi in range(2)]
    return (y_prompt, y_sample, *st[0], *st[1])
```

```python
import functools
import math

import jax
import jax.numpy as jnp
from jax import lax
from jax.experimental import pallas as pl
from jax.experimental.pallas import tpu as pltpu

f32 = jnp.float32
bf16 = jnp.bfloat16
i32 = jnp.int32

LANES = 128
SUBLANES = 8
VMEM_LIMIT = 56 * 1024 * 1024

CHUNK = 64
EPS = 1e-6
N_HEADS = 16
N_KV_HEADS = 4
HEAD_DIM = 128
GQA_GROUP = N_HEADS // N_KV_HEADS
ROT_DIM = HEAD_DIM // 4
ROPE_THETA = 500000.0
ATTN_SCALE = HEAD_DIM ** -0.5
N_IDX_HEADS = 16
IDX_DIM = 64
IDX_ROT_DIM = IDX_DIM // 4
TOPK_MAX = 256
LRU_BLOCK = 128
LRU_CONV = 4
LRU_C = 8.0
S5_GROUP = 16
S5_STATE = 64
SCONV_W = 3
PROMPT_STREAMS = 8

NEG_BIG = -1e30
INT_MIN = -(2 ** 31)
LOG2E = math.log2(math.e)
EXP_SCALE = ATTN_SCALE * LOG2E


def _cparams(n_axes):
    return pltpu.CompilerParams(dimension_semantics=("arbitrary",) * n_axes, vmem_limit_bytes=VMEM_LIMIT)


def _pick(n, cands):
    for c in cands:
        if n % c == 0:
            return c
    raise ValueError(f"no tile in {cands} divides {n}")


def _sigmoid(x):
    return 0.5 + 0.5 * jnp.tanh(0.5 * x)


def _rmsnorm_rows(x, g):
    ms = jnp.mean(x * x, axis=-1, keepdims=True)
    return x * lax.rsqrt(ms + EPS) * g


def _mm_body(*refs, norm, n_w, has_res, glu):
    it = iter(refs)
    x_ref = next(it)
    g_ref = next(it) if norm else None
    w_refs = [next(it) for _ in range(n_w)]
    r_ref = next(it) if has_res else None
    o_ref = next(it)
    h_ref = next(it) if norm else None

    if norm:
        @pl.when(pl.program_id(1) == 0)
        def _():
            h_ref[...] = _rmsnorm_rows(x_ref[...], g_ref[...]).astype(bf16)
        h = h_ref[...]
    else:
        h = x_ref[...]
    acc = jnp.dot(h, w_refs[0][...], preferred_element_type=f32)
    if glu:
        acc = acc * jax.nn.sigmoid(jnp.dot(h, w_refs[1][...], preferred_element_type=f32))
    if has_res:
        acc = acc + r_ref[...]
    o_ref[...] = acc.astype(o_ref.dtype)


def _matmul(x, ws, *, gain=None, residual=None, out_dtype=f32, glu=False, name="mm"):
    n, k = x.shape
    m = ws[0].shape[1]
    norm = gain is not None
    tm = _pick(n, (1024, 512, 256, 128, 64))
    tn = _pick(m, (1024, 768, 512, 256, 128))
    in_specs = [pl.BlockSpec((tm, k), lambda i, j: (i, 0))]
    args = [x]
    if norm:
        in_specs.append(pl.BlockSpec((1, k), lambda i, j: (0, 0)))
        args.append(gain.reshape(1, k).astype(f32))
    for w in ws:
        in_specs.append(pl.BlockSpec((k, tn), lambda i, j: (0, j)))
        args.append(w)
    if residual is not None:
        in_specs.append(pl.BlockSpec((tm, tn), lambda i, j: (i, j)))
        args.append(residual)
    body = functools.partial(_mm_body, norm=norm, n_w=len(ws), has_res=residual is not None, glu=glu)
    return pl.pallas_call(
        body,
        grid=(n // tm, m // tn),
        in_specs=in_specs,
        out_specs=pl.BlockSpec((tm, tn), lambda i, j: (i, j)),
        out_shape=jax.ShapeDtypeStruct((n, m), out_dtype),
        scratch_shapes=[pltpu.VMEM((tm, k), bf16)] if norm else [],
        compiler_params=_cparams(2),
        name=name,
    )(*args)


def _ffn_body(x_ref, g_ref, wu_ref, wd_ref, o_ref, h_ref, *, relayout):
    tm, d = h_ref.shape

    @pl.when(pl.program_id(1) == 0)
    def _():
        x = x_ref[...]
        if relayout == "to_streams":
            x = jnp.swapaxes(x, 0, 1).reshape(tm, d)
        elif relayout == "to_natural":
            n_str = o_ref.shape[0]
            x = jnp.swapaxes(x.reshape(tm // n_str, n_str, d), 0, 1)
        h_ref[...] = _rmsnorm_rows(x.reshape(tm, d), g_ref[...]).astype(bf16)
        o_ref[...] = x

    u = jnp.dot(h_ref[...], wu_ref[...], preferred_element_type=f32)
    u = jnp.square(jnp.maximum(u, 0.0)).astype(bf16)
    o_ref[...] += jnp.dot(u, wd_ref[...], preferred_element_type=f32).reshape(o_ref.shape)


def _ffn(x, gain, w_up, w_down, layer, *, streams=None, relayout=None):
    n, d = x.shape
    ff = w_up.shape[2]
    tf = _pick(ff, (1024, 512, 256, 128))
    plain = pl.BlockSpec((_pick(n, (512, 256, 128, 64)), d), lambda i, j: (i, 0))
    in_spec = out_spec = plain
    out_shape = (n, d)
    tm = plain.block_shape[0]
    if relayout is not None:
        n_str, n_steps = streams
        tj = _pick(n_steps, (64, 32, 16, 8))
        tm = n_str * tj
        n_run = n_steps // tj
        nat_shape = (n // n_steps, n_steps, d)
        nat_spec = pl.BlockSpec((n_str, tj, d), lambda i, j: (i // n_run, i % n_run, 0))
        str_spec = pl.BlockSpec((tm, d), lambda i, j: (i, 0))
        if relayout == "to_streams":
            x, in_spec, out_spec = x.reshape(nat_shape), nat_spec, str_spec
        else:
            in_spec, out_spec, out_shape = str_spec, nat_spec, nat_shape
    out = pl.pallas_call(
        functools.partial(_ffn_body, relayout=relayout),
        grid=(n // tm, ff // tf),
        in_specs=[
            in_spec,
            pl.BlockSpec((1, d), lambda i, j: (0, 0)),
            pl.BlockSpec((None, d, tf), lambda i, j: (layer, 0, j)),
            pl.BlockSpec((None, tf, d), lambda i, j: (layer, j, 0)),
        ],
        out_specs=out_spec,
        out_shape=jax.ShapeDtypeStruct(out_shape, f32),
        scratch_shapes=[pltpu.VMEM((tm, d), bf16)],
        compiler_params=_cparams(2),
        name="ffn",
    )(x, gain.reshape(1, d).astype(f32), w_up, w_down)
    return out.reshape(n, d)


def _norm_body(x_ref, g_ref, o_ref):
    o_ref[...] = _rmsnorm_rows(x_ref[...], g_ref[...])


def _norm(x, gain):
    n, d = x.shape
    tm = _pick(n, (512, 256, 128, 64))
    return pl.pallas_call(
        _norm_body,
        grid=(n // tm,),
        in_specs=[pl.BlockSpec((tm, d), lambda i: (i, 0)), pl.BlockSpec((1, d), lambda i: (0, 0))],
        out_specs=pl.BlockSpec((tm, d), lambda i: (i, 0)),
        out_shape=jax.ShapeDtypeStruct((n, d), f32),
        compiler_params=_cparams(1),
        name="rmsnorm",
    )(x, gain.reshape(1, d).astype(f32))


def _rope_tables(pos, width, rot):
    half = rot // 2
    inv = ROPE_THETA ** (-jnp.arange(half, dtype=f32) / half)
    ang = pos.astype(f32)[:, None] * inv[None, :]
    cos, sin = jnp.cos(ang), jnp.sin(ang)
    n = pos.shape[0]
    pad = jnp.zeros((n, width - rot), f32)
    zero = jnp.zeros((n, half), f32)
    c = jnp.concatenate([cos, cos, pad + 1.0], axis=1)
    s_up = jnp.concatenate([-sin, zero, pad], axis=1)
    s_dn = jnp.concatenate([zero, sin, pad], axis=1)
    reps = LANES // width
    return tuple(jnp.tile(t, (1, reps)) for t in (c, s_up, s_dn))


def _rope(x, c, s_up, s_dn, half):
    return x * c + pltpu.roll(x, LANES - half, axis=1) * s_up + pltpu.roll(x, half, axis=1) * s_dn


def _attn_post_body(q_ref, k_ref, qi_ref, kw_ref, c_ref, su_ref, sd_ref, ci_ref, sui_ref, sdi_ref,
                    gq_ref, gk_ref, gki_ref, qo_ref, ko_ref, qio_ref, kio_ref):
    c, su, sd = c_ref[...], su_ref[...], sd_ref[...]
    ci, sui, sdi = ci_ref[...], sui_ref[...], sdi_ref[...]
    for h in range(N_HEADS):
        sl = slice(h * HEAD_DIM, (h + 1) * HEAD_DIM)
        y = _rope(_rmsnorm_rows(q_ref[:, sl], gq_ref[...]), c, su, sd, ROT_DIM // 2)
        qo_ref[:, sl] = y.astype(qo_ref.dtype)
    for h in range(N_KV_HEADS):
        sl = slice(h * HEAD_DIM, (h + 1) * HEAD_DIM)
        ko_ref[:, sl] = _rope(_rmsnorm_rows(k_ref[:, sl], gk_ref[...]), c, su, sd, ROT_DIM // 2)
    for j in range(N_IDX_HEADS // 2):
        y = _rope(qi_ref[:, j * LANES:(j + 1) * LANES], ci, sui, sdi, IDX_ROT_DIM // 2).astype(qio_ref.dtype)
        qio_ref[2 * j] = y[:, :IDX_DIM]
        qio_ref[2 * j + 1] = y[:, IDX_DIM:]
    kw = kw_ref[:, :LANES]
    lane = lax.broadcasted_iota(i32, kw.shape, 1)
    ms = jnp.sum(jnp.where(lane < IDX_DIM, kw * kw, 0.0), axis=-1, keepdims=True) * (1.0 / IDX_DIM)
    y = _rope(kw * lax.rsqrt(ms + EPS) * gki_ref[...], ci, sui, sdi, IDX_ROT_DIM // 2)
    kio_ref[...] = y[:, :IDX_DIM]


def _attn_post(proj, pos, g_q, g_k, g_ki):
    n = proj.shape[0]
    tm = _pick(n, (256, 128, 64))
    q_w, kv_w, iq_w = N_HEADS * HEAD_DIM, N_KV_HEADS * HEAD_DIM, N_IDX_HEADS * IDX_DIM
    tabs = _rope_tables(pos, HEAD_DIM, ROT_DIM) + _rope_tables(pos, IDX_DIM, IDX_ROT_DIM)
    gki = jnp.concatenate([g_ki.astype(f32), jnp.zeros((LANES - IDX_DIM,), f32)]).reshape(1, LANES)
    row = lambda i: (i, 0)
    const = lambda i: (0, 0)
    in_specs = [
        pl.BlockSpec((tm, q_w), row),
        pl.BlockSpec((tm, kv_w), lambda i: (i, q_w // kv_w)),
        pl.BlockSpec((tm, iq_w), lambda i: (i, (q_w + 2 * kv_w) // iq_w)),
        pl.BlockSpec((tm, kv_w), lambda i: (i, (q_w + 2 * kv_w + iq_w) // kv_w)),
    ] + [pl.BlockSpec((tm, LANES), row)] * 6 + [pl.BlockSpec((1, LANES), const)] * 3
    out_shape = (
        jax.ShapeDtypeStruct((n, q_w), bf16),
        jax.ShapeDtypeStruct((n, kv_w), f32),
        jax.ShapeDtypeStruct((N_IDX_HEADS, n, IDX_DIM), bf16),
        jax.ShapeDtypeStruct((n, IDX_DIM), f32),
    )
    out_specs = (
        pl.BlockSpec((tm, q_w), row),
        pl.BlockSpec((tm, kv_w), row),
        pl.BlockSpec((N_IDX_HEADS, tm, IDX_DIM), lambda i: (0, i, 0)),
        pl.BlockSpec((tm, IDX_DIM), row),
    )
    return pl.pallas_call(
        _attn_post_body,
        grid=(n // tm,),
        in_specs=in_specs,
        out_specs=out_specs,
        out_shape=out_shape,
        compiler_params=_cparams(1),
        name="attn_post",
    )(proj, proj, proj, proj, *tabs, g_q.reshape(1, -1).astype(f32), g_k.reshape(1, -1).astype(f32), gki)


def _sortable(x):
    b = pltpu.bitcast(x, i32)
    return b ^ ((b >> 31) & 0x7FFFFFFF)


def _fold_rows(x, op):
    return op(x.reshape(x.shape[0] // SUBLANES, SUBLANES, x.shape[1]), axis=0)


def _select_bias(qi_refs, wi_ref, ki_seg_refs, keys_ref, bias_ref, *, seg_lens, tq, n_sel, causal, q_tile, blk,
                 n_blk, n_chunks):
    n_seg = len(seg_lens)
    n_pack = LANES // tq
    offs = [sum(seg_lens[:s]) for s in range(n_seg)]
    wi_rows = [wi_ref[h:h + 1, :] for h in range(N_IDX_HEADS)]
    qi_rows = []
    for h in range(N_IDX_HEADS):
        for w in range(n_pack):
            qh = qi_refs[w][h]
            parts = [qh if v == w else jnp.zeros_like(qh) for v in range(n_pack)]
            qi_rows.append(jnp.concatenate(parts, axis=1) if n_pack > 1 else qh)
    qi_all = jnp.concatenate(qi_rows, axis=0)

    for s in range(n_seg):
        ki_refs = ki_seg_refs[s]
        lc = _pick(seg_lens[s], (256, 128, 64))

        def score_chunk(c, carry, ki_refs=ki_refs, lc=lc, off=offs[s]):
            r0 = pl.multiple_of(c * lc, lc)
            kic = [r[pl.ds(r0, lc), :].astype(bf16) for r in ki_refs]
            kic = jnp.concatenate(kic, axis=1) if n_pack > 1 else kic[0]
            sc = lax.dot_general(kic, qi_all, (((1,), (1,)), ((), ())), preferred_element_type=f32)
            acc = jnp.zeros((lc, LANES), f32)
            for h in range(N_IDX_HEADS):
                acc = acc + jnp.maximum(sc[:, h * LANES:(h + 1) * LANES], 0.0) * wi_rows[h]
            key = _sortable(acc)
            if causal:
                kpos = r0 + lax.broadcasted_iota(i32, (lc, LANES), 0)
                qpos = q_tile * tq + lax.broadcasted_iota(i32, (lc, LANES), 1)
                key = jnp.where((kpos // CHUNK) <= (qpos // CHUNK), key, INT_MIN)
            keys_ref[pl.ds(pl.multiple_of(off + r0, CHUNK), lc), :] = key
            return carry

        lax.fori_loop(0, n_chunks(s, lc), score_chunk, 0)

    def count(pred):
        def body(i, acc):
            r0 = pl.multiple_of(i * blk, blk)
            hit = jnp.where(pred(keys_ref[pl.ds(r0, blk), :], r0), 1, 0)
            return acc + _fold_rows(hit, jnp.sum)
        acc = lax.fori_loop(0, n_blk, body, jnp.zeros((SUBLANES, LANES), i32))
        return jnp.sum(acc, axis=0, keepdims=True)

    def count_ge(cand):
        return count(lambda k, r0: k >= cand)

    ans = jnp.where(count_ge(jnp.zeros((1, LANES), i32)) >= n_sel, 0, INT_MIN)

    def bit_step(i, ans):
        cand = ans | (1 << (30 - i))
        return jnp.where(count_ge(cand) >= n_sel, cand, ans)

    ans = lax.fori_loop(0, 31, bit_step, ans)
    tau = jnp.maximum(ans, INT_MIN + 1)
    n_ge = count_ge(tau)
    ties = jnp.max(n_ge) > n_sel

    @pl.when(jnp.logical_not(ties))
    def _():
        def body(i, carry):
            r0 = pl.multiple_of(i * blk, blk)
            keep = keys_ref[pl.ds(r0, blk), :] >= tau
            bias_ref[pl.ds(r0, blk), :] = jnp.where(keep, 0.0, NEG_BIG).astype(bias_ref.dtype)
            return carry
        lax.fori_loop(0, n_blk, body, 0)

    @pl.when(ties)
    def _():
        def row_idx(r0):
            return r0 + lax.broadcasted_iota(i32, (blk, LANES), 0)
        need = n_sel - count(lambda k, r0: k > tau)

        def idx_step(i, m):
            cand = m | (1 << (14 - i))
            n_below = count(lambda k, r0: jnp.logical_and(k == tau, row_idx(r0) < cand))
            return jnp.where(n_below <= need, cand, m)

        m_cut = lax.fori_loop(0, 15, idx_step, jnp.zeros((1, LANES), i32))

        def body(i, carry):
            r0 = pl.multiple_of(i * blk, blk)
            k = keys_ref[pl.ds(r0, blk), :]
            keep = jnp.logical_or(k > tau, jnp.logical_and(k == tau, row_idx(r0) < m_cut))
            bias_ref[pl.ds(r0, blk), :] = jnp.where(keep, 0.0, NEG_BIG).astype(bias_ref.dtype)
            return carry
        lax.fori_loop(0, n_blk, body, 0)


def _augmented_queries(q_heads, tq, lane0):
    qg = jnp.concatenate(q_heads, axis=0)
    wq = qg.shape[0]
    bias_lane = lane0 + lax.broadcasted_iota(i32, (wq, LANES), 0) % tq
    one_hot = (lax.broadcasted_iota(i32, (wq, LANES), 1) == bias_lane).astype(bf16)
    return jnp.concatenate([qg, one_hot], axis=1)


def _write_heads(o_ref, acc, l_acc, col0, tq):
    o_t = (acc / jnp.sum(l_acc, axis=0, keepdims=True)).T
    for j in range(GQA_GROUP):
        c = col0 + j * HEAD_DIM
        o_ref[:, c:c + HEAD_DIM] = o_t[j * tq:(j + 1) * tq, :].astype(o_ref.dtype)


def _attn_past_body(*refs, segs, tq, n_sel):
    n_seg = len(segs)
    n_pack = LANES // tq
    seg_lens = [n for n, _ in segs]
    it = iter(refs)
    qi_refs = [next(it) for _ in range(n_pack)]
    wi_ref, q_ref = next(it), next(it)
    seg_refs = [([next(it) for _ in range(n_pack)], next(it), next(it)) for _ in range(n_seg)]
    o_ref = next(it)
    keys_ref, bias_ref, sc_ref, acc_ref = (next(it) for _ in range(4))

    l_tot = sum(seg_lens)
    offs = [sum(seg_lens[:s]) for s in range(n_seg)]
    wq = GQA_GROUP * tq
    in_pack = pl.program_id(1) // N_KV_HEADS
    g_head = pl.program_id(1) % N_KV_HEADS
    blk = _pick(l_tot, (512, 320, 256, 192, 128, 64))
    n_chunks = lambda s, lc: seg_lens[s] // lc

    @pl.when(pl.program_id(1) == 0)
    def _():
        _select_bias(qi_refs, wi_ref, [r[0] for r in seg_refs], keys_ref, bias_ref, seg_lens=seg_lens, tq=tq,
                     n_sel=n_sel, causal=False, q_tile=0, blk=blk, n_blk=l_tot // blk, n_chunks=n_chunks)

    q_aug = _augmented_queries([q_ref[:, j * HEAD_DIM:(j + 1) * HEAD_DIM] for j in range(GQA_GROUP)], tq, in_pack * tq)

    def read_kv(ref, interleaved, r0, lc):
        if interleaved:
            return ref[pl.ds(r0 * N_KV_HEADS + g_head, lc, stride=N_KV_HEADS), :].astype(bf16)
        return ref[pl.ds(r0, lc), :].astype(bf16)

    m_acc = jnp.full((SUBLANES, wq), NEG_BIG, f32)
    for s in range(n_seg):
        k_ref = seg_refs[s][1]
        lc = _pick(seg_lens[s], (1024, 512, 256, 128, 64))

        def logits_chunk(c, m_acc, k_ref=k_ref, lc=lc, off=offs[s], inter=segs[s][1]):
            r0 = pl.multiple_of(c * lc, lc)
            row = pl.multiple_of(off + r0, CHUNK)
            k_aug = jnp.concatenate([read_kv(k_ref, inter, r0, lc), bias_ref[pl.ds(row, lc), :]], axis=1)
            sc = lax.dot_general(k_aug, q_aug, (((1,), (1,)), ((), ())), preferred_element_type=f32)
            sc_ref[pl.ds(row, lc), :] = sc
            return jnp.maximum(m_acc, _fold_rows(sc, jnp.max))

        m_acc = lax.fori_loop(0, n_chunks(s, lc), logits_chunk, m_acc)
    m_row = jnp.max(m_acc, axis=0, keepdims=True)

    acc_ref[...] = jnp.zeros(acc_ref.shape, f32)
    l_acc = jnp.zeros((SUBLANES, wq), f32)
    for s in range(n_seg):
        v_ref = seg_refs[s][2]
        lc = _pick(seg_lens[s], (1024, 512, 256, 128, 64))

        def value_chunk(c, l_acc, v_ref=v_ref, lc=lc, off=offs[s], inter=segs[s][1]):
            r0 = pl.multiple_of(c * lc, lc)
            row = pl.multiple_of(off + r0, CHUNK)
            p = jnp.exp2((sc_ref[pl.ds(row, lc), :] - m_row) * EXP_SCALE)
            vc = read_kv(v_ref, inter, r0, lc)
            acc_ref[...] += lax.dot_general(vc, p.astype(bf16), (((0,), (0,)), ((), ())), preferred_element_type=f32)
            return l_acc + _fold_rows(p, jnp.sum)

        l_acc = lax.fori_loop(0, n_chunks(s, lc), value_chunk, l_acc)

    _write_heads(o_ref, acc_ref[...], l_acc, 0, tq)


def _attn_past(q, qi, wi_t, segs, *, n_batch, t_seq, n_sel):
    n = q.shape[0]
    tq = t_seq
    n_pack = LANES // tq
    assert n_batch % n_pack == 0 and LANES % tq == 0
    wq = GQA_GROUP * tq
    kv_w = GQA_GROUP * HEAD_DIM
    batch_of = lambda bp, e: bp * n_pack + e // N_KV_HEADS
    head_of = lambda e: e % N_KV_HEADS
    in_specs, args = [], []
    for w in range(n_pack):
        in_specs.append(pl.BlockSpec((N_IDX_HEADS, tq, IDX_DIM), lambda bp, e, w=w: (0, bp * n_pack + w, 0)))
        args.append(qi)
    in_specs += [
        pl.BlockSpec((None, N_IDX_HEADS, LANES), lambda bp, e: (bp, 0, 0)),
        pl.BlockSpec((tq, kv_w), lambda bp, e: (batch_of(bp, e), head_of(e))),
    ]
    args += [wi_t, q]
    seg_meta = []
    for ki, k, v, n_keys, imap, interleaved in segs:
        lead = (None,) * (ki.ndim - 2)
        kv_rows = n_keys * N_KV_HEADS if interleaved else n_keys
        kv_map = lambda bp, e, imap=imap, inter=interleaved: imap(batch_of(bp, e), 0 if inter else head_of(e))
        for w in range(n_pack):
            in_specs.append(pl.BlockSpec(lead + (n_keys, IDX_DIM), lambda bp, e, imap=imap, w=w: imap(bp * n_pack + w, 0)))
            args.append(ki)
        in_specs += [pl.BlockSpec(lead + (kv_rows, HEAD_DIM), kv_map)] * 2
        args += [k, v]
        seg_meta.append((n_keys, interleaved))
    l_tot = sum(m[0] for m in seg_meta)
    body = functools.partial(_attn_past_body, segs=tuple(seg_meta), tq=tq, n_sel=n_sel)
    return pl.pallas_call(
        body,
        grid=(n_batch // n_pack, n_pack * N_KV_HEADS),
        in_specs=in_specs,
        out_specs=pl.BlockSpec((tq, kv_w), lambda bp, e: (batch_of(bp, e), head_of(e))),
        out_shape=jax.ShapeDtypeStruct((n, N_HEADS * HEAD_DIM), bf16),
        scratch_shapes=[
            pltpu.VMEM((l_tot, LANES), i32),
            pltpu.VMEM((l_tot, LANES), bf16),
            pltpu.VMEM((l_tot, wq), f32),
            pltpu.VMEM((HEAD_DIM, wq), f32),
        ],
        compiler_params=_cparams(2),
        name="attn_past",
    )(*args)


def _attn_causal_body(qi_ref, wi_ref, q_ref, ki_ref, k_ref, v_ref, o_ref, keys_ref, bias_ref, sc_ref, acc_ref,
                      m_ref, l_ref, *, t_seq, n_sel):
    tq = LANES
    wq = GQA_GROUP * tq
    q_tile = pl.program_id(1)
    lc = _pick(t_seq, (1024, 512))
    n_lc = jnp.minimum(((q_tile + 1) * tq + lc - 1) // lc, t_seq // lc)
    blk = 512
    _select_bias([qi_ref], wi_ref, [[ki_ref]], keys_ref, bias_ref, seg_lens=[t_seq], tq=tq, n_sel=n_sel, causal=True,
                 q_tile=q_tile, blk=blk, n_blk=n_lc * (lc // blk), n_chunks=lambda s, c: n_lc * (lc // c))

    def q_aug(g):
        heads = [q_ref[:, (g * GQA_GROUP + j) * HEAD_DIM:(g * GQA_GROUP + j + 1) * HEAD_DIM] for j in range(GQA_GROUP)]
        return _augmented_queries(heads, tq, 0)

    def fold_lanes(x, op):
        parts = [x[:, j * LANES:(j + 1) * LANES] for j in range(x.shape[1] // LANES)]
        while len(parts) > 1:
            parts = [op(parts[i], parts[i + 1]) for i in range(0, len(parts) - 1, 2)] + parts[len(parts) & ~1:]
        return parts[0]

    def logits(g, qa, c):
        r0 = pl.multiple_of(c * lc, lc)
        k_aug = jnp.concatenate([k_ref[pl.ds(r0, lc), g * HEAD_DIM:(g + 1) * HEAD_DIM], bias_ref[pl.ds(r0, lc), :]], axis=1)
        sc = lax.dot_general(qa, k_aug, (((1,), (1,)), ((), ())), preferred_element_type=f32)
        sc_ref[g % 2, :, pl.ds(r0, lc)] = sc
        m_ref[(g % 2) * wq:(g % 2 + 1) * wq, :] = jnp.maximum(m_ref[(g % 2) * wq:(g % 2 + 1) * wq, :], fold_lanes(sc, jnp.maximum))

    def values(g, m_all, c):
        r0 = pl.multiple_of(c * lc, lc)
        sc = sc_ref[g % 2, :, pl.ds(r0, lc)]
        p = jnp.exp2((sc - jnp.concatenate([m_all] * (lc // LANES), axis=1)) * EXP_SCALE)
        l_ref[...] += fold_lanes(p, jnp.add)
        vc = v_ref[pl.ds(r0, lc), g * HEAD_DIM:(g + 1) * HEAD_DIM]
        acc_ref[...] += jnp.dot(p.astype(bf16), vc, preferred_element_type=f32)

    def reset_max(g):
        m_ref[(g % 2) * wq:(g % 2 + 1) * wq, :] = jnp.full((wq, LANES), NEG_BIG, f32)

    def loop(body):
        lax.fori_loop(0, n_lc, lambda c, carry: (body(c), carry)[1], 0)

    reset_max(0)
    qa = q_aug(0)
    loop(lambda c, qa=qa: logits(0, qa, c))
    for g in range(N_KV_HEADS):
        m_part = m_ref[(g % 2) * wq:(g % 2 + 1) * wq, :]
        m_all = jnp.broadcast_to(jnp.max(m_part, axis=1, keepdims=True), (wq, LANES))
        acc_ref[...] = jnp.zeros(acc_ref.shape, f32)
        l_ref[...] = jnp.zeros(l_ref.shape, f32)
        if g + 1 < N_KV_HEADS:
            reset_max(g + 1)
            qa = q_aug(g + 1)

            def both(c, g=g, m_all=m_all, qa=qa):
                values(g, m_all, c)
                logits(g + 1, qa, c)

            loop(both)
        else:
            loop(lambda c, g=g, m_all=m_all: values(g, m_all, c))
        o = acc_ref[...] / jnp.sum(l_ref[...], axis=1, keepdims=True)
        for j in range(GQA_GROUP):
            col = (g * GQA_GROUP + j) * HEAD_DIM
            o_ref[:, col:col + HEAD_DIM] = o[j * tq:(j + 1) * tq, :].astype(o_ref.dtype)


def _attn_causal(q, qi, wi_t, ki, k, v, *, n_batch, t_seq, n_sel):
    n = q.shape[0]
    tq = LANES
    assert t_seq % 512 == 0
    nq = t_seq // tq
    q_w, kv_w = N_HEADS * HEAD_DIM, N_KV_HEADS * HEAD_DIM
    row = lambda b, i: (b * nq + i, 0)
    whole = lambda b, i: (b, 0, 0)
    body = functools.partial(_attn_causal_body, t_seq=t_seq, n_sel=n_sel)
    return pl.pallas_call(
        body,
        grid=(n_batch, nq),
        in_specs=[
            pl.BlockSpec((N_IDX_HEADS, tq, IDX_DIM), lambda b, i: (0, b * nq + i, 0)),
            pl.BlockSpec((None, N_IDX_HEADS, LANES), lambda b, i: (b, 0, i)),
            pl.BlockSpec((tq, q_w), row),
            pl.BlockSpec((None, t_seq, IDX_DIM), whole),
            pl.BlockSpec((None, t_seq, kv_w), whole),
            pl.BlockSpec((None, t_seq, kv_w), whole),
        ],
        out_specs=pl.BlockSpec((tq, q_w), row),
        out_shape=jax.ShapeDtypeStruct((n, q_w), bf16),
        scratch_shapes=[
            pltpu.VMEM((t_seq, LANES), i32),
            pltpu.VMEM((t_seq, LANES), bf16),
            pltpu.VMEM((2, GQA_GROUP * tq, t_seq), f32),
            pltpu.VMEM((GQA_GROUP * tq, HEAD_DIM), f32),
            pltpu.VMEM((2 * GQA_GROUP * tq, LANES), f32),
            pltpu.VMEM((GQA_GROUP * tq, LANES), f32),
        ],
        compiler_params=_cparams(2),
        name="attn_causal",
    )(qi, wi_t, q, ki, k, v)


def _attn_group(q, qi, wi, ki_new, k_new, v_new, past, *, n_batch, t_seq, offset):
    kv_w = N_KV_HEADS * HEAD_DIM
    n_pack = max(1, LANES // t_seq)
    wi_g = wi.reshape(n_batch // n_pack, n_pack, t_seq, N_IDX_HEADS).transpose(0, 3, 1, 2)
    wi_g = wi_g.reshape(n_batch // n_pack, N_IDX_HEADS, n_pack * t_seq)
    ki_g = ki_new.reshape(n_batch, t_seq, IDX_DIM)
    n_sel = min(TOPK_MAX, (offset + t_seq) // 4)

    if past is None:
        k_g = k_new.astype(bf16).reshape(n_batch, t_seq, kv_w)
        v_g = v_new.astype(bf16).reshape(n_batch, t_seq, kv_w)
        return _attn_causal(q, qi, wi_g, ki_g, k_g, v_g, n_batch=n_batch, t_seq=t_seq, n_sel=n_sel)

    assert offset % CHUNK == 0 and t_seq <= CHUNK
    k_g = k_new.reshape(n_batch, t_seq, kv_w)
    v_g = v_new.reshape(n_batch, t_seq, kv_w)
    pk, pv, pki, r = past
    n_past = pk.shape[2]
    pk = pk.reshape(pk.shape[0], n_batch, n_past * N_KV_HEADS, HEAD_DIM)
    pv = pv.reshape(pv.shape[0], n_batch, n_past * N_KV_HEADS, HEAD_DIM)
    past_map = lambda b, hb: (r, b, 0, 0)
    new_map = lambda b, hb: (b, 0, hb)
    segs = [(pki, pk, pv, n_past, past_map, True), (ki_g, k_g, v_g, t_seq, new_map, False)]
    return _attn_past(q, qi, wi_g, segs, n_batch=n_batch, t_seq=t_seq, n_sel=n_sel)


def _attn_layer(x, group, past, gain, w_in_p, g_q, g_k, g_ki, w_o):
    q_w, kv_w, iq_w = N_HEADS * HEAD_DIM, N_KV_HEADS * HEAD_DIM, N_IDX_HEADS * IDX_DIM
    proj = _matmul(x, [w_in_p], gain=gain, name="attn_in")
    pos = group.offset + jnp.tile(jnp.arange(group.t_seq), group.n_batch)
    q, k, qi, ki = _attn_post(proj, pos, g_q, g_k, g_ki)
    v = proj[:, q_w + kv_w:q_w + 2 * kv_w]
    wi = proj[:, q_w + 2 * kv_w + iq_w + IDX_DIM:q_w + 2 * kv_w + iq_w + IDX_DIM + N_IDX_HEADS]
    o = _attn_group(q, qi, wi, ki, k, v, past, n_batch=group.n_batch, t_seq=group.t_seq, offset=group.offset)
    x = _matmul(o, [w_o], residual=x, name="attn_out")
    return x, k, v, ki


def _halo_rows(x_ref, init, j, *, n_str, n_steps, chained):
    if not chained:
        return init
    prev = x_ref[(n_steps - j) * n_str:(n_steps - j + 1) * n_str, :]
    shifted = pltpu.roll(prev, 1, axis=0)
    first = lax.broadcasted_iota(i32, shifted.shape, 0) == 0
    return jnp.where(first, init, shifted)


def _chain_carries(h_end, p_end, c0):
    rows = [c0]
    for s in range(1, h_end.shape[0]):
        rows.append(p_end[s - 1:s] * rows[-1] + h_end[s - 1:s])
    return jnp.concatenate(rows, axis=0)


def _lru_core_body(xb_ref, gate_ref, cw_ref, cb_ref, wa_ref, ba_ref, wx_ref, bx_ref, lam_ref, buf_ref, h0_ref,
                   y_ref, hl_ref, a_ref, b_ref, *, n_str, n_steps, chained, lw):
    rows = n_str * n_steps
    ch = _pick(n_steps, (16, 8, 4, 2, 1)) * n_str
    n_halo = LRU_CONV - 1
    cw = [cw_ref[j:j + 1, :] for j in range(LRU_CONV)]
    cb = cb_ref[...]
    sp = jax.nn.softplus(-lam_ref[...])
    halo = jnp.concatenate(
        [_halo_rows(xb_ref, buf_ref[n_halo - j], j, n_str=n_str, n_steps=n_steps, chained=chained)
         for j in range(n_halo, 0, -1)], axis=0)

    def gates(xin, r0):
        xc = cb
        for j in range(LRU_CONV):
            xc = xc + xin[j * n_str:j * n_str + ch, :] * cw[j]
        xcb = xc.astype(bf16)
        for t in range(lw // LRU_BLOCK):
            sl = slice(t * LRU_BLOCK, (t + 1) * LRU_BLOCK)
            xt = xcb[:, sl]
            r = _sigmoid(jnp.dot(xt, wa_ref[t], preferred_element_type=f32) + ba_ref[:, sl])
            ig = _sigmoid(jnp.dot(xt, wx_ref[t], preferred_element_type=f32) + bx_ref[:, sl])
            log_a = (-LRU_C) * r * sp[:, sl]
            th = jnp.tanh(log_a)
            mult = jnp.sqrt(-2.0 * th / (1.0 - th))
            a_ref[pl.ds(r0, ch), sl] = jnp.exp(log_a)
            b_ref[pl.ds(r0, ch), sl] = mult * (ig * xc[:, sl])

    gates(jnp.concatenate([halo, xb_ref[0:ch, :]], axis=0), 0)

    def gate_chunk(c, carry):
        r0 = pl.multiple_of(c * ch, ch)
        gates(xb_ref[pl.ds(r0 - n_halo * n_str, ch + n_halo * n_str), :], r0)
        return carry

    lax.fori_loop(1, rows // ch, gate_chunk, 0)

    def scan_step(t, carry):
        r0 = pl.multiple_of(t * n_str, n_str)
        a = a_ref[pl.ds(r0, n_str), :]
        h = a * carry[0] + b_ref[pl.ds(r0, n_str), :]
        b_ref[pl.ds(r0, n_str), :] = h
        if not chained:
            return (h,)
        p = a * carry[1]
        a_ref[pl.ds(r0, n_str), :] = p
        return (h, p)

    if chained:
        init = (jnp.zeros((n_str, lw), f32), jnp.ones((n_str, lw), f32))
    else:
        init = (h0_ref[...],)
    fin = lax.fori_loop(0, n_steps, scan_step, init, unroll=8)

    if chained:
        carries = _chain_carries(fin[0], fin[1], h0_ref[...])
        c_rep = jnp.concatenate([carries] * (ch // n_str), axis=0)

    def out_chunk(c, carry):
        r0 = pl.multiple_of(c * ch, ch)
        h = b_ref[pl.ds(r0, ch), :]
        if chained:
            h = h + a_ref[pl.ds(r0, ch), :] * c_rep
            b_ref[pl.ds(r0, ch), :] = h
        y_ref[pl.ds(r0, ch), :] = (h * jax.nn.gelu(gate_ref[pl.ds(r0, ch), :])).astype(y_ref.dtype)
        return carry

    lax.fori_loop(0, rows // ch, out_chunk, 0)
    if chained:
        hl_ref[...] = b_ref[rows - 1:rows, :]
    else:
        hl_ref[...] = fin[0]


def _lru_core(proj, params, buf, h0, *, n_seq, n_str, n_steps, chained):
    w = proj.shape[1] // 2
    lw = 256
    rows = n_str * n_steps
    ns = 1 if chained else n_str
    cw, cb, wa, ba, wx, bx, lam = params
    nb = lw // LRU_BLOCK
    vec = lambda s, n: (0, n)
    in_specs = [
        pl.BlockSpec((rows, lw), lambda s, n: (s, w // lw + n)),
        pl.BlockSpec((rows, lw), lambda s, n: (s, n)),
        pl.BlockSpec((LRU_CONV, lw), vec),
        pl.BlockSpec((1, lw), vec),
        pl.BlockSpec((nb, LRU_BLOCK, LRU_BLOCK), lambda s, n: (n, 0, 0)),
        pl.BlockSpec((1, lw), vec),
        pl.BlockSpec((nb, LRU_BLOCK, LRU_BLOCK), lambda s, n: (n, 0, 0)),
        pl.BlockSpec((1, lw), vec),
        pl.BlockSpec((1, lw), vec),
        pl.BlockSpec((None, LRU_CONV - 1, ns, lw), lambda s, n: (s, 0, 0, n)),
        pl.BlockSpec((None, ns, lw), lambda s, n: (s, 0, n)),
    ]
    body = functools.partial(_lru_core_body, n_str=n_str, n_steps=n_steps, chained=chained, lw=lw)
    return pl.pallas_call(
        body,
        grid=(n_seq, w // lw),
        in_specs=in_specs,
        out_specs=(
            pl.BlockSpec((rows, lw), lambda s, n: (s, n)),
            pl.BlockSpec((None, ns, lw), lambda s, n: (s, 0, n)),
        ),
        out_shape=(
            jax.ShapeDtypeStruct((n_seq * rows, w), bf16),
            jax.ShapeDtypeStruct((n_seq, ns, w), f32),
        ),
        scratch_shapes=[pltpu.VMEM((rows, lw), f32), pltpu.VMEM((rows, lw), f32)],
        compiler_params=_cparams(2),
        name="lru_core",
    )(proj, proj, cw, cb.reshape(1, w), wa, ba.reshape(1, w), wx, bx.reshape(1, w), lam.reshape(1, w), buf, h0)


def _s5_core_body(u_ref, bm_ref, cm_ref, are_ref, aim_ref, d_ref, sre_ref, sim_ref,
                  y_ref, ore_ref, oim_ref, hs_ref, p_ref, *, n_str, n_steps, chained):
    rows = n_str * n_steps
    half = hs_ref.shape[1] // 2
    spc = _pick(n_steps, tuple(s for s in (64, 32, 16, 8, 4, 2, 1) if s * n_str <= 512))
    ch = spc * n_str

    def proj_chunk(c, carry):
        r0 = pl.multiple_of(c * ch, ch)
        hs_ref[pl.ds(r0, ch), :] = jnp.dot(u_ref[pl.ds(r0, ch), :].astype(bf16), bm_ref[...], preferred_element_type=f32)
        return carry

    lax.fori_loop(0, rows // ch, proj_chunk, 0)

    a_re = jnp.broadcast_to(are_ref[...], (n_str, half))
    a_im = jnp.broadcast_to(aim_ref[...], (n_str, half))

    def scan_step(t, carry):
        r0 = pl.multiple_of(t * n_str, n_str)
        h_re, h_im = carry[0], carry[1]
        n_re = a_re * h_re - a_im * h_im + hs_ref[pl.ds(r0, n_str), :half]
        n_im = a_re * h_im + a_im * h_re + hs_ref[pl.ds(r0, n_str), half:]
        hs_ref[pl.ds(r0, n_str), :half] = n_re
        hs_ref[pl.ds(r0, n_str), half:] = n_im
        if not chained:
            return (n_re, n_im)
        p_re, p_im = carry[2], carry[3]
        q_re = a_re[:1] * p_re - a_im[:1] * p_im
        q_im = a_re[:1] * p_im + a_im[:1] * p_re
        p_ref[pl.ds(t, 1), :half] = q_re
        p_ref[pl.ds(t, 1), half:] = q_im
        return (n_re, n_im, q_re, q_im)

    if chained:
        zero = jnp.zeros((n_str, half), f32)
        init = (zero, zero, jnp.ones((1, half), f32), jnp.zeros((1, half), f32))
    else:
        init = (sre_ref[...], sim_ref[...])
    fin = lax.fori_loop(0, n_steps, scan_step, init, unroll=2)

    if chained:
        pe_re, pe_im = fin[2], fin[3]
        c_re, c_im = [sre_ref[...]], [sim_ref[...]]
        for s in range(1, n_str):
            prev_re, prev_im = c_re[-1], c_im[-1]
            c_re.append(pe_re * prev_re - pe_im * prev_im + fin[0][s - 1:s])
            c_im.append(pe_re * prev_im + pe_im * prev_re + fin[1][s - 1:s])
        c_re = jnp.concatenate(c_re, axis=0)
        c_im = jnp.concatenate(c_im, axis=0)
        ore_ref[...] = fin[0][n_str - 1:] + pe_re * c_re[n_str - 1:] - pe_im * c_im[n_str - 1:]
        oim_ref[...] = fin[1][n_str - 1:] + pe_re * c_im[n_str - 1:] + pe_im * c_re[n_str - 1:]
        c_re = jnp.concatenate([c_re] * spc, axis=0)
        c_im = jnp.concatenate([c_im] * spc, axis=0)
    else:
        ore_ref[...] = fin[0]
        oim_ref[...] = fin[1]

    def per_stream(p):
        return jnp.broadcast_to(p[:, None, :], (spc, n_str, p.shape[1])).reshape(ch, p.shape[1])

    def out_chunk(c, carry):
        r0 = pl.multiple_of(c * ch, ch)
        u = u_ref[pl.ds(r0, ch), :]
        hs = hs_ref[pl.ds(r0, ch), :]
        if chained:
            t0 = pl.multiple_of(c * spc, spc)
            p_re = per_stream(p_ref[pl.ds(t0, spc), :half])
            p_im = per_stream(p_ref[pl.ds(t0, spc), half:])
            hs = jnp.concatenate([hs[:, :half] + p_re * c_re - p_im * c_im,
                                  hs[:, half:] + p_re * c_im + p_im * c_re], axis=1)
        y = jnp.dot(hs.astype(bf16), cm_ref[...], preferred_element_type=f32) + d_ref[...] * u
        y_ref[pl.ds(r0, ch), :] = jax.nn.gelu(y).astype(y_ref.dtype)
        return carry

    lax.fori_loop(0, rows // ch, out_chunk, 0)


def _s5_core(u, mats, s_re, s_im, *, n_seq, n_str, n_steps, chained):
    d = u.shape[1]
    bm, cm, a_re, a_im, dsk = mats
    n_gb = d // LANES
    sw = bm.shape[2] // 2
    rows = n_str * n_steps
    ns = 1 if chained else n_str
    st_spec = pl.BlockSpec((None, ns, sw), lambda s, g: (s, 0, g))
    in_specs = [
        pl.BlockSpec((rows, LANES), lambda s, g: (s, g)),
        pl.BlockSpec((None, LANES, 2 * sw), lambda s, g: (g, 0, 0)),
        pl.BlockSpec((None, 2 * sw, LANES), lambda s, g: (g, 0, 0)),
        pl.BlockSpec((1, sw), lambda s, g: (0, g)),
        pl.BlockSpec((1, sw), lambda s, g: (0, g)),
        pl.BlockSpec((1, LANES), lambda s, g: (0, g)),
        st_spec,
        st_spec,
    ]
    body = functools.partial(_s5_core_body, n_str=n_str, n_steps=n_steps, chained=chained)
    st_shape = jax.ShapeDtypeStruct((n_seq, ns, n_gb * sw), f32)
    return pl.pallas_call(
        body,
        grid=(n_seq, n_gb),
        in_specs=in_specs,
        out_specs=(pl.BlockSpec((rows, LANES), lambda s, g: (s, g)), st_spec, st_spec),
        out_shape=(jax.ShapeDtypeStruct((n_seq * rows, d), bf16), st_shape, st_shape),
        scratch_shapes=[pltpu.VMEM((rows, 2 * sw), f32), pltpu.VMEM((n_steps if chained else SUBLANES, 2 * sw), f32)],
        compiler_params=_cparams(2),
        name="s5_core",
    )(u, bm, cm, a_re, a_im, dsk, s_re, s_im)


def _s5_matrices(a_re, a_im, log_dt, b_re, b_im, c_re, c_im, d_skip):
    g, p = a_re.shape
    a = lax.complex(a_re.astype(f32), a_im.astype(f32))
    step = jnp.exp(log_dt.astype(f32))[:, None]
    a_bar = jnp.exp(a * step)
    b_bar = ((a_bar - 1.0) / a)[:, :, None] * lax.complex(b_re.astype(f32), b_im.astype(f32))
    gpb = LANES // S5_GROUP
    n_gb = g // gpb
    eye = jnp.eye(gpb, dtype=f32)

    def in_block(m):
        m = m.reshape(n_gb, gpb, p, S5_GROUP)
        return jnp.einsum('ngpc,gh->ngchp', m, eye).reshape(n_gb, gpb * S5_GROUP, gpb * p)

    def out_block(m):
        m = m.reshape(n_gb, gpb, S5_GROUP, p)
        return jnp.einsum('ngcp,gh->ngphc', m, eye).reshape(n_gb, gpb * p, gpb * S5_GROUP)

    bm = jnp.concatenate([in_block(jnp.real(b_bar)), in_block(jnp.imag(b_bar))], axis=2).astype(bf16)
    cm = jnp.concatenate([out_block(c_re.astype(f32)), -out_block(c_im.astype(f32))], axis=1).astype(bf16)
    return bm, cm, jnp.real(a_bar).reshape(1, g * p), jnp.imag(a_bar).reshape(1, g * p), d_skip.reshape(1, -1).astype(f32)


def _sconv_body(bg_ref, cg_ref, v_ref, cw_ref, buf_ref, u_ref, nb_ref, cv_ref, *, n_str, n_steps, chained):
    rows = n_str * n_steps
    ch = _pick(n_steps, (16, 8, 4, 2, 1)) * n_str
    n_halo = SCONV_W - 1
    cw = [cw_ref[j:j + 1, :] for j in range(SCONV_W)]

    def prod_chunk(c, carry):
        r0 = pl.multiple_of(c * ch, ch)
        cv_ref[pl.ds(r0, ch), :] = cg_ref[pl.ds(r0, ch), :] * v_ref[pl.ds(r0, ch), :]
        return carry

    lax.fori_loop(0, rows // ch, prod_chunk, 0)
    halo = jnp.concatenate(
        [_halo_rows(cv_ref, buf_ref[n_halo - j], j, n_str=n_str, n_steps=n_steps, chained=chained)
         for j in range(n_halo, 0, -1)], axis=0)

    def emit(xin, r0):
        z = xin[0:ch, :] * cw[0]
        for j in range(1, SCONV_W):
            z = z + xin[j * n_str:j * n_str + ch, :] * cw[j]
        u_ref[pl.ds(r0, ch), :] = (bg_ref[pl.ds(r0, ch), :] * z).astype(u_ref.dtype)

    emit(jnp.concatenate([halo, cv_ref[0:ch, :]], axis=0), 0)

    def out_chunk(c, carry):
        r0 = pl.multiple_of(c * ch, ch)
        emit(cv_ref[pl.ds(r0 - n_halo * n_str, ch + n_halo * n_str), :], r0)
        return carry

    lax.fori_loop(1, rows // ch, out_chunk, 0)
    for j in range(n_halo):
        step_rows = cv_ref[(n_steps - n_halo + j) * n_str:(n_steps - n_halo + j + 1) * n_str, :]
        nb_ref[j] = step_rows[n_str - 1:, :] if chained else step_rows


def _sconv_core(proj, conv_w, buf, *, n_seq, n_str, n_steps, chained):
    d = proj.shape[1] // 3
    lw = 256
    rows = n_str * n_steps
    ns = 1 if chained else n_str
    nl = d // lw
    body = functools.partial(_sconv_body, n_str=n_str, n_steps=n_steps, chained=chained)
    buf_spec = pl.BlockSpec((None, SCONV_W - 1, ns, lw), lambda s, n: (s, 0, 0, n))
    return pl.pallas_call(
        body,
        grid=(n_seq, nl),
        in_specs=[
            pl.BlockSpec((rows, lw), lambda s, n: (s, n)),
            pl.BlockSpec((rows, lw), lambda s, n: (s, nl + n)),
            pl.BlockSpec((rows, lw), lambda s, n: (s, 2 * nl + n)),
            pl.BlockSpec((SCONV_W, lw), lambda s, n: (0, n)),
            buf_spec,
        ],
        out_specs=(pl.BlockSpec((rows, lw), lambda s, n: (s, n)), buf_spec),
        out_shape=(
            jax.ShapeDtypeStruct((n_seq * rows, d), bf16),
            jax.ShapeDtypeStruct((n_seq, SCONV_W - 1, ns, d), f32),
        ),
        scratch_shapes=[pltpu.VMEM((rows, lw), f32)],
        compiler_params=_cparams(2),
        name="sconv_core",
    )(proj, proj, proj, conv_w, buf)


class _Group:
    def __init__(self, n_batch, t_seq, offset, has_past):
        self.n_batch, self.t_seq, self.offset, self.has_past = n_batch, t_seq, offset, has_past
        self.chained = not has_past
        if self.chained:
            self.n_seq, self.n_str, self.n_steps = n_batch, PROMPT_STREAMS, t_seq // PROMPT_STREAMS
        else:
            self.n_seq, self.n_str, self.n_steps = 1, n_batch, t_seq
        self.n_rows = n_batch * t_seq
        self.dims = dict(n_seq=self.n_seq, n_str=self.n_str, n_steps=self.n_steps, chained=self.chained)

    def to_streams(self, x):
        f = x.shape[-1]
        if self.chained:
            return x.reshape(self.n_batch, self.n_str, self.n_steps, f).transpose(0, 2, 1, 3).reshape(self.n_rows, f)
        return x.reshape(self.n_batch, self.t_seq, f).transpose(1, 0, 2).reshape(self.n_rows, f)

    def to_natural(self, x):
        f = x.shape[-1]
        if self.chained:
            return x.reshape(self.n_batch, self.n_steps, self.n_str, f).transpose(0, 2, 1, 3).reshape(self.n_rows, f)
        return x.reshape(self.t_seq, self.n_batch, f).transpose(1, 0, 2).reshape(self.n_rows, f)

    def last_steps(self, x, k, c0, c1):
        if self.chained:
            rows = [x[b * self.t_seq + (self.n_steps - k + j) * self.n_str + self.n_str - 1, c0:c1]
                    for b in range(self.n_batch) for j in range(k)]
            return jnp.stack(rows).reshape(self.n_batch, k, c1 - c0)
        r0 = (self.t_seq - k) * self.n_batch
        return x[r0:, c0:c1].reshape(k, self.n_batch, c1 - c0).transpose(1, 0, 2)

    def state_in(self, s):
        if s.ndim == 2:
            return s.astype(f32)[None]
        return s.astype(f32).transpose(1, 0, 2)[None]

    def zeros_state(self, k, width):
        shape = (self.n_seq, 1, width) if k is None else (self.n_seq, k, 1, width)
        return jnp.zeros(shape, f32)

    def state_out(self, s):
        return s.reshape(self.n_batch, -1)


def kernel(x_prompt, x_sample, cache_attn_k, cache_attn_v, cache_idx_k, state_lru_conv, state_lru_h, state_s5_re, state_s5_im, state_sconv, norm_mix, norm_ffn, attn_w_in, attn_g_q, attn_g_k, attn_g_ki, attn_w_o, lru_w_in, lru_conv_w, lru_conv_b, lru_w_a, lru_b_a, lru_w_x, lru_b_x, lru_lambda, lru_w_out, s5_a_re, s5_a_im, s5_log_dt, s5_b_re, s5_b_im, s5_c_re, s5_c_im, s5_d, s5_w_glu, sc_w_in, sc_conv_w, sc_w_out, ffn_w_up, ffn_w_down):
    bp, tp, d = x_prompt.shape
    bs, ts, _ = x_sample.shape
    n_past = cache_attn_k.shape[2]
    depth = norm_mix.shape[0]
    n_mix = 4
    groups = [_Group(bp, tp, 0, False), _Group(bs, ts, n_past, True)]
    xs = [x_prompt.reshape(bp * tp, d).astype(f32), x_sample.reshape(bs * ts, d).astype(f32)]
    layouts = ["natural", "natural"]
    outs = [[[] for _ in range(8)] for _ in groups]
    w_up_bf, w_down_bf = ffn_w_up.astype(bf16), ffn_w_down.astype(bf16)

    def wanted(i):
        return "natural" if i >= depth or i % n_mix == 0 else "streams"

    for i in range(depth):
        m, r = i % n_mix, i // n_mix
        for gi, g in enumerate(groups):
            if layouts[gi] != wanted(i):
                xs[gi] = g.to_natural(xs[gi]) if wanted(i) == "natural" else g.to_streams(xs[gi])
                layouts[gi] = wanted(i)
        if m == 0:
            n_in = attn_w_in.shape[2]
            n_pad = -(-n_in // 768) * 768
            w_in_p = jnp.pad(attn_w_in[r], ((0, 0), (0, n_pad - n_in))).astype(bf16)
            w_o = attn_w_o[r].astype(bf16)
            pasts = [None, (cache_attn_k, cache_attn_v, cache_idx_k, r)]
            for gi, g in enumerate(groups):
                xs[gi], k, v, ki = _attn_layer(xs[gi], g, pasts[gi], norm_mix[i], w_in_p, attn_g_q[r], attn_g_k[r],
                                               attn_g_ki[r], w_o)
                outs[gi][0].append(k.reshape(g.n_batch, g.t_seq, N_KV_HEADS, HEAD_DIM))
                outs[gi][1].append(v.reshape(g.n_batch, g.t_seq, N_KV_HEADS, HEAD_DIM))
                outs[gi][2].append(ki.reshape(g.n_batch, g.t_seq, IDX_DIM))
        elif m == 1:
            w = lru_w_in.shape[2] // 2
            w_in, w_out = lru_w_in[r].astype(bf16), lru_w_out[r].astype(bf16)
            params = (lru_conv_w[r], lru_conv_b[r], lru_w_a[r].astype(bf16), lru_b_a[r], lru_w_x[r].astype(bf16),
                      lru_b_x[r], lru_lambda[r])
            for gi, g in enumerate(groups):
                proj = _matmul(xs[gi], [w_in], gain=norm_mix[i], name="lru_in")
                if g.has_past:
                    buf, h0 = g.state_in(state_lru_conv[r]), g.state_in(state_lru_h[r])
                else:
                    buf, h0 = g.zeros_state(LRU_CONV - 1, w), g.zeros_state(None, w)
                y, hl = _lru_core(proj, params, buf, h0, **g.dims)
                outs[gi][3].append(g.last_steps(proj, LRU_CONV - 1, w, 2 * w))
                outs[gi][4].append(g.state_out(hl))
                xs[gi] = _matmul(y, [w_out], residual=xs[gi], name="lru_out")
        elif m == 2:
            mats = _s5_matrices(s5_a_re[r], s5_a_im[r], s5_log_dt[r], s5_b_re[r], s5_b_im[r], s5_c_re[r],
                                s5_c_im[r], s5_d[r])
            n_grp, n_state = s5_a_re.shape[1], s5_a_re.shape[2]
            gp = n_grp * n_state
            wg = s5_w_glu[r].astype(bf16)
            wg = [wg[:, :d], wg[:, d:]]
            for gi, g in enumerate(groups):
                u = _norm(xs[gi], norm_mix[i])
                if g.has_past:
                    s_re = g.state_in(state_s5_re[r].reshape(g.n_batch, gp))
                    s_im = g.state_in(state_s5_im[r].reshape(g.n_batch, gp))
                else:
                    s_re, s_im = g.zeros_state(None, gp), g.zeros_state(None, gp)
                y, o_re, o_im = _s5_core(u, mats, s_re, s_im, **g.dims)
                outs[gi][5].append(g.state_out(o_re).reshape(g.n_batch, n_grp, n_state))
                outs[gi][6].append(g.state_out(o_im).reshape(g.n_batch, n_grp, n_state))
                xs[gi] = _matmul(y, wg, residual=xs[gi], glu=True, name="s5_glu")
        else:
            w_in, w_out = sc_w_in[r].astype(bf16), sc_w_out[r].astype(bf16)
            for gi, g in enumerate(groups):
                proj = _matmul(xs[gi], [w_in], gain=norm_mix[i], name="sconv_in")
                buf = g.state_in(state_sconv[r]) if g.has_past else g.zeros_state(SCONV_W - 1, d)
                u, nb = _sconv_core(proj, sc_conv_w[r], buf, **g.dims)
                outs[gi][7].append(nb.transpose(0, 2, 1, 3).reshape(g.n_batch, SCONV_W - 1, d))
                xs[gi] = _matmul(u, [w_out], residual=xs[gi], name="sconv_out")
        for gi, g in enumerate(groups):
            relayout = None
            if g.chained and layouts[gi] != wanted(i + 1):
                relayout = "to_" + wanted(i + 1)
                layouts[gi] = wanted(i + 1)
            xs[gi] = _ffn(xs[gi], norm_ffn[i], w_up_bf, w_down_bf, i, streams=(g.n_str, g.n_steps), relayout=relayout)

    for gi, g in enumerate(groups):
        if layouts[gi] != "natural":
            xs[gi] = g.to_natural(xs[gi])
    y_prompt = xs[0].reshape(bp, tp, d)
    y_sample = xs[1].reshape(bs, ts, d)
    st = [[jnp.stack(l) for l in outs[gi]] for gi in range(2)]
    return (y_prompt, y_sample, *st[0], *st[1])
```

```python
import functools
import math

import jax
import jax.numpy as jnp
from jax import lax
from jax.experimental import pallas as pl
from jax.experimental.pallas import tpu as pltpu

f32 = jnp.float32
bf16 = jnp.bfloat16
i32 = jnp.int32

LANES = 128
SUBLANES = 8
VMEM_LIMIT = 56 * 1024 * 1024

CHUNK = 64
EPS = 1e-6
N_HEADS = 16
N_KV_HEADS = 4
HEAD_DIM = 128
GQA_GROUP = N_HEADS // N_KV_HEADS
ROT_DIM = HEAD_DIM // 4
ROPE_THETA = 500000.0
ATTN_SCALE = HEAD_DIM ** -0.5
N_IDX_HEADS = 16
IDX_DIM = 64
IDX_ROT_DIM = IDX_DIM // 4
TOPK_MAX = 256
LRU_BLOCK = 128
LRU_CONV = 4
LRU_C = 8.0
S5_GROUP = 16
S5_STATE = 64
SCONV_W = 3
PROMPT_STREAMS = 8

NEG_BIG = -1e30
INT_MIN = -(2 ** 31)
LOG2E = math.log2(math.e)
EXP_SCALE = ATTN_SCALE * LOG2E


def _cparams(n_axes):
    return pltpu.CompilerParams(dimension_semantics=("arbitrary",) * n_axes, vmem_limit_bytes=VMEM_LIMIT)


def _pick(n, cands):
    for c in cands:
        if n % c == 0:
            return c
    raise ValueError(f"no tile in {cands} divides {n}")


def _sigmoid(x):
    return 0.5 + 0.5 * jnp.tanh(0.5 * x)


def _rmsnorm_rows(x, g):
    ms = jnp.mean(x * x, axis=-1, keepdims=True)
    return x * lax.rsqrt(ms + EPS) * g


def _mm_body(*refs, norm, n_w, has_res, glu):
    it = iter(refs)
    x_ref = next(it)
    g_ref = next(it) if norm else None
    w_refs = [next(it) for _ in range(n_w)]
    r_ref = next(it) if has_res else None
    o_ref = next(it)
    h_ref = next(it) if norm else None

    if norm:
        @pl.when(pl.program_id(1) == 0)
        def _():
            h_ref[...] = _rmsnorm_rows(x_ref[...], g_ref[...]).astype(bf16)
        h = h_ref[...]
    else:
        h = x_ref[...]
    acc = jnp.dot(h, w_refs[0][...], preferred_element_type=f32)
    if glu:
        acc = acc * jax.nn.sigmoid(jnp.dot(h, w_refs[1][...], preferred_element_type=f32))
    if has_res:
        acc = acc + r_ref[...]
    o_ref[...] = acc.astype(o_ref.dtype)


def _matmul(x, ws, *, gain=None, residual=None, out_dtype=f32, glu=False, name="mm"):
    n, k = x.shape
    m = ws[0].shape[1]
    norm = gain is not None
    tm = _pick(n, (1024, 512, 256, 128, 64))
    tn = _pick(m, (1024, 768, 512, 256, 128))
    in_specs = [pl.BlockSpec((tm, k), lambda i, j: (i, 0))]
    args = [x]
    if norm:
        in_specs.append(pl.BlockSpec((1, k), lambda i, j: (0, 0)))
        args.append(gain.reshape(1, k).astype(f32))
    for w in ws:
        in_specs.append(pl.BlockSpec((k, tn), lambda i, j: (0, j)))
        args.append(w)
    if residual is not None:
        in_specs.append(pl.BlockSpec((tm, tn), lambda i, j: (i, j)))
        args.append(residual)
    body = functools.partial(_mm_body, norm=norm, n_w=len(ws), has_res=residual is not None, glu=glu)
    return pl.pallas_call(
        body,
        grid=(n // tm, m // tn),
        in_specs=in_specs,
        out_specs=pl.BlockSpec((tm, tn), lambda i, j: (i, j)),
        out_shape=jax.ShapeDtypeStruct((n, m), out_dtype),
        scratch_shapes=[pltpu.VMEM((tm, k), bf16)] if norm else [],
        compiler_params=_cparams(2),
        name=name,
    )(*args)


def _ffn_body(x_ref, g_ref, wu_ref, wd_ref, o_ref, h_ref, *, relayout):
    tm, d = h_ref.shape

    @pl.when(pl.program_id(1) == 0)
    def _():
        x = x_ref[...]
        if relayout == "to_streams":
            x = jnp.swapaxes(x, 0, 1).reshape(tm, d)
        elif relayout == "to_natural":
            n_str = o_ref.shape[0]
            x = jnp.swapaxes(x.reshape(tm // n_str, n_str, d), 0, 1)
        h_ref[...] = _rmsnorm_rows(x.reshape(tm, d), g_ref[...]).astype(bf16)
        o_ref[...] = x

    u = jnp.dot(h_ref[...], wu_ref[...], preferred_element_type=f32)
    u = jnp.square(jnp.maximum(u, 0.0)).astype(bf16)
    o_ref[...] += jnp.dot(u, wd_ref[...], preferred_element_type=f32).reshape(o_ref.shape)


def _ffn(x, gain, w_up, w_down, layer, *, streams=None, relayout=None):
    n, d = x.shape
    ff = w_up.shape[2]
    tf = _pick(ff, (1024, 512, 256, 128))
    plain = pl.BlockSpec((_pick(n, (512, 256, 128, 64)), d), lambda i, j: (i, 0))
    in_spec = out_spec = plain
    out_shape = (n, d)
    tm = plain.block_shape[0]
    if relayout is not None:
        n_str, n_steps = streams
        tj = _pick(n_steps, (64, 32, 16, 8))
        tm = n_str * tj
        n_run = n_steps // tj
        nat_shape = (n // n_steps, n_steps, d)
        nat_spec = pl.BlockSpec((n_str, tj, d), lambda i, j: (i // n_run, i % n_run, 0))
        str_spec = pl.BlockSpec((tm, d), lambda i, j: (i, 0))
        if relayout == "to_streams":
            x, in_spec, out_spec = x.reshape(nat_shape), nat_spec, str_spec
        else:
            in_spec, out_spec, out_shape = str_spec, nat_spec, nat_shape
    out = pl.pallas_call(
        functools.partial(_ffn_body, relayout=relayout),
        grid=(n // tm, ff // tf),
        in_specs=[
            in_spec,
            pl.BlockSpec((1, d), lambda i, j: (0, 0)),
            pl.BlockSpec((None, d, tf), lambda i, j: (layer, 0, j)),
            pl.BlockSpec((None, tf, d), lambda i, j: (layer, j, 0)),
        ],
        out_specs=out_spec,
        out_shape=jax.ShapeDtypeStruct(out_shape, f32),
        scratch_shapes=[pltpu.VMEM((tm, d), bf16)],
        compiler_params=_cparams(2),
        name="ffn",
    )(x, gain.reshape(1, d).astype(f32), w_up, w_down)
    return out.reshape(n, d)


def _norm_body(x_ref, g_ref, o_ref):
    o_ref[...] = _rmsnorm_rows(x_ref[...], g_ref[...])


def _norm(x, gain):
    n, d = x.shape
    tm = _pick(n, (512, 256, 128, 64))
    return pl.pallas_call(
        _norm_body,
        grid=(n // tm,),
        in_specs=[pl.BlockSpec((tm, d), lambda i: (i, 0)), pl.BlockSpec((1, d), lambda i: (0, 0))],
        out_specs=pl.BlockSpec((tm, d), lambda i: (i, 0)),
        out_shape=jax.ShapeDtypeStruct((n, d), f32),
        compiler_params=_cparams(1),
        name="rmsnorm",
    )(x, gain.reshape(1, d).astype(f32))


def _rope_tables(pos, width, rot):
    half = rot // 2
    inv = ROPE_THETA ** (-jnp.arange(half, dtype=f32) / half)
    ang = pos.astype(f32)[:, None] * inv[None, :]
    cos, sin = jnp.cos(ang), jnp.sin(ang)
    n = pos.shape[0]
    pad = jnp.zeros((n, width - rot), f32)
    zero = jnp.zeros((n, half), f32)
    c = jnp.concatenate([cos, cos, pad + 1.0], axis=1)
    s_up = jnp.concatenate([-sin, zero, pad], axis=1)
    s_dn = jnp.concatenate([zero, sin, pad], axis=1)
    reps = LANES // width
    return tuple(jnp.tile(t, (1, reps)) for t in (c, s_up, s_dn))


def _rope(x, c, s_up, s_dn, half):
    return x * c + pltpu.roll(x, LANES - half, axis=1) * s_up + pltpu.roll(x, half, axis=1) * s_dn


def _attn_post_body(q_ref, k_ref, qi_ref, kw_ref, c_ref, su_ref, sd_ref, ci_ref, sui_ref, sdi_ref,
                    gq_ref, gk_ref, gki_ref, qo_ref, ko_ref, qio_ref, kio_ref):
    c, su, sd = c_ref[...], su_ref[...], sd_ref[...]
    ci, sui, sdi = ci_ref[...], sui_ref[...], sdi_ref[...]
    for h in range(N_HEADS):
        sl = slice(h * HEAD_DIM, (h + 1) * HEAD_DIM)
        y = _rope(_rmsnorm_rows(q_ref[:, sl], gq_ref[...]), c, su, sd, ROT_DIM // 2)
        qo_ref[:, sl] = y.astype(qo_ref.dtype)
    for h in range(N_KV_HEADS):
        sl = slice(h * HEAD_DIM, (h + 1) * HEAD_DIM)
        ko_ref[:, sl] = _rope(_rmsnorm_rows(k_ref[:, sl], gk_ref[...]), c, su, sd, ROT_DIM // 2)
    for j in range(N_IDX_HEADS // 2):
        y = _rope(qi_ref[:, j * LANES:(j + 1) * LANES], ci, sui, sdi, IDX_ROT_DIM // 2).astype(qio_ref.dtype)
        qio_ref[2 * j] = y[:, :IDX_DIM]
        qio_ref[2 * j + 1] = y[:, IDX_DIM:]
    kw = kw_ref[:, :LANES]
    lane = lax.broadcasted_iota(i32, kw.shape, 1)
    ms = jnp.sum(jnp.where(lane < IDX_DIM, kw * kw, 0.0), axis=-1, keepdims=True) * (1.0 / IDX_DIM)
    y = _rope(kw * lax.rsqrt(ms + EPS) * gki_ref[...], ci, sui, sdi, IDX_ROT_DIM // 2)
    kio_ref[...] = y[:, :IDX_DIM]


def _attn_post(proj, pos, g_q, g_k, g_ki):
    n = proj.shape[0]
    tm = _pick(n, (256, 128, 64))
    q_w, kv_w, iq_w = N_HEADS * HEAD_DIM, N_KV_HEADS * HEAD_DIM, N_IDX_HEADS * IDX_DIM
    tabs = _rope_tables(pos, HEAD_DIM, ROT_DIM) + _rope_tables(pos, IDX_DIM, IDX_ROT_DIM)
    gki = jnp.concatenate([g_ki.astype(f32), jnp.zeros((LANES - IDX_DIM,), f32)]).reshape(1, LANES)
    row = lambda i: (i, 0)
    const = lambda i: (0, 0)
    in_specs = [
        pl.BlockSpec((tm, q_w), row),
        pl.BlockSpec((tm, kv_w), lambda i: (i, q_w // kv_w)),
        pl.BlockSpec((tm, iq_w), lambda i: (i, (q_w + 2 * kv_w) // iq_w)),
        pl.BlockSpec((tm, kv_w), lambda i: (i, (q_w + 2 * kv_w + iq_w) // kv_w)),
    ] + [pl.BlockSpec((tm, LANES), row)] * 6 + [pl.BlockSpec((1, LANES), const)] * 3
    out_shape = (
        jax.ShapeDtypeStruct((n, q_w), bf16),
        jax.ShapeDtypeStruct((n, kv_w), f32),
        jax.ShapeDtypeStruct((N_IDX_HEADS, n, IDX_DIM), bf16),
        jax.ShapeDtypeStruct((n, IDX_DIM), f32),
    )
    out_specs = (
        pl.BlockSpec((tm, q_w), row),
        pl.BlockSpec((tm, kv_w), row),
        pl.BlockSpec((N_IDX_HEADS, tm, IDX_DIM), lambda i: (0, i, 0)),
        pl.BlockSpec((tm, IDX_DIM), row),
    )
    return pl.pallas_call(
        _attn_post_body,
        grid=(n // tm,),
        in_specs=in_specs,
        out_specs=out_specs,
        out_shape=out_shape,
        compiler_params=_cparams(1),
        name="attn_post",
    )(proj, proj, proj, proj, *tabs, g_q.reshape(1, -1).astype(f32), g_k.reshape(1, -1).astype(f32), gki)


def _sortable(x):
    b = pltpu.bitcast(x, i32)
    return b ^ ((b >> 31) & 0x7FFFFFFF)


def _fold_rows(x, op):
    return op(x.reshape(x.shape[0] // SUBLANES, SUBLANES, x.shape[1]), axis=0)


def _select_bias(qi_refs, wi_ref, ki_seg_refs, keys_ref, bias_ref, *, seg_lens, tq, n_sel, causal, q_tile, blk,
                 n_blk, n_chunks):
    n_seg = len(seg_lens)
    n_pack = LANES // tq
    offs = [sum(seg_lens[:s]) for s in range(n_seg)]
    wi_rows = [wi_ref[h:h + 1, :] for h in range(N_IDX_HEADS)]
    qi_rows = []
    for h in range(N_IDX_HEADS):
        for w in range(n_pack):
            qh = qi_refs[w][h]
            parts = [qh if v == w else jnp.zeros_like(qh) for v in range(n_pack)]
            qi_rows.append(jnp.concatenate(parts, axis=1) if n_pack > 1 else qh)
    qi_all = jnp.concatenate(qi_rows, axis=0)

    for s in range(n_seg):
        ki_refs = ki_seg_refs[s]
        lc = _pick(seg_lens[s], (256, 128, 64))

        def score_chunk(c, carry, ki_refs=ki_refs, lc=lc, off=offs[s]):
            r0 = pl.multiple_of(c * lc, lc)
            kic = [r[pl.ds(r0, lc), :].astype(bf16) for r in ki_refs]
            kic = jnp.concatenate(kic, axis=1) if n_pack > 1 else kic[0]
            sc = lax.dot_general(kic, qi_all, (((1,), (1,)), ((), ())), preferred_element_type=f32)
            acc = jnp.zeros((lc, LANES), f32)
            for h in range(N_IDX_HEADS):
                acc = acc + jnp.maximum(sc[:, h * LANES:(h + 1) * LANES], 0.0) * wi_rows[h]
            key = _sortable(acc)
            if causal:
                kpos = r0 + lax.broadcasted_iota(i32, (lc, LANES), 0)
                qpos = q_tile * tq + lax.broadcasted_iota(i32, (lc, LANES), 1)
                key = jnp.where((kpos // CHUNK) <= (qpos // CHUNK), key, INT_MIN)
            keys_ref[pl.ds(pl.multiple_of(off + r0, CHUNK), lc), :] = key
            return carry

        lax.fori_loop(0, n_chunks(s, lc), score_chunk, 0, unroll=2 if n_chunks(s, lc) % 2 == 0 else 1)

    def count(pred):
        def body(i, acc):
            r0 = pl.multiple_of(i * blk, blk)
            hit = jnp.where(pred(keys_ref[pl.ds(r0, blk), :], r0), 1, 0)
            return acc + _fold_rows(hit, jnp.sum)
        acc = lax.fori_loop(0, n_blk, body, jnp.zeros((SUBLANES, LANES), i32), unroll=isinstance(n_blk, int))
        return jnp.sum(acc, axis=0, keepdims=True)

    def count_ge(cand):
        return count(lambda k, r0: k >= cand)

    ans = jnp.where(count_ge(jnp.zeros((1, LANES), i32)) >= n_sel, 0, INT_MIN)

    def bit_step(i, ans):
        cand = ans | (1 << (30 - i))
        return jnp.where(count_ge(cand) >= n_sel, cand, ans)

    ans = lax.fori_loop(0, 31, bit_step, ans)
    tau = jnp.maximum(ans, INT_MIN + 1)
    n_ge = count_ge(tau)
    ties = jnp.max(n_ge) > n_sel

    @pl.when(jnp.logical_not(ties))
    def _():
        def body(i, carry):
            r0 = pl.multiple_of(i * blk, blk)
            keep = keys_ref[pl.ds(r0, blk), :] >= tau
            bias_ref[pl.ds(r0, blk), :] = jnp.where(keep, 0.0, NEG_BIG).astype(bias_ref.dtype)
            return carry
        lax.fori_loop(0, n_blk, body, 0)

    @pl.when(ties)
    def _():
        def row_idx(r0):
            return r0 + lax.broadcasted_iota(i32, (blk, LANES), 0)
        need = n_sel - count(lambda k, r0: k > tau)

        def idx_step(i, m):
            cand = m | (1 << (14 - i))
            n_below = count(lambda k, r0: jnp.logical_and(k == tau, row_idx(r0) < cand))
            return jnp.where(n_below <= need, cand, m)

        m_cut = lax.fori_loop(0, 15, idx_step, jnp.zeros((1, LANES), i32))

        def body(i, carry):
            r0 = pl.multiple_of(i * blk, blk)
            k = keys_ref[pl.ds(r0, blk), :]
            keep = jnp.logical_or(k > tau, jnp.logical_and(k == tau, row_idx(r0) < m_cut))
            bias_ref[pl.ds(r0, blk), :] = jnp.where(keep, 0.0, NEG_BIG).astype(bias_ref.dtype)
            return carry
        lax.fori_loop(0, n_blk, body, 0)


def _augmented_queries(q_heads, tq, lane0):
    qg = jnp.concatenate(q_heads, axis=0)
    wq = qg.shape[0]
    bias_lane = lane0 + lax.broadcasted_iota(i32, (wq, LANES), 0) % tq
    one_hot = (lax.broadcasted_iota(i32, (wq, LANES), 1) == bias_lane).astype(bf16)
    return jnp.concatenate([qg, one_hot], axis=1)


def _write_heads(o_ref, acc, l_acc, col0, tq):
    o_t = (acc / jnp.sum(l_acc, axis=0, keepdims=True)).T
    for j in range(GQA_GROUP):
        c = col0 + j * HEAD_DIM
        o_ref[:, c:c + HEAD_DIM] = o_t[j * tq:(j + 1) * tq, :].astype(o_ref.dtype)


def _attn_past_body(*refs, segs, tq, n_sel):
    n_seg = len(segs)
    n_pack = LANES // tq
    seg_lens = [n for n, _ in segs]
    it = iter(refs)
    qi_refs = [next(it) for _ in range(n_pack)]
    wi_ref, q_ref = next(it), next(it)
    seg_refs = [([next(it) for _ in range(n_pack)], next(it), next(it)) for _ in range(n_seg)]
    o_ref = next(it)
    keys_ref, bias_ref, sc_ref, acc_ref = (next(it) for _ in range(4))

    l_tot = sum(seg_lens)
    offs = [sum(seg_lens[:s]) for s in range(n_seg)]
    wq = GQA_GROUP * tq
    in_pack = pl.program_id(1) // N_KV_HEADS
    g_head = pl.program_id(1) % N_KV_HEADS
    blk = _pick(l_tot, (512, 320, 256, 192, 128, 64))
    n_chunks = lambda s, lc: seg_lens[s] // lc

    @pl.when(pl.program_id(1) == 0)
    def _():
        _select_bias(qi_refs, wi_ref, [r[0] for r in seg_refs], keys_ref, bias_ref, seg_lens=seg_lens, tq=tq,
                     n_sel=n_sel, causal=False, q_tile=0, blk=blk, n_blk=l_tot // blk, n_chunks=n_chunks)

    q_aug = _augmented_queries([q_ref[:, j * HEAD_DIM:(j + 1) * HEAD_DIM] for j in range(GQA_GROUP)], tq, in_pack * tq)

    def read_kv(ref, interleaved, r0, lc):
        if interleaved:
            return ref[pl.ds(r0 * N_KV_HEADS + g_head, lc, stride=N_KV_HEADS), :].astype(bf16)
        return ref[pl.ds(r0, lc), :].astype(bf16)

    m_acc = jnp.full((SUBLANES, wq), NEG_BIG, f32)
    for s in range(n_seg):
        k_ref = seg_refs[s][1]
        lc = _pick(seg_lens[s], (1024, 512, 256, 128, 64))

        def logits_chunk(c, m_acc, k_ref=k_ref, lc=lc, off=offs[s], inter=segs[s][1]):
            r0 = pl.multiple_of(c * lc, lc)
            row = pl.multiple_of(off + r0, CHUNK)
            k_aug = jnp.concatenate([read_kv(k_ref, inter, r0, lc), bias_ref[pl.ds(row, lc), :]], axis=1)
            sc = lax.dot_general(k_aug, q_aug, (((1,), (1,)), ((), ())), preferred_element_type=f32)
            sc_ref[pl.ds(row, lc), :] = sc
            return jnp.maximum(m_acc, _fold_rows(sc, jnp.max))

        m_acc = lax.fori_loop(0, n_chunks(s, lc), logits_chunk, m_acc, unroll=True)
    m_row = jnp.max(m_acc, axis=0, keepdims=True)

    acc_ref[...] = jnp.zeros(acc_ref.shape, f32)
    l_acc = jnp.zeros((SUBLANES, wq), f32)
    for s in range(n_seg):
        v_ref = seg_refs[s][2]
        lc = _pick(seg_lens[s], (1024, 512, 256, 128, 64))

        def value_chunk(c, l_acc, v_ref=v_ref, lc=lc, off=offs[s], inter=segs[s][1]):
            r0 = pl.multiple_of(c * lc, lc)
            row = pl.multiple_of(off + r0, CHUNK)
            p = jnp.exp2((sc_ref[pl.ds(row, lc), :] - m_row) * EXP_SCALE)
            vc = read_kv(v_ref, inter, r0, lc)
            acc_ref[...] += lax.dot_general(vc, p.astype(bf16), (((0,), (0,)), ((), ())), preferred_element_type=f32)
            return l_acc + _fold_rows(p, jnp.sum)

        l_acc = lax.fori_loop(0, n_chunks(s, lc), value_chunk, l_acc, unroll=True)

    _write_heads(o_ref, acc_ref[...], l_acc, 0, tq)


def _attn_past(q, qi, wi_t, segs, *, n_batch, t_seq, n_sel):
    n = q.shape[0]
    tq = t_seq
    n_pack = LANES // tq
    assert n_batch % n_pack == 0 and LANES % tq == 0
    wq = GQA_GROUP * tq
    kv_w = GQA_GROUP * HEAD_DIM
    batch_of = lambda bp, e: bp * n_pack + e // N_KV_HEADS
    head_of = lambda e: e % N_KV_HEADS
    in_specs, args = [], []
    for w in range(n_pack):
        in_specs.append(pl.BlockSpec((N_IDX_HEADS, tq, IDX_DIM), lambda bp, e, w=w: (0, bp * n_pack + w, 0)))
        args.append(qi)
    in_specs += [
        pl.BlockSpec((None, N_IDX_HEADS, LANES), lambda bp, e: (bp, 0, 0)),
        pl.BlockSpec((tq, kv_w), lambda bp, e: (batch_of(bp, e), head_of(e))),
    ]
    args += [wi_t, q]
    seg_meta = []
    for ki, k, v, n_keys, imap, interleaved in segs:
        lead = (None,) * (ki.ndim - 2)
        kv_rows = n_keys * N_KV_HEADS if interleaved else n_keys
        kv_map = lambda bp, e, imap=imap, inter=interleaved: imap(batch_of(bp, e), 0 if inter else head_of(e))
        for w in range(n_pack):
            in_specs.append(pl.BlockSpec(lead + (n_keys, IDX_DIM), lambda bp, e, imap=imap, w=w: imap(bp * n_pack + w, 0)))
            args.append(ki)
        in_specs += [pl.BlockSpec(lead + (kv_rows, HEAD_DIM), kv_map)] * 2
        args += [k, v]
        seg_meta.append((n_keys, interleaved))
    l_tot = sum(m[0] for m in seg_meta)
    body = functools.partial(_attn_past_body, segs=tuple(seg_meta), tq=tq, n_sel=n_sel)
    return pl.pallas_call(
        body,
        grid=(n_batch // n_pack, n_pack * N_KV_HEADS),
        in_specs=in_specs,
        out_specs=pl.BlockSpec((tq, kv_w), lambda bp, e: (batch_of(bp, e), head_of(e))),
        out_shape=jax.ShapeDtypeStruct((n, N_HEADS * HEAD_DIM), bf16),
        scratch_shapes=[
            pltpu.VMEM((l_tot, LANES), i32),
            pltpu.VMEM((l_tot, LANES), bf16),
            pltpu.VMEM((l_tot, wq), f32),
            pltpu.VMEM((HEAD_DIM, wq), f32),
        ],
        compiler_params=_cparams(2),
        name="attn_past",
    )(*args)


def _attn_causal_body(qi_ref, wi_ref, q_ref, ki_ref, k_ref, v_ref, o_ref, keys_ref, bias_ref, sc_ref, acc_ref,
                      m_ref, l_ref, *, t_seq, n_sel):
    tq = LANES
    q_tile = pl.program_id(1)
    lc = _pick(t_seq, (1024, 512))
    n_dyn = jnp.minimum(((q_tile + 1) * tq + lc - 1) // lc, t_seq // lc)
    for n_lc in range(1, t_seq // lc + 1):
        pl.when(n_dyn == n_lc)(functools.partial(
            _attn_causal_tile, qi_ref, wi_ref, q_ref, ki_ref, k_ref, v_ref, o_ref, keys_ref, bias_ref, sc_ref, acc_ref,
            m_ref, l_ref, t_seq=t_seq, n_sel=n_sel, q_tile=q_tile, lc=lc, n_lc=n_lc))


def _attn_causal_tile(qi_ref, wi_ref, q_ref, ki_ref, k_ref, v_ref, o_ref, keys_ref, bias_ref, sc_ref, acc_ref,
                      m_ref, l_ref, *, t_seq, n_sel, q_tile, lc, n_lc):
    tq = LANES
    wq = GQA_GROUP * tq
    blk = 512
    _select_bias([qi_ref], wi_ref, [[ki_ref]], keys_ref, bias_ref, seg_lens=[t_seq], tq=tq, n_sel=n_sel, causal=True,
                 q_tile=q_tile, blk=blk, n_blk=n_lc * (lc // blk), n_chunks=lambda s, c: n_lc * (lc // c))

    def q_aug(g):
        heads = [q_ref[:, (g * GQA_GROUP + j) * HEAD_DIM:(g * GQA_GROUP + j + 1) * HEAD_DIM] for j in range(GQA_GROUP)]
        return _augmented_queries(heads, tq, 0)

    def fold_lanes(x, op):
        parts = [x[:, j * LANES:(j + 1) * LANES] for j in range(x.shape[1] // LANES)]
        while len(parts) > 1:
            parts = [op(parts[i], parts[i + 1]) for i in range(0, len(parts) - 1, 2)] + parts[len(parts) & ~1:]
        return parts[0]

    def logits(g, qa, c):
        r0 = pl.multiple_of(c * lc, lc)
        k_aug = jnp.concatenate([k_ref[pl.ds(r0, lc), g * HEAD_DIM:(g + 1) * HEAD_DIM], bias_ref[pl.ds(r0, lc), :]], axis=1)
        sc = lax.dot_general(qa, k_aug, (((1,), (1,)), ((), ())), preferred_element_type=f32)
        sc_ref[g % 2, :, pl.ds(r0, lc)] = sc
        m_ref[(g % 2) * wq:(g % 2 + 1) * wq, :] = jnp.maximum(m_ref[(g % 2) * wq:(g % 2 + 1) * wq, :], fold_lanes(sc, jnp.maximum))

    def values(g, m_all, c):
        r0 = pl.multiple_of(c * lc, lc)
        sc = sc_ref[g % 2, :, pl.ds(r0, lc)]
        p = jnp.exp2((sc - jnp.concatenate([m_all] * (lc // LANES), axis=1)) * EXP_SCALE)
        l_ref[...] += fold_lanes(p, jnp.add)
        vc = v_ref[pl.ds(r0, lc), g * HEAD_DIM:(g + 1) * HEAD_DIM]
        acc_ref[...] += jnp.dot(p.astype(bf16), vc, preferred_element_type=f32)

    def reset_max(g):
        m_ref[(g % 2) * wq:(g % 2 + 1) * wq, :] = jnp.full((wq, LANES), NEG_BIG, f32)

    def loop(body):
        lax.fori_loop(0, n_lc, lambda c, carry: (body(c), carry)[1], 0, unroll=True)

    reset_max(0)
    qa = q_aug(0)
    loop(lambda c, qa=qa: logits(0, qa, c))
    for g in range(N_KV_HEADS):
        m_part = m_ref[(g % 2) * wq:(g % 2 + 1) * wq, :]
        m_all = jnp.broadcast_to(jnp.max(m_part, axis=1, keepdims=True), (wq, LANES))
        acc_ref[...] = jnp.zeros(acc_ref.shape, f32)
        l_ref[...] = jnp.zeros(l_ref.shape, f32)
        if g + 1 < N_KV_HEADS:
            reset_max(g + 1)
            qa = q_aug(g + 1)

            def both(c, g=g, m_all=m_all, qa=qa):
                values(g, m_all, c)
                logits(g + 1, qa, c)

            loop(both)
        else:
            loop(lambda c, g=g, m_all=m_all: values(g, m_all, c))
        o = acc_ref[...] / jnp.sum(l_ref[...], axis=1, keepdims=True)
        for j in range(GQA_GROUP):
            col = (g * GQA_GROUP + j) * HEAD_DIM
            o_ref[:, col:col + HEAD_DIM] = o[j * tq:(j + 1) * tq, :].astype(o_ref.dtype)


def _attn_causal(q, qi, wi_t, ki, k, v, *, n_batch, t_seq, n_sel):
    n = q.shape[0]
    tq = LANES
    assert t_seq % 512 == 0
    nq = t_seq // tq
    q_w, kv_w = N_HEADS * HEAD_DIM, N_KV_HEADS * HEAD_DIM
    row = lambda b, i: (b * nq + i, 0)
    whole = lambda b, i: (b, 0, 0)
    body = functools.partial(_attn_causal_body, t_seq=t_seq, n_sel=n_sel)
    return pl.pallas_call(
        body,
        grid=(n_batch, nq),
        in_specs=[
            pl.BlockSpec((N_IDX_HEADS, tq, IDX_DIM), lambda b, i: (0, b * nq + i, 0)),
            pl.BlockSpec((None, N_IDX_HEADS, LANES), lambda b, i: (b, 0, i)),
            pl.BlockSpec((tq, q_w), row),
            pl.BlockSpec((None, t_seq, IDX_DIM), whole, pipeline_mode=pl.Buffered(1)),
            pl.BlockSpec((None, t_seq, kv_w), whole, pipeline_mode=pl.Buffered(1)),
            pl.BlockSpec((None, t_seq, kv_w), whole, pipeline_mode=pl.Buffered(1)),
        ],
        out_specs=pl.BlockSpec((tq, q_w), row),
        out_shape=jax.ShapeDtypeStruct((n, q_w), bf16),
        scratch_shapes=[
            pltpu.VMEM((t_seq, LANES), i32),
            pltpu.VMEM((t_seq, LANES), bf16),
            pltpu.VMEM((2, GQA_GROUP * tq, t_seq), f32),
            pltpu.VMEM((GQA_GROUP * tq, HEAD_DIM), f32),
            pltpu.VMEM((2 * GQA_GROUP * tq, LANES), f32),
            pltpu.VMEM((GQA_GROUP * tq, LANES), f32),
        ],
        compiler_params=_cparams(2),
        name="attn_causal",
    )(qi, wi_t, q, ki, k, v)


def _attn_group(q, qi, wi, ki_new, k_new, v_new, past, *, n_batch, t_seq, offset):
    kv_w = N_KV_HEADS * HEAD_DIM
    n_pack = max(1, LANES // t_seq)
    wi_g = wi.reshape(n_batch // n_pack, n_pack, t_seq, N_IDX_HEADS).transpose(0, 3, 1, 2)
    wi_g = wi_g.reshape(n_batch // n_pack, N_IDX_HEADS, n_pack * t_seq)
    ki_g = ki_new.reshape(n_batch, t_seq, IDX_DIM)
    n_sel = min(TOPK_MAX, (offset + t_seq) // 4)

    if past is None:
        k_g = k_new.astype(bf16).reshape(n_batch, t_seq, kv_w)
        v_g = v_new.astype(bf16).reshape(n_batch, t_seq, kv_w)
        return _attn_causal(q, qi, wi_g, ki_g, k_g, v_g, n_batch=n_batch, t_seq=t_seq, n_sel=n_sel)

    assert offset % CHUNK == 0 and t_seq <= CHUNK
    k_g = k_new.reshape(n_batch, t_seq, kv_w)
    v_g = v_new.reshape(n_batch, t_seq, kv_w)
    pk, pv, pki, r = past
    n_past = pk.shape[2]
    pk = pk.reshape(pk.shape[0], n_batch, n_past * N_KV_HEADS, HEAD_DIM)
    pv = pv.reshape(pv.shape[0], n_batch, n_past * N_KV_HEADS, HEAD_DIM)
    past_map = lambda b, hb: (r, b, 0, 0)
    new_map = lambda b, hb: (b, 0, hb)
    segs = [(pki, pk, pv, n_past, past_map, True), (ki_g, k_g, v_g, t_seq, new_map, False)]
    return _attn_past(q, qi, wi_g, segs, n_batch=n_batch, t_seq=t_seq, n_sel=n_sel)


def _attn_layer(x, group, past, gain, w_in_p, g_q, g_k, g_ki, w_o):
    q_w, kv_w, iq_w = N_HEADS * HEAD_DIM, N_KV_HEADS * HEAD_DIM, N_IDX_HEADS * IDX_DIM
    proj = _matmul(x, [w_in_p], gain=gain, name="attn_in")
    pos = group.offset + jnp.tile(jnp.arange(group.t_seq), group.n_batch)
    q, k, qi, ki = _attn_post(proj, pos, g_q, g_k, g_ki)
    v = proj[:, q_w + kv_w:q_w + 2 * kv_w]
    wi = proj[:, q_w + 2 * kv_w + iq_w + IDX_DIM:q_w + 2 * kv_w + iq_w + IDX_DIM + N_IDX_HEADS]
    o = _attn_group(q, qi, wi, ki, k, v, past, n_batch=group.n_batch, t_seq=group.t_seq, offset=group.offset)
    x = _matmul(o, [w_o], residual=x, name="attn_out")
    return x, k, v, ki


def _halo_rows(x_ref, init, j, *, n_str, n_steps, chained):
    if not chained:
        return init
    prev = x_ref[(n_steps - j) * n_str:(n_steps - j + 1) * n_str, :]
    shifted = pltpu.roll(prev, 1, axis=0)
    first = lax.broadcasted_iota(i32, shifted.shape, 0) == 0
    return jnp.where(first, init, shifted)


def _chain_carries(h_end, p_end, c0):
    rows = [c0]
    for s in range(1, h_end.shape[0]):
        rows.append(p_end[s - 1:s] * rows[-1] + h_end[s - 1:s])
    return jnp.concatenate(rows, axis=0)


def _lru_core_body(xb_ref, gate_ref, cw_ref, cb_ref, wa_ref, ba_ref, wx_ref, bx_ref, lam_ref, buf_ref, h0_ref,
                   y_ref, hl_ref, a_ref, b_ref, *, n_str, n_steps, chained, lw):
    rows = n_str * n_steps
    ch = _pick(n_steps, (16, 8, 4, 2, 1)) * n_str
    n_halo = LRU_CONV - 1
    cw = [cw_ref[j:j + 1, :] for j in range(LRU_CONV)]
    cb = cb_ref[...]
    sp = jax.nn.softplus(-lam_ref[...])
    halo = jnp.concatenate(
        [_halo_rows(xb_ref, buf_ref[n_halo - j], j, n_str=n_str, n_steps=n_steps, chained=chained)
         for j in range(n_halo, 0, -1)], axis=0)

    def gates(xin, r0):
        xc = cb
        for j in range(LRU_CONV):
            xc = xc + xin[j * n_str:j * n_str + ch, :] * cw[j]
        xcb = xc.astype(bf16)
        for t in range(lw // LRU_BLOCK):
            sl = slice(t * LRU_BLOCK, (t + 1) * LRU_BLOCK)
            xt = xcb[:, sl]
            r = _sigmoid(jnp.dot(xt, wa_ref[t], preferred_element_type=f32) + ba_ref[:, sl])
            ig = _sigmoid(jnp.dot(xt, wx_ref[t], preferred_element_type=f32) + bx_ref[:, sl])
            log_a = (-LRU_C) * r * sp[:, sl]
            th = jnp.tanh(log_a)
            mult = jnp.sqrt(-2.0 * th / (1.0 - th))
            a_ref[pl.ds(r0, ch), sl] = jnp.exp(log_a)
            b_ref[pl.ds(r0, ch), sl] = mult * (ig * xc[:, sl])

    gates(jnp.concatenate([halo, xb_ref[0:ch, :]], axis=0), 0)

    def gate_chunk(c, carry):
        r0 = pl.multiple_of(c * ch, ch)
        gates(xb_ref[pl.ds(r0 - n_halo * n_str, ch + n_halo * n_str), :], r0)
        return carry

    lax.fori_loop(1, rows // ch, gate_chunk, 0)

    def scan_step(t, carry):
        r0 = pl.multiple_of(t * n_str, n_str)
        a = a_ref[pl.ds(r0, n_str), :]
        h = a * carry[0] + b_ref[pl.ds(r0, n_str), :]
        b_ref[pl.ds(r0, n_str), :] = h
        if not chained:
            return (h,)
        p = a * carry[1]
        a_ref[pl.ds(r0, n_str), :] = p
        return (h, p)

    if chained:
        init = (jnp.zeros((n_str, lw), f32), jnp.ones((n_str, lw), f32))
    else:
        init = (h0_ref[...],)
    fin = lax.fori_loop(0, n_steps, scan_step, init, unroll=8)

    if chained:
        carries = _chain_carries(fin[0], fin[1], h0_ref[...])
        c_rep = jnp.concatenate([carries] * (ch // n_str), axis=0)

    def out_chunk(c, carry):
        r0 = pl.multiple_of(c * ch, ch)
        h = b_ref[pl.ds(r0, ch), :]
        if chained:
            h = h + a_ref[pl.ds(r0, ch), :] * c_rep
            b_ref[pl.ds(r0, ch), :] = h
        y_ref[pl.ds(r0, ch), :] = (h * jax.nn.gelu(gate_ref[pl.ds(r0, ch), :])).astype(y_ref.dtype)
        return carry

    lax.fori_loop(0, rows // ch, out_chunk, 0)
    if chained:
        hl_ref[...] = b_ref[rows - 1:rows, :]
    else:
        hl_ref[...] = fin[0]


def _lru_core(proj, params, buf, h0, *, n_seq, n_str, n_steps, chained):
    w = proj.shape[1] // 2
    lw = 256
    rows = n_str * n_steps
    ns = 1 if chained else n_str
    cw, cb, wa, ba, wx, bx, lam = params
    nb = lw // LRU_BLOCK
    vec = lambda s, n: (0, n)
    in_specs = [
        pl.BlockSpec((rows, lw), lambda s, n: (s, w // lw + n)),
        pl.BlockSpec((rows, lw), lambda s, n: (s, n)),
        pl.BlockSpec((LRU_CONV, lw), vec),
        pl.BlockSpec((1, lw), vec),
        pl.BlockSpec((nb, LRU_BLOCK, LRU_BLOCK), lambda s, n: (n, 0, 0)),
        pl.BlockSpec((1, lw), vec),
        pl.BlockSpec((nb, LRU_BLOCK, LRU_BLOCK), lambda s, n: (n, 0, 0)),
        pl.BlockSpec((1, lw), vec),
        pl.BlockSpec((1, lw), vec),
        pl.BlockSpec((None, LRU_CONV - 1, ns, lw), lambda s, n: (s, 0, 0, n)),
        pl.BlockSpec((None, ns, lw), lambda s, n: (s, 0, n)),
    ]
    body = functools.partial(_lru_core_body, n_str=n_str, n_steps=n_steps, chained=chained, lw=lw)
    return pl.pallas_call(
        body,
        grid=(n_seq, w // lw),
        in_specs=in_specs,
        out_specs=(
            pl.BlockSpec((rows, lw), lambda s, n: (s, n)),
            pl.BlockSpec((None, ns, lw), lambda s, n: (s, 0, n)),
        ),
        out_shape=(
            jax.ShapeDtypeStruct((n_seq * rows, w), bf16),
            jax.ShapeDtypeStruct((n_seq, ns, w), f32),
        ),
        scratch_shapes=[pltpu.VMEM((rows, lw), f32), pltpu.VMEM((rows, lw), f32)],
        compiler_params=_cparams(2),
        name="lru_core",
    )(proj, proj, cw, cb.reshape(1, w), wa, ba.reshape(1, w), wx, bx.reshape(1, w), lam.reshape(1, w), buf, h0)


def _s5_core_body(u_ref, bm_ref, cm_ref, are_ref, aim_ref, d_ref, sre_ref, sim_ref,
                  y_ref, ore_ref, oim_ref, hs_ref, p_ref, *, n_str, n_steps, chained):
    rows = n_str * n_steps
    half = hs_ref.shape[1] // 2
    spc = _pick(n_steps, tuple(s for s in (64, 32, 16, 8, 4, 2, 1) if s * n_str <= 512))
    ch = spc * n_str

    def proj_chunk(c, carry):
        r0 = pl.multiple_of(c * ch, ch)
        hs_ref[pl.ds(r0, ch), :] = jnp.dot(u_ref[pl.ds(r0, ch), :].astype(bf16), bm_ref[...], preferred_element_type=f32)
        return carry

    lax.fori_loop(0, rows // ch, proj_chunk, 0)

    a_re = jnp.broadcast_to(are_ref[...], (n_str, half))
    a_im = jnp.broadcast_to(aim_ref[...], (n_str, half))

    def scan_step(t, carry):
        r0 = pl.multiple_of(t * n_str, n_str)
        h_re, h_im = carry[0], carry[1]
        n_re = a_re * h_re - a_im * h_im + hs_ref[pl.ds(r0, n_str), :half]
        n_im = a_re * h_im + a_im * h_re + hs_ref[pl.ds(r0, n_str), half:]
        hs_ref[pl.ds(r0, n_str), :half] = n_re
        hs_ref[pl.ds(r0, n_str), half:] = n_im
        if not chained:
            return (n_re, n_im)
        p_re, p_im = carry[2], carry[3]
        q_re = a_re[:1] * p_re - a_im[:1] * p_im
        q_im = a_re[:1] * p_im + a_im[:1] * p_re
        p_ref[pl.ds(t, 1), :half] = q_re
        p_ref[pl.ds(t, 1), half:] = q_im
        return (n_re, n_im, q_re, q_im)

    if chained:
        zero = jnp.zeros((n_str, half), f32)
        init = (zero, zero, jnp.ones((1, half), f32), jnp.zeros((1, half), f32))
    else:
        init = (sre_ref[...], sim_ref[...])
    fin = lax.fori_loop(0, n_steps, scan_step, init, unroll=2)

    if chained:
        pe_re, pe_im = fin[2], fin[3]
        c_re, c_im = [sre_ref[...]], [sim_ref[...]]
        for s in range(1, n_str):
            prev_re, prev_im = c_re[-1], c_im[-1]
            c_re.append(pe_re * prev_re - pe_im * prev_im + fin[0][s - 1:s])
            c_im.append(pe_re * prev_im + pe_im * prev_re + fin[1][s - 1:s])
        c_re = jnp.concatenate(c_re, axis=0)
        c_im = jnp.concatenate(c_im, axis=0)
        ore_ref[...] = fin[0][n_str - 1:] + pe_re * c_re[n_str - 1:] - pe_im * c_im[n_str - 1:]
        oim_ref[...] = fin[1][n_str - 1:] + pe_re * c_im[n_str - 1:] + pe_im * c_re[n_str - 1:]
        c_re = jnp.concatenate([c_re] * spc, axis=0)
        c_im = jnp.concatenate([c_im] * spc, axis=0)
    else:
        ore_ref[...] = fin[0]
        oim_ref[...] = fin[1]

    def per_stream(p):
        return jnp.broadcast_to(p[:, None, :], (spc, n_str, p.shape[1])).reshape(ch, p.shape[1])

    def out_chunk(c, carry):
        r0 = pl.multiple_of(c * ch, ch)
        u = u_ref[pl.ds(r0, ch), :]
        hs = hs_ref[pl.ds(r0, ch), :]
        if chained:
            t0 = pl.multiple_of(c * spc, spc)
            p_re = per_stream(p_ref[pl.ds(t0, spc), :half])
            p_im = per_stream(p_ref[pl.ds(t0, spc), half:])
            hs = jnp.concatenate([hs[:, :half] + p_re * c_re - p_im * c_im,
                                  hs[:, half:] + p_re * c_im + p_im * c_re], axis=1)
        y = jnp.dot(hs.astype(bf16), cm_ref[...], preferred_element_type=f32) + d_ref[...] * u
        y_ref[pl.ds(r0, ch), :] = jax.nn.gelu(y).astype(y_ref.dtype)
        return carry

    lax.fori_loop(0, rows // ch, out_chunk, 0)


def _s5_core(u, mats, s_re, s_im, *, n_seq, n_str, n_steps, chained):
    d = u.shape[1]
    bm, cm, a_re, a_im, dsk = mats
    n_gb = d // LANES
    sw = bm.shape[2] // 2
    rows = n_str * n_steps
    ns = 1 if chained else n_str
    st_spec = pl.BlockSpec((None, ns, sw), lambda s, g: (s, 0, g))
    in_specs = [
        pl.BlockSpec((rows, LANES), lambda s, g: (s, g)),
        pl.BlockSpec((None, LANES, 2 * sw), lambda s, g: (g, 0, 0)),
        pl.BlockSpec((None, 2 * sw, LANES), lambda s, g: (g, 0, 0)),
        pl.BlockSpec((1, sw), lambda s, g: (0, g)),
        pl.BlockSpec((1, sw), lambda s, g: (0, g)),
        pl.BlockSpec((1, LANES), lambda s, g: (0, g)),
        st_spec,
        st_spec,
    ]
    body = functools.partial(_s5_core_body, n_str=n_str, n_steps=n_steps, chained=chained)
    st_shape = jax.ShapeDtypeStruct((n_seq, ns, n_gb * sw), f32)
    return pl.pallas_call(
        body,
        grid=(n_seq, n_gb),
        in_specs=in_specs,
        out_specs=(pl.BlockSpec((rows, LANES), lambda s, g: (s, g)), st_spec, st_spec),
        out_shape=(jax.ShapeDtypeStruct((n_seq * rows, d), bf16), st_shape, st_shape),
        scratch_shapes=[pltpu.VMEM((rows, 2 * sw), f32), pltpu.VMEM((n_steps if chained else SUBLANES, 2 * sw), f32)],
        compiler_params=_cparams(2),
        name="s5_core",
    )(u, bm, cm, a_re, a_im, dsk, s_re, s_im)


def _s5_matrices(a_re, a_im, log_dt, b_re, b_im, c_re, c_im, d_skip):
    g, p = a_re.shape
    a = lax.complex(a_re.astype(f32), a_im.astype(f32))
    step = jnp.exp(log_dt.astype(f32))[:, None]
    a_bar = jnp.exp(a * step)
    b_bar = ((a_bar - 1.0) / a)[:, :, None] * lax.complex(b_re.astype(f32), b_im.astype(f32))
    gpb = LANES // S5_GROUP
    n_gb = g // gpb
    eye = jnp.eye(gpb, dtype=f32)

    def in_block(m):
        m = m.reshape(n_gb, gpb, p, S5_GROUP)
        return jnp.einsum('ngpc,gh->ngchp', m, eye).reshape(n_gb, gpb * S5_GROUP, gpb * p)

    def out_block(m):
        m = m.reshape(n_gb, gpb, S5_GROUP, p)
        return jnp.einsum('ngcp,gh->ngphc', m, eye).reshape(n_gb, gpb * p, gpb * S5_GROUP)

    bm = jnp.concatenate([in_block(jnp.real(b_bar)), in_block(jnp.imag(b_bar))], axis=2).astype(bf16)
    cm = jnp.concatenate([out_block(c_re.astype(f32)), -out_block(c_im.astype(f32))], axis=1).astype(bf16)
    return bm, cm, jnp.real(a_bar).reshape(1, g * p), jnp.imag(a_bar).reshape(1, g * p), d_skip.reshape(1, -1).astype(f32)


def _sconv_body(bg_ref, cg_ref, v_ref, cw_ref, buf_ref, u_ref, nb_ref, cv_ref, *, n_str, n_steps, chained):
    rows = n_str * n_steps
    ch = _pick(n_steps, (16, 8, 4, 2, 1)) * n_str
    n_halo = SCONV_W - 1
    cw = [cw_ref[j:j + 1, :] for j in range(SCONV_W)]

    def prod_chunk(c, carry):
        r0 = pl.multiple_of(c * ch, ch)
        cv_ref[pl.ds(r0, ch), :] = cg_ref[pl.ds(r0, ch), :] * v_ref[pl.ds(r0, ch), :]
        return carry

    lax.fori_loop(0, rows // ch, prod_chunk, 0)
    halo = jnp.concatenate(
        [_halo_rows(cv_ref, buf_ref[n_halo - j], j, n_str=n_str, n_steps=n_steps, chained=chained)
         for j in range(n_halo, 0, -1)], axis=0)

    def emit(xin, r0):
        z = xin[0:ch, :] * cw[0]
        for j in range(1, SCONV_W):
            z = z + xin[j * n_str:j * n_str + ch, :] * cw[j]
        u_ref[pl.ds(r0, ch), :] = (bg_ref[pl.ds(r0, ch), :] * z).astype(u_ref.dtype)

    emit(jnp.concatenate([halo, cv_ref[0:ch, :]], axis=0), 0)

    def out_chunk(c, carry):
        r0 = pl.multiple_of(c * ch, ch)
        emit(cv_ref[pl.ds(r0 - n_halo * n_str, ch + n_halo * n_str), :], r0)
        return carry

    lax.fori_loop(1, rows // ch, out_chunk, 0)
    for j in range(n_halo):
        step_rows = cv_ref[(n_steps - n_halo + j) * n_str:(n_steps - n_halo + j + 1) * n_str, :]
        nb_ref[j] = step_rows[n_str - 1:, :] if chained else step_rows


def _sconv_core(proj, conv_w, buf, *, n_seq, n_str, n_steps, chained):
    d = proj.shape[1] // 3
    lw = 256
    rows = n_str * n_steps
    ns = 1 if chained else n_str
    nl = d // lw
    body = functools.partial(_sconv_body, n_str=n_str, n_steps=n_steps, chained=chained)
    buf_spec = pl.BlockSpec((None, SCONV_W - 1, ns, lw), lambda s, n: (s, 0, 0, n))
    return pl.pallas_call(
        body,
        grid=(n_seq, nl),
        in_specs=[
            pl.BlockSpec((rows, lw), lambda s, n: (s, n)),
            pl.BlockSpec((rows, lw), lambda s, n: (s, nl + n)),
            pl.BlockSpec((rows, lw), lambda s, n: (s, 2 * nl + n)),
            pl.BlockSpec((SCONV_W, lw), lambda s, n: (0, n)),
            buf_spec,
        ],
        out_specs=(pl.BlockSpec((rows, lw), lambda s, n: (s, n)), buf_spec),
        out_shape=(
            jax.ShapeDtypeStruct((n_seq * rows, d), bf16),
            jax.ShapeDtypeStruct((n_seq, SCONV_W - 1, ns, d), f32),
        ),
        scratch_shapes=[pltpu.VMEM((rows, lw), f32)],
        compiler_params=_cparams(2),
        name="sconv_core",
    )(proj, proj, proj, conv_w, buf)


class _Group:
    def __init__(self, n_batch, t_seq, offset, has_past):
        self.n_batch, self.t_seq, self.offset, self.has_past = n_batch, t_seq, offset, has_past
        self.chained = not has_past
        if self.chained:
            self.n_seq, self.n_str, self.n_steps = n_batch, PROMPT_STREAMS, t_seq // PROMPT_STREAMS
        else:
            self.n_seq, self.n_str, self.n_steps = 1, n_batch, t_seq
        self.n_rows = n_batch * t_seq
        self.dims = dict(n_seq=self.n_seq, n_str=self.n_str, n_steps=self.n_steps, chained=self.chained)

    def to_streams(self, x):
        f = x.shape[-1]
        if self.chained:
            return x.reshape(self.n_batch, self.n_str, self.n_steps, f).transpose(0, 2, 1, 3).reshape(self.n_rows, f)
        return x.reshape(self.n_batch, self.t_seq, f).transpose(1, 0, 2).reshape(self.n_rows, f)

    def to_natural(self, x):
        f = x.shape[-1]
        if self.chained:
            return x.reshape(self.n_batch, self.n_steps, self.n_str, f).transpose(0, 2, 1, 3).reshape(self.n_rows, f)
        return x.reshape(self.t_seq, self.n_batch, f).transpose(1, 0, 2).reshape(self.n_rows, f)

    def last_steps(self, x, k, c0, c1):
        if self.chained:
            rows = [x[b * self.t_seq + (self.n_steps - k + j) * self.n_str + self.n_str - 1, c0:c1]
                    for b in range(self.n_batch) for j in range(k)]
            return jnp.stack(rows).reshape(self.n_batch, k, c1 - c0)
        r0 = (self.t_seq - k) * self.n_batch
        return x[r0:, c0:c1].reshape(k, self.n_batch, c1 - c0).transpose(1, 0, 2)

    def state_in(self, s):
        if s.ndim == 2:
            return s.astype(f32)[None]
        return s.astype(f32).transpose(1, 0, 2)[None]

    def zeros_state(self, k, width):
        shape = (self.n_seq, 1, width) if k is None else (self.n_seq, k, 1, width)
        return jnp.zeros(shape, f32)

    def state_out(self, s):
        return s.reshape(self.n_batch, -1)


def kernel(x_prompt, x_sample, cache_attn_k, cache_attn_v, cache_idx_k, state_lru_conv, state_lru_h, state_s5_re, state_s5_im, state_sconv, norm_mix, norm_ffn, attn_w_in, attn_g_q, attn_g_k, attn_g_ki, attn_w_o, lru_w_in, lru_conv_w, lru_conv_b, lru_w_a, lru_b_a, lru_w_x, lru_b_x, lru_lambda, lru_w_out, s5_a_re, s5_a_im, s5_log_dt, s5_b_re, s5_b_im, s5_c_re, s5_c_im, s5_d, s5_w_glu, sc_w_in, sc_conv_w, sc_w_out, ffn_w_up, ffn_w_down):
    bp, tp, d = x_prompt.shape
    bs, ts, _ = x_sample.shape
    n_past = cache_attn_k.shape[2]
    depth = norm_mix.shape[0]
    n_mix = 4
    groups = [_Group(bp, tp, 0, False), _Group(bs, ts, n_past, True)]
    xs = [x_prompt.reshape(bp * tp, d).astype(f32), x_sample.reshape(bs * ts, d).astype(f32)]
    layouts = ["natural", "natural"]
    outs = [[[] for _ in range(8)] for _ in groups]
    w_up_bf, w_down_bf = ffn_w_up.astype(bf16), ffn_w_down.astype(bf16)

    def wanted(i):
        return "natural" if i >= depth or i % n_mix == 0 else "streams"

    for i in range(depth):
        m, r = i % n_mix, i // n_mix
        for gi, g in enumerate(groups):
            if layouts[gi] != wanted(i):
                xs[gi] = g.to_natural(xs[gi]) if wanted(i) == "natural" else g.to_streams(xs[gi])
                layouts[gi] = wanted(i)
        if m == 0:
            n_in = attn_w_in.shape[2]
            n_pad = -(-n_in // 768) * 768
            w_in_p = jnp.pad(attn_w_in[r], ((0, 0), (0, n_pad - n_in))).astype(bf16)
            w_o = attn_w_o[r].astype(bf16)
            pasts = [None, (cache_attn_k, cache_attn_v, cache_idx_k, r)]
            for gi, g in enumerate(groups):
                xs[gi], k, v, ki = _attn_layer(xs[gi], g, pasts[gi], norm_mix[i], w_in_p, attn_g_q[r], attn_g_k[r],
                                               attn_g_ki[r], w_o)
                outs[gi][0].append(k.reshape(g.n_batch, g.t_seq, N_KV_HEADS, HEAD_DIM))
                outs[gi][1].append(v.reshape(g.n_batch, g.t_seq, N_KV_HEADS, HEAD_DIM))
                outs[gi][2].append(ki.reshape(g.n_batch, g.t_seq, IDX_DIM))
        elif m == 1:
            w = lru_w_in.shape[2] // 2
            w_in, w_out = lru_w_in[r].astype(bf16), lru_w_out[r].astype(bf16)
            params = (lru_conv_w[r], lru_conv_b[r], lru_w_a[r].astype(bf16), lru_b_a[r], lru_w_x[r].astype(bf16),
                      lru_b_x[r], lru_lambda[r])
            for gi, g in enumerate(groups):
                proj = _matmul(xs[gi], [w_in], gain=norm_mix[i], name="lru_in")
                if g.has_past:
                    buf, h0 = g.state_in(state_lru_conv[r]), g.state_in(state_lru_h[r])
                else:
                    buf, h0 = g.zeros_state(LRU_CONV - 1, w), g.zeros_state(None, w)
                y, hl = _lru_core(proj, params, buf, h0, **g.dims)
                outs[gi][3].append(g.last_steps(proj, LRU_CONV - 1, w, 2 * w))
                outs[gi][4].append(g.state_out(hl))
                xs[gi] = _matmul(y, [w_out], residual=xs[gi], name="lru_out")
        elif m == 2:
            mats = _s5_matrices(s5_a_re[r], s5_a_im[r], s5_log_dt[r], s5_b_re[r], s5_b_im[r], s5_c_re[r],
                                s5_c_im[r], s5_d[r])
            n_grp, n_state = s5_a_re.shape[1], s5_a_re.shape[2]
            gp = n_grp * n_state
            wg = s5_w_glu[r].astype(bf16)
            wg = [wg[:, :d], wg[:, d:]]
            for gi, g in enumerate(groups):
                u = _norm(xs[gi], norm_mix[i])
                if g.has_past:
                    s_re = g.state_in(state_s5_re[r].reshape(g.n_batch, gp))
                    s_im = g.state_in(state_s5_im[r].reshape(g.n_batch, gp))
                else:
                    s_re, s_im = g.zeros_state(None, gp), g.zeros_state(None, gp)
                y, o_re, o_im = _s5_core(u, mats, s_re, s_im, **g.dims)
                outs[gi][5].append(g.state_out(o_re).reshape(g.n_batch, n_grp, n_state))
                outs[gi][6].append(g.state_out(o_im).reshape(g.n_batch, n_grp, n_state))
                xs[gi] = _matmul(y, wg, residual=xs[gi], glu=True, name="s5_glu")
        else:
            w_in, w_out = sc_w_in[r].astype(bf16), sc_w_out[r].astype(bf16)
            for gi, g in enumerate(groups):
                proj = _matmul(xs[gi], [w_in], gain=norm_mix[i], name="sconv_in")
                buf = g.state_in(state_sconv[r]) if g.has_past else g.zeros_state(SCONV_W - 1, d)
                u, nb = _sconv_core(proj, sc_conv_w[r], buf, **g.dims)
                outs[gi][7].append(nb.transpose(0, 2, 1, 3).reshape(g.n_batch, SCONV_W - 1, d))
                xs[gi] = _matmul(u, [w_out], residual=xs[gi], name="sconv_out")
        for gi, g in enumerate(groups):
            relayout = None
            if g.chained and layouts[gi] != wanted(i + 1):
                relayout = "to_" + wanted(i + 1)
                layouts[gi] = wanted(i + 1)
            xs[gi] = _ffn(xs[gi], norm_ffn[i], w_up_bf, w_down_bf, i, streams=(g.n_str, g.n_steps), relayout=relayout)

    for gi, g in enumerate(groups):
        if layouts[gi] != "natural":
            xs[gi] = g.to_natural(xs[gi])
    y_prompt = xs[0].reshape(bp, tp, d)
    y_sample = xs[1].reshape(bs, ts, d)
    st = [[jnp.stack(l) for l in outs[gi]] for gi in range(2)]
    return (y_prompt, y_sample, *st[0], *st[1])
```

```python
import functools
import math

import jax
import jax.numpy as jnp
from jax import lax
from jax.experimental import pallas as pl
from jax.experimental.pallas import tpu as pltpu

f32 = jnp.float32
bf16 = jnp.bfloat16
i32 = jnp.int32

LANES = 128
SUBLANES = 8
VMEM_LIMIT = 56 * 1024 * 1024

CHUNK = 64
EPS = 1e-6
N_HEADS = 16
N_KV_HEADS = 4
HEAD_DIM = 128
GQA_GROUP = N_HEADS // N_KV_HEADS
ROT_DIM = HEAD_DIM // 4
ROPE_THETA = 500000.0
ATTN_SCALE = HEAD_DIM ** -0.5
N_IDX_HEADS = 16
IDX_DIM = 64
IDX_ROT_DIM = IDX_DIM // 4
TOPK_MAX = 256
LRU_BLOCK = 128
LRU_CONV = 4
LRU_C = 8.0
S5_GROUP = 16
S5_STATE = 64
SCONV_W = 3
PROMPT_STREAMS = 8

NEG_BIG = -1e30
INT_MIN = -(2 ** 31)
LOG2E = math.log2(math.e)
EXP_SCALE = ATTN_SCALE * LOG2E


def _cparams(n_axes):
    return pltpu.CompilerParams(dimension_semantics=("arbitrary",) * n_axes, vmem_limit_bytes=VMEM_LIMIT)


def _pick(n, cands):
    for c in cands:
        if n % c == 0:
            return c
    raise ValueError(f"no tile in {cands} divides {n}")


def _sigmoid(x):
    return 0.5 + 0.5 * jnp.tanh(0.5 * x)


def _rmsnorm_rows(x, g):
    ms = jnp.mean(x * x, axis=-1, keepdims=True)
    return x * lax.rsqrt(ms + EPS) * g


def _mm_body(*refs, norm, n_w, has_res, glu):
    it = iter(refs)
    x_ref = next(it)
    g_ref = next(it) if norm else None
    w_refs = [next(it) for _ in range(n_w)]
    r_ref = next(it) if has_res else None
    o_ref = next(it)
    h_ref = next(it) if norm else None

    if norm:
        @pl.when(pl.program_id(1) == 0)
        def _():
            h_ref[...] = _rmsnorm_rows(x_ref[...], g_ref[...]).astype(bf16)
        h = h_ref[...]
    else:
        h = x_ref[...]
    acc = jnp.dot(h, w_refs[0][...], preferred_element_type=f32)
    if glu:
        acc = acc * jax.nn.sigmoid(jnp.dot(h, w_refs[1][...], preferred_element_type=f32))
    if has_res:
        acc = acc + r_ref[...]
    o_ref[...] = acc.astype(o_ref.dtype)


def _matmul(x, ws, *, gain=None, residual=None, out_dtype=f32, glu=False, name="mm"):
    n, k = x.shape
    m = ws[0].shape[1]
    norm = gain is not None
    tm = _pick(n, (1024, 512, 256, 128, 64))
    tn = _pick(m, (1024, 768, 512, 256, 128))
    in_specs = [pl.BlockSpec((tm, k), lambda i, j: (i, 0))]
    args = [x]
    if norm:
        in_specs.append(pl.BlockSpec((1, k), lambda i, j: (0, 0)))
        args.append(gain.reshape(1, k).astype(f32))
    for w in ws:
        in_specs.append(pl.BlockSpec((k, tn), lambda i, j: (0, j)))
        args.append(w)
    if residual is not None:
        in_specs.append(pl.BlockSpec((tm, tn), lambda i, j: (i, j)))
        args.append(residual)
    body = functools.partial(_mm_body, norm=norm, n_w=len(ws), has_res=residual is not None, glu=glu)
    return pl.pallas_call(
        body,
        grid=(n // tm, m // tn),
        in_specs=in_specs,
        out_specs=pl.BlockSpec((tm, tn), lambda i, j: (i, j)),
        out_shape=jax.ShapeDtypeStruct((n, m), out_dtype),
        scratch_shapes=[pltpu.VMEM((tm, k), bf16)] if norm else [],
        compiler_params=_cparams(2),
        name=name,
    )(*args)


def _ffn_body(x_ref, g_ref, wu_ref, wd_ref, o_ref, h_ref, *, relayout):
    tm, d = h_ref.shape

    @pl.when(pl.program_id(1) == 0)
    def _():
        x = x_ref[...]
        if relayout == "to_streams":
            x = jnp.swapaxes(x, 0, 1).reshape(tm, d)
        elif relayout == "to_natural":
            n_str = o_ref.shape[0]
            x = jnp.swapaxes(x.reshape(tm // n_str, n_str, d), 0, 1)
        h_ref[...] = _rmsnorm_rows(x.reshape(tm, d), g_ref[...]).astype(bf16)
        o_ref[...] = x

    u = jnp.dot(h_ref[...], wu_ref[...], preferred_element_type=f32)
    u = jnp.square(jnp.maximum(u, 0.0)).astype(bf16)
    o_ref[...] += jnp.dot(u, wd_ref[...], preferred_element_type=f32).reshape(o_ref.shape)


def _ffn(x, gain, w_up, w_down, layer, *, streams=None, relayout=None):
    n, d = x.shape
    ff = w_up.shape[2]
    tf = _pick(ff, (1024, 512, 256, 128))
    plain = pl.BlockSpec((_pick(n, (512, 256, 128, 64)), d), lambda i, j: (i, 0))
    in_spec = out_spec = plain
    out_shape = (n, d)
    tm = plain.block_shape[0]
    if relayout is not None:
        n_str, n_steps = streams
        tj = _pick(n_steps, (64, 32, 16, 8))
        tm = n_str * tj
        n_run = n_steps // tj
        nat_shape = (n // n_steps, n_steps, d)
        nat_spec = pl.BlockSpec((n_str, tj, d), lambda i, j: (i // n_run, i % n_run, 0))
        str_spec = pl.BlockSpec((tm, d), lambda i, j: (i, 0))
        if relayout == "to_streams":
            x, in_spec, out_spec = x.reshape(nat_shape), nat_spec, str_spec
        else:
            in_spec, out_spec, out_shape = str_spec, nat_spec, nat_shape
    out = pl.pallas_call(
        functools.partial(_ffn_body, relayout=relayout),
        grid=(n // tm, ff // tf),
        in_specs=[
            in_spec,
            pl.BlockSpec((1, d), lambda i, j: (0, 0)),
            pl.BlockSpec((None, d, tf), lambda i, j: (layer, 0, j)),
            pl.BlockSpec((None, tf, d), lambda i, j: (layer, j, 0)),
        ],
        out_specs=out_spec,
        out_shape=jax.ShapeDtypeStruct(out_shape, f32),
        scratch_shapes=[pltpu.VMEM((tm, d), bf16)],
        compiler_params=_cparams(2),
        name="ffn",
    )(x, gain.reshape(1, d).astype(f32), w_up, w_down)
    return out.reshape(n, d)


def _norm_body(x_ref, g_ref, o_ref):
    o_ref[...] = _rmsnorm_rows(x_ref[...], g_ref[...])


def _norm(x, gain):
    n, d = x.shape
    tm = _pick(n, (512, 256, 128, 64))
    return pl.pallas_call(
        _norm_body,
        grid=(n // tm,),
        in_specs=[pl.BlockSpec((tm, d), lambda i: (i, 0)), pl.BlockSpec((1, d), lambda i: (0, 0))],
        out_specs=pl.BlockSpec((tm, d), lambda i: (i, 0)),
        out_shape=jax.ShapeDtypeStruct((n, d), f32),
        compiler_params=_cparams(1),
        name="rmsnorm",
    )(x, gain.reshape(1, d).astype(f32))


def _rope_tables(pos, width, rot):
    half = rot // 2
    inv = ROPE_THETA ** (-jnp.arange(half, dtype=f32) / half)
    ang = pos.astype(f32)[:, None] * inv[None, :]
    cos, sin = jnp.cos(ang), jnp.sin(ang)
    n = pos.shape[0]
    pad = jnp.zeros((n, width - rot), f32)
    zero = jnp.zeros((n, half), f32)
    c = jnp.concatenate([cos, cos, pad + 1.0], axis=1)
    s_up = jnp.concatenate([-sin, zero, pad], axis=1)
    s_dn = jnp.concatenate([zero, sin, pad], axis=1)
    reps = LANES // width
    return tuple(jnp.tile(t, (1, reps)) for t in (c, s_up, s_dn))


def _rope(x, c, s_up, s_dn, half):
    return x * c + pltpu.roll(x, LANES - half, axis=1) * s_up + pltpu.roll(x, half, axis=1) * s_dn


def _attn_post_body(q_ref, k_ref, qi_ref, kw_ref, c_ref, su_ref, sd_ref, ci_ref, sui_ref, sdi_ref,
                    gq_ref, gk_ref, gki_ref, qo_ref, ko_ref, qio_ref, kio_ref):
    c, su, sd = c_ref[...], su_ref[...], sd_ref[...]
    ci, sui, sdi = ci_ref[...], sui_ref[...], sdi_ref[...]
    for h in range(N_HEADS):
        sl = slice(h * HEAD_DIM, (h + 1) * HEAD_DIM)
        y = _rope(_rmsnorm_rows(q_ref[:, sl], gq_ref[...]), c, su, sd, ROT_DIM // 2)
        qo_ref[:, sl] = y.astype(qo_ref.dtype)
    for h in range(N_KV_HEADS):
        sl = slice(h * HEAD_DIM, (h + 1) * HEAD_DIM)
        ko_ref[:, sl] = _rope(_rmsnorm_rows(k_ref[:, sl], gk_ref[...]), c, su, sd, ROT_DIM // 2)
    for j in range(N_IDX_HEADS // 2):
        y = _rope(qi_ref[:, j * LANES:(j + 1) * LANES], ci, sui, sdi, IDX_ROT_DIM // 2).astype(qio_ref.dtype)
        qio_ref[2 * j] = y[:, :IDX_DIM]
        qio_ref[2 * j + 1] = y[:, IDX_DIM:]
    kw = kw_ref[:, :LANES]
    lane = lax.broadcasted_iota(i32, kw.shape, 1)
    ms = jnp.sum(jnp.where(lane < IDX_DIM, kw * kw, 0.0), axis=-1, keepdims=True) * (1.0 / IDX_DIM)
    y = _rope(kw * lax.rsqrt(ms + EPS) * gki_ref[...], ci, sui, sdi, IDX_ROT_DIM // 2)
    kio_ref[...] = y[:, :IDX_DIM]


def _attn_post(proj, pos, g_q, g_k, g_ki):
    n = proj.shape[0]
    tm = _pick(n, (256, 128, 64))
    q_w, kv_w, iq_w = N_HEADS * HEAD_DIM, N_KV_HEADS * HEAD_DIM, N_IDX_HEADS * IDX_DIM
    tabs = _rope_tables(pos, HEAD_DIM, ROT_DIM) + _rope_tables(pos, IDX_DIM, IDX_ROT_DIM)
    gki = jnp.concatenate([g_ki.astype(f32), jnp.zeros((LANES - IDX_DIM,), f32)]).reshape(1, LANES)
    row = lambda i: (i, 0)
    const = lambda i: (0, 0)
    in_specs = [
        pl.BlockSpec((tm, q_w), row),
        pl.BlockSpec((tm, kv_w), lambda i: (i, q_w // kv_w)),
        pl.BlockSpec((tm, iq_w), lambda i: (i, (q_w + 2 * kv_w) // iq_w)),
        pl.BlockSpec((tm, kv_w), lambda i: (i, (q_w + 2 * kv_w + iq_w) // kv_w)),
    ] + [pl.BlockSpec((tm, LANES), row)] * 6 + [pl.BlockSpec((1, LANES), const)] * 3
    out_shape = (
        jax.ShapeDtypeStruct((n, q_w), bf16),
        jax.ShapeDtypeStruct((n, kv_w), f32),
        jax.ShapeDtypeStruct((N_IDX_HEADS, n, IDX_DIM), bf16),
        jax.ShapeDtypeStruct((n, IDX_DIM), f32),
    )
    out_specs = (
        pl.BlockSpec((tm, q_w), row),
        pl.BlockSpec((tm, kv_w), row),
        pl.BlockSpec((N_IDX_HEADS, tm, IDX_DIM), lambda i: (0, i, 0)),
        pl.BlockSpec((tm, IDX_DIM), row),
    )
    return pl.pallas_call(
        _attn_post_body,
        grid=(n // tm,),
        in_specs=in_specs,
        out_specs=out_specs,
        out_shape=out_shape,
        compiler_params=_cparams(1),
        name="attn_post",
    )(proj, proj, proj, proj, *tabs, g_q.reshape(1, -1).astype(f32), g_k.reshape(1, -1).astype(f32), gki)


def _sortable(x):
    b = pltpu.bitcast(x, i32)
    return b ^ ((b >> 31) & 0x7FFFFFFF)


def _fold_rows(x, op):
    return op(x.reshape(x.shape[0] // SUBLANES, SUBLANES, x.shape[1]), axis=0)


def _select_bias(qi_refs, wi_ref, ki_seg_refs, keys_ref, bias_ref, *, seg_lens, tq, n_sel, causal, q_tile, blk,
                 n_blk, n_chunks, score_rows=256):
    n_seg = len(seg_lens)
    n_pack = LANES // tq
    offs = [sum(seg_lens[:s]) for s in range(n_seg)]
    wi_rows = [wi_ref[h:h + 1, :] for h in range(N_IDX_HEADS)]
    qi_rows = []
    for h in range(N_IDX_HEADS):
        for w in range(n_pack):
            qh = qi_refs[w][h]
            parts = [qh if v == w else jnp.zeros_like(qh) for v in range(n_pack)]
            qi_rows.append(jnp.concatenate(parts, axis=1) if n_pack > 1 else qh)
    qi_all = jnp.concatenate(qi_rows, axis=0)

    for s in range(n_seg):
        ki_refs = ki_seg_refs[s]
        lc = _pick(seg_lens[s], (256, 128, 64))
        n_sub = max(1, min(score_rows, seg_lens[s]) // lc)

        def score_chunk(c, carry, ki_refs=ki_refs, lc=lc, n_sub=n_sub, off=offs[s]):
            for h2 in range(n_sub):
                r0 = pl.multiple_of((c * n_sub + h2) * lc, lc)
                kic = [r[pl.ds(r0, lc), :].astype(bf16) for r in ki_refs]
                kic = jnp.concatenate(kic, axis=1) if n_pack > 1 else kic[0]
                sc = lax.dot_general(kic, qi_all, (((1,), (1,)), ((), ())), preferred_element_type=f32)
                acc = jnp.zeros((lc, LANES), f32)
                for h in range(N_IDX_HEADS):
                    acc = acc + jnp.maximum(sc[:, h * LANES:(h + 1) * LANES], 0.0) * wi_rows[h]
                key = _sortable(acc)
                if causal:
                    kpos = r0 + lax.broadcasted_iota(i32, (lc, LANES), 0)
                    qpos = q_tile * tq + lax.broadcasted_iota(i32, (lc, LANES), 1)
                    key = jnp.where((kpos // CHUNK) <= (qpos // CHUNK), key, INT_MIN)
                keys_ref[pl.ds(pl.multiple_of(off + r0, CHUNK), lc), :] = key
            return carry

        n_trip = n_chunks(s, lc * n_sub)
        lax.fori_loop(0, n_trip, score_chunk, 0, unroll=2 if isinstance(n_trip, int) and n_trip % 2 == 0 else 1)

    def count(pred):
        def body(i, acc):
            r0 = pl.multiple_of(i * blk, blk)
            hit = jnp.where(pred(keys_ref[pl.ds(r0, blk), :], r0), 1, 0)
            return acc + _fold_rows(hit, jnp.sum)
        acc = lax.fori_loop(0, n_blk, body, jnp.zeros((SUBLANES, LANES), i32), unroll=isinstance(n_blk, int))
        return jnp.sum(acc, axis=0, keepdims=True)

    def count_ge(cand):
        return count(lambda k, r0: k >= cand)

    ans = jnp.where(count_ge(jnp.zeros((1, LANES), i32)) >= n_sel, 0, INT_MIN)

    def bit_step(i, ans):
        cand = ans | (1 << (30 - i))
        return jnp.where(count_ge(cand) >= n_sel, cand, ans)

    ans = lax.fori_loop(0, 31, bit_step, ans)
    tau = jnp.maximum(ans, INT_MIN + 1)
    n_ge = count_ge(tau)
    ties = jnp.max(n_ge) > n_sel

    @pl.when(jnp.logical_not(ties))
    def _():
        def body(i, carry):
            r0 = pl.multiple_of(i * blk, blk)
            keep = keys_ref[pl.ds(r0, blk), :] >= tau
            bias_ref[pl.ds(r0, blk), :] = jnp.where(keep, 0.0, NEG_BIG).astype(bias_ref.dtype)
            return carry
        lax.fori_loop(0, n_blk, body, 0)

    @pl.when(ties)
    def _():
        def row_idx(r0):
            return r0 + lax.broadcasted_iota(i32, (blk, LANES), 0)
        need = n_sel - count(lambda k, r0: k > tau)

        def idx_step(i, m):
            cand = m | (1 << (14 - i))
            n_below = count(lambda k, r0: jnp.logical_and(k == tau, row_idx(r0) < cand))
            return jnp.where(n_below <= need, cand, m)

        m_cut = lax.fori_loop(0, 15, idx_step, jnp.zeros((1, LANES), i32))

        def body(i, carry):
            r0 = pl.multiple_of(i * blk, blk)
            k = keys_ref[pl.ds(r0, blk), :]
            keep = jnp.logical_or(k > tau, jnp.logical_and(k == tau, row_idx(r0) < m_cut))
            bias_ref[pl.ds(r0, blk), :] = jnp.where(keep, 0.0, NEG_BIG).astype(bias_ref.dtype)
            return carry
        lax.fori_loop(0, n_blk, body, 0)


def _augmented_queries(q_heads, tq, lane0):
    qg = jnp.concatenate(q_heads, axis=0)
    wq = qg.shape[0]
    bias_lane = lane0 + lax.broadcasted_iota(i32, (wq, LANES), 0) % tq
    one_hot = (lax.broadcasted_iota(i32, (wq, LANES), 1) == bias_lane).astype(bf16)
    return jnp.concatenate([qg, one_hot], axis=1)


def _write_heads(o_ref, acc, l_acc, col0, tq):
    o_t = (acc / jnp.sum(l_acc, axis=0, keepdims=True)).T
    for j in range(GQA_GROUP):
        c = col0 + j * HEAD_DIM
        o_ref[:, c:c + HEAD_DIM] = o_t[j * tq:(j + 1) * tq, :].astype(o_ref.dtype)


def _attn_past_body(*refs, segs, tq, n_sel):
    n_seg = len(segs)
    n_pack = LANES // tq
    seg_lens = [n for n, _ in segs]
    it = iter(refs)
    qi_refs = [next(it) for _ in range(n_pack)]
    wi_ref, q_ref = next(it), next(it)
    seg_refs = [([next(it) for _ in range(n_pack)], next(it), next(it)) for _ in range(n_seg)]
    o_ref = next(it)
    keys_ref, bias_ref, sc_ref, acc_ref = (next(it) for _ in range(4))

    l_tot = sum(seg_lens)
    offs = [sum(seg_lens[:s]) for s in range(n_seg)]
    wq = GQA_GROUP * tq
    in_pack = pl.program_id(1) // N_KV_HEADS
    g_head = pl.program_id(1) % N_KV_HEADS
    blk = _pick(l_tot, (512, 320, 256, 192, 128, 64))
    n_chunks = lambda s, lc: seg_lens[s] // lc

    @pl.when(pl.program_id(1) == 0)
    def _():
        _select_bias(qi_refs, wi_ref, [r[0] for r in seg_refs], keys_ref, bias_ref, seg_lens=seg_lens, tq=tq,
                     n_sel=n_sel, causal=False, q_tile=0, blk=blk, n_blk=l_tot // blk, n_chunks=n_chunks)

    q_aug = _augmented_queries([q_ref[:, j * HEAD_DIM:(j + 1) * HEAD_DIM] for j in range(GQA_GROUP)], tq, in_pack * tq)

    def read_kv(ref, interleaved, r0, lc):
        if interleaved:
            return ref[pl.ds(r0 * N_KV_HEADS + g_head, lc, stride=N_KV_HEADS), :].astype(bf16)
        return ref[pl.ds(r0, lc), :].astype(bf16)

    m_acc = jnp.full((SUBLANES, wq), NEG_BIG, f32)
    for s in range(n_seg):
        k_ref = seg_refs[s][1]
        lc = _pick(seg_lens[s], (1024, 512, 256, 128, 64))

        def logits_chunk(c, m_acc, k_ref=k_ref, lc=lc, off=offs[s], inter=segs[s][1]):
            r0 = pl.multiple_of(c * lc, lc)
            row = pl.multiple_of(off + r0, CHUNK)
            k_aug = jnp.concatenate([read_kv(k_ref, inter, r0, lc), bias_ref[pl.ds(row, lc), :]], axis=1)
            sc = lax.dot_general(k_aug, q_aug, (((1,), (1,)), ((), ())), preferred_element_type=f32)
            sc_ref[pl.ds(row, lc), :] = sc
            return jnp.maximum(m_acc, _fold_rows(sc, jnp.max))

        m_acc = lax.fori_loop(0, n_chunks(s, lc), logits_chunk, m_acc, unroll=True)
    m_row = jnp.max(m_acc, axis=0, keepdims=True)

    acc_ref[...] = jnp.zeros(acc_ref.shape, f32)
    l_acc = jnp.zeros((SUBLANES, wq), f32)
    for s in range(n_seg):
        v_ref = seg_refs[s][2]
        lc = _pick(seg_lens[s], (1024, 512, 256, 128, 64))

        def value_chunk(c, l_acc, v_ref=v_ref, lc=lc, off=offs[s], inter=segs[s][1]):
            r0 = pl.multiple_of(c * lc, lc)
            row = pl.multiple_of(off + r0, CHUNK)
            p = jnp.exp2((sc_ref[pl.ds(row, lc), :] - m_row) * EXP_SCALE)
            vc = read_kv(v_ref, inter, r0, lc)
            acc_ref[...] += lax.dot_general(vc, p.astype(bf16), (((0,), (0,)), ((), ())), preferred_element_type=f32)
            return l_acc + _fold_rows(p, jnp.sum)

        l_acc = lax.fori_loop(0, n_chunks(s, lc), value_chunk, l_acc, unroll=True)

    _write_heads(o_ref, acc_ref[...], l_acc, 0, tq)


def _attn_past(q, qi, wi_t, segs, *, n_batch, t_seq, n_sel):
    n = q.shape[0]
    tq = t_seq
    n_pack = LANES // tq
    assert n_batch % n_pack == 0 and LANES % tq == 0
    wq = GQA_GROUP * tq
    kv_w = GQA_GROUP * HEAD_DIM
    batch_of = lambda bp, e: bp * n_pack + e // N_KV_HEADS
    head_of = lambda e: e % N_KV_HEADS
    in_specs, args = [], []
    for w in range(n_pack):
        in_specs.append(pl.BlockSpec((N_IDX_HEADS, tq, IDX_DIM), lambda bp, e, w=w: (0, bp * n_pack + w, 0)))
        args.append(qi)
    in_specs += [
        pl.BlockSpec((None, N_IDX_HEADS, LANES), lambda bp, e: (bp, 0, 0)),
        pl.BlockSpec((tq, kv_w), lambda bp, e: (batch_of(bp, e), head_of(e))),
    ]
    args += [wi_t, q]
    seg_meta = []
    for ki, k, v, n_keys, imap, interleaved in segs:
        lead = (None,) * (ki.ndim - 2)
        kv_rows = n_keys * N_KV_HEADS if interleaved else n_keys
        kv_map = lambda bp, e, imap=imap, inter=interleaved: imap(batch_of(bp, e), 0 if inter else head_of(e))
        for w in range(n_pack):
            in_specs.append(pl.BlockSpec(lead + (n_keys, IDX_DIM), lambda bp, e, imap=imap, w=w: imap(bp * n_pack + w, 0)))
            args.append(ki)
        in_specs += [pl.BlockSpec(lead + (kv_rows, HEAD_DIM), kv_map)] * 2
        args += [k, v]
        seg_meta.append((n_keys, interleaved))
    l_tot = sum(m[0] for m in seg_meta)
    body = functools.partial(_attn_past_body, segs=tuple(seg_meta), tq=tq, n_sel=n_sel)
    return pl.pallas_call(
        body,
        grid=(n_batch // n_pack, n_pack * N_KV_HEADS),
        in_specs=in_specs,
        out_specs=pl.BlockSpec((tq, kv_w), lambda bp, e: (batch_of(bp, e), head_of(e))),
        out_shape=jax.ShapeDtypeStruct((n, N_HEADS * HEAD_DIM), bf16),
        scratch_shapes=[
            pltpu.VMEM((l_tot, LANES), i32),
            pltpu.VMEM((l_tot, LANES), bf16),
            pltpu.VMEM((l_tot, wq), f32),
            pltpu.VMEM((HEAD_DIM, wq), f32),
        ],
        compiler_params=_cparams(2),
        name="attn_past",
    )(*args)


def _attn_causal_body(qi_ref, wi_ref, q_ref, ki_ref, k_ref, v_ref, o_ref, keys_ref, bias_ref, sc_ref, acc_ref,
                      m_ref, l_ref, *, t_seq, n_sel):
    tq = LANES
    wq = GQA_GROUP * tq
    q_tile = pl.program_id(1)
    lc = _pick(t_seq, (1024, 512))
    n_lc = jnp.minimum(((q_tile + 1) * tq + lc - 1) // lc, t_seq // lc)
    blk = 512
    _select_bias([qi_ref], wi_ref, [[ki_ref]], keys_ref, bias_ref, seg_lens=[t_seq], tq=tq, n_sel=n_sel, causal=True,
                 q_tile=q_tile, blk=blk, n_blk=n_lc * (lc // blk), score_rows=512, n_chunks=lambda s, c: n_lc * (lc // c))

    def q_aug(g):
        heads = [q_ref[:, (g * GQA_GROUP + j) * HEAD_DIM:(g * GQA_GROUP + j + 1) * HEAD_DIM] for j in range(GQA_GROUP)]
        return _augmented_queries(heads, tq, 0)

    def fold_lanes(x, op):
        parts = [x[:, j * LANES:(j + 1) * LANES] for j in range(x.shape[1] // LANES)]
        while len(parts) > 1:
            parts = [op(parts[i], parts[i + 1]) for i in range(0, len(parts) - 1, 2)] + parts[len(parts) & ~1:]
        return parts[0]

    def logits(g, qa, c):
        r0 = pl.multiple_of(c * lc, lc)
        k_aug = jnp.concatenate([k_ref[pl.ds(r0, lc), g * HEAD_DIM:(g + 1) * HEAD_DIM], bias_ref[pl.ds(r0, lc), :]], axis=1)
        sc = lax.dot_general(qa, k_aug, (((1,), (1,)), ((), ())), preferred_element_type=f32)
        sc_ref[g % 2, :, pl.ds(r0, lc)] = sc
        m_ref[(g % 2) * wq:(g % 2 + 1) * wq, :] = jnp.maximum(m_ref[(g % 2) * wq:(g % 2 + 1) * wq, :], fold_lanes(sc, jnp.maximum))

    def values(g, m_all, c):
        r0 = pl.multiple_of(c * lc, lc)
        sc = sc_ref[g % 2, :, pl.ds(r0, lc)]
        p = jnp.exp2((sc - jnp.concatenate([m_all] * (lc // LANES), axis=1)) * EXP_SCALE)
        l_ref[...] += fold_lanes(p, jnp.add)
        vc = v_ref[pl.ds(r0, lc), g * HEAD_DIM:(g + 1) * HEAD_DIM]
        acc_ref[...] += jnp.dot(p.astype(bf16), vc, preferred_element_type=f32)

    def reset_max(g):
        m_ref[(g % 2) * wq:(g % 2 + 1) * wq, :] = jnp.full((wq, LANES), NEG_BIG, f32)

    def loop(body):
        lax.fori_loop(0, n_lc, lambda c, carry: (body(c), carry)[1], 0)

    reset_max(0)
    qa = q_aug(0)
    loop(lambda c, qa=qa: logits(0, qa, c))
    for g in range(N_KV_HEADS):
        m_part = m_ref[(g % 2) * wq:(g % 2 + 1) * wq, :]
        m_all = jnp.broadcast_to(jnp.max(m_part, axis=1, keepdims=True), (wq, LANES))
        acc_ref[...] = jnp.zeros(acc_ref.shape, f32)
        l_ref[...] = jnp.zeros(l_ref.shape, f32)
        if g + 1 < N_KV_HEADS:
            reset_max(g + 1)
            qa = q_aug(g + 1)

            def both(c, g=g, m_all=m_all, qa=qa):
                values(g, m_all, c)
                logits(g + 1, qa, c)

            loop(both)
        else:
            loop(lambda c, g=g, m_all=m_all: values(g, m_all, c))
        o = acc_ref[...] / jnp.sum(l_ref[...], axis=1, keepdims=True)
        for j in range(GQA_GROUP):
            col = (g * GQA_GROUP + j) * HEAD_DIM
            o_ref[:, col:col + HEAD_DIM] = o[j * tq:(j + 1) * tq, :].astype(o_ref.dtype)


def _attn_causal(q, qi, wi_t, ki, k, v, *, n_batch, t_seq, n_sel):
    n = q.shape[0]
    tq = LANES
    assert t_seq % 512 == 0
    nq = t_seq // tq
    q_w, kv_w = N_HEADS * HEAD_DIM, N_KV_HEADS * HEAD_DIM
    row = lambda b, i: (b * nq + i, 0)
    whole = lambda b, i: (b, 0, 0)
    body = functools.partial(_attn_causal_body, t_seq=t_seq, n_sel=n_sel)
    return pl.pallas_call(
        body,
        grid=(n_batch, nq),
        in_specs=[
            pl.BlockSpec((N_IDX_HEADS, tq, IDX_DIM), lambda b, i: (0, b * nq + i, 0)),
            pl.BlockSpec((None, N_IDX_HEADS, LANES), lambda b, i: (b, 0, i)),
            pl.BlockSpec((tq, q_w), row),
            pl.BlockSpec((None, t_seq, IDX_DIM), whole, pipeline_mode=pl.Buffered(1)),
            pl.BlockSpec((None, t_seq, kv_w), whole, pipeline_mode=pl.Buffered(1)),
            pl.BlockSpec((None, t_seq, kv_w), whole, pipeline_mode=pl.Buffered(1)),
        ],
        out_specs=pl.BlockSpec((tq, q_w), row),
        out_shape=jax.ShapeDtypeStruct((n, q_w), bf16),
        scratch_shapes=[
            pltpu.VMEM((t_seq, LANES), i32),
            pltpu.VMEM((t_seq, LANES), bf16),
            pltpu.VMEM((2, GQA_GROUP * tq, t_seq), f32),
            pltpu.VMEM((GQA_GROUP * tq, HEAD_DIM), f32),
            pltpu.VMEM((2 * GQA_GROUP * tq, LANES), f32),
            pltpu.VMEM((GQA_GROUP * tq, LANES), f32),
        ],
        compiler_params=_cparams(2),
        name="attn_causal",
    )(qi, wi_t, q, ki, k, v)


def _attn_group(q, qi, wi, ki_new, k_new, v_new, past, *, n_batch, t_seq, offset):
    kv_w = N_KV_HEADS * HEAD_DIM
    n_pack = max(1, LANES // t_seq)
    wi_g = wi.reshape(n_batch // n_pack, n_pack, t_seq, N_IDX_HEADS).transpose(0, 3, 1, 2)
    wi_g = wi_g.reshape(n_batch // n_pack, N_IDX_HEADS, n_pack * t_seq)
    ki_g = ki_new.reshape(n_batch, t_seq, IDX_DIM)
    n_sel = min(TOPK_MAX, (offset + t_seq) // 4)

    if past is None:
        k_g = k_new.astype(bf16).reshape(n_batch, t_seq, kv_w)
        v_g = v_new.astype(bf16).reshape(n_batch, t_seq, kv_w)
        return _attn_causal(q, qi, wi_g, ki_g, k_g, v_g, n_batch=n_batch, t_seq=t_seq, n_sel=n_sel)

    assert offset % CHUNK == 0 and t_seq <= CHUNK
    k_g = k_new.reshape(n_batch, t_seq, kv_w)
    v_g = v_new.reshape(n_batch, t_seq, kv_w)
    pk, pv, pki, r = past
    n_past = pk.shape[2]
    pk = pk.reshape(pk.shape[0], n_batch, n_past * N_KV_HEADS, HEAD_DIM)
    pv = pv.reshape(pv.shape[0], n_batch, n_past * N_KV_HEADS, HEAD_DIM)
    past_map = lambda b, hb: (r, b, 0, 0)
    new_map = lambda b, hb: (b, 0, hb)
    segs = [(pki, pk, pv, n_past, past_map, True), (ki_g, k_g, v_g, t_seq, new_map, False)]
    return _attn_past(q, qi, wi_g, segs, n_batch=n_batch, t_seq=t_seq, n_sel=n_sel)


def _attn_layer(x, group, past, gain, w_in_p, g_q, g_k, g_ki, w_o):
    q_w, kv_w, iq_w = N_HEADS * HEAD_DIM, N_KV_HEADS * HEAD_DIM, N_IDX_HEADS * IDX_DIM
    proj = _matmul(x, [w_in_p], gain=gain, name="attn_in")
    pos = group.offset + jnp.tile(jnp.arange(group.t_seq), group.n_batch)
    q, k, qi, ki = _attn_post(proj, pos, g_q, g_k, g_ki)
    v = proj[:, q_w + kv_w:q_w + 2 * kv_w]
    wi = proj[:, q_w + 2 * kv_w + iq_w + IDX_DIM:q_w + 2 * kv_w + iq_w + IDX_DIM + N_IDX_HEADS]
    o = _attn_group(q, qi, wi, ki, k, v, past, n_batch=group.n_batch, t_seq=group.t_seq, offset=group.offset)
    x = _matmul(o, [w_o], residual=x, name="attn_out")
    return x, k, v, ki


def _halo_rows(x_ref, init, j, *, n_str, n_steps, chained):
    if not chained:
        return init
    prev = x_ref[(n_steps - j) * n_str:(n_steps - j + 1) * n_str, :]
    shifted = pltpu.roll(prev, 1, axis=0)
    first = lax.broadcasted_iota(i32, shifted.shape, 0) == 0
    return jnp.where(first, init, shifted)


def _chain_carries(h_end, p_end, c0):
    rows = [c0]
    for s in range(1, h_end.shape[0]):
        rows.append(p_end[s - 1:s] * rows[-1] + h_end[s - 1:s])
    return jnp.concatenate(rows, axis=0)


def _lru_core_body(xb_ref, gate_ref, cw_ref, cb_ref, wa_ref, ba_ref, wx_ref, bx_ref, lam_ref, buf_ref, h0_ref,
                   y_ref, hl_ref, a_ref, b_ref, *, n_str, n_steps, chained, lw):
    rows = n_str * n_steps
    ch = _pick(n_steps, (16, 8, 4, 2, 1)) * n_str
    n_halo = LRU_CONV - 1
    cw = [cw_ref[j:j + 1, :] for j in range(LRU_CONV)]
    cb = cb_ref[...]
    sp = jax.nn.softplus(-lam_ref[...])
    halo = jnp.concatenate(
        [_halo_rows(xb_ref, buf_ref[n_halo - j], j, n_str=n_str, n_steps=n_steps, chained=chained)
         for j in range(n_halo, 0, -1)], axis=0)

    def gates(xin, r0):
        xc = cb
        for j in range(LRU_CONV):
            xc = xc + xin[j * n_str:j * n_str + ch, :] * cw[j]
        xcb = xc.astype(bf16)
        for t in range(lw // LRU_BLOCK):
            sl = slice(t * LRU_BLOCK, (t + 1) * LRU_BLOCK)
            xt = xcb[:, sl]
            r = _sigmoid(jnp.dot(xt, wa_ref[t], preferred_element_type=f32) + ba_ref[:, sl])
            ig = _sigmoid(jnp.dot(xt, wx_ref[t], preferred_element_type=f32) + bx_ref[:, sl])
            log_a = (-LRU_C) * r * sp[:, sl]
            th = jnp.tanh(log_a)
            mult = jnp.sqrt(-2.0 * th / (1.0 - th))
            a_ref[pl.ds(r0, ch), sl] = jnp.exp(log_a)
            b_ref[pl.ds(r0, ch), sl] = mult * (ig * xc[:, sl])

    gates(jnp.concatenate([halo, xb_ref[0:ch, :]], axis=0), 0)

    def gate_chunk(c, carry):
        r0 = pl.multiple_of(c * ch, ch)
        gates(xb_ref[pl.ds(r0 - n_halo * n_str, ch + n_halo * n_str), :], r0)
        return carry

    lax.fori_loop(1, rows // ch, gate_chunk, 0)

    def scan_step(t, carry):
        r0 = pl.multiple_of(t * n_str, n_str)
        a = a_ref[pl.ds(r0, n_str), :]
        h = a * carry[0] + b_ref[pl.ds(r0, n_str), :]
        b_ref[pl.ds(r0, n_str), :] = h
        if not chained:
            return (h,)
        p = a * carry[1]
        a_ref[pl.ds(r0, n_str), :] = p
        return (h, p)

    if chained:
        init = (jnp.zeros((n_str, lw), f32), jnp.ones((n_str, lw), f32))
    else:
        init = (h0_ref[...],)
    fin = lax.fori_loop(0, n_steps, scan_step, init, unroll=8)

    if chained:
        carries = _chain_carries(fin[0], fin[1], h0_ref[...])
        c_rep = jnp.concatenate([carries] * (ch // n_str), axis=0)

    def out_chunk(c, carry):
        r0 = pl.multiple_of(c * ch, ch)
        h = b_ref[pl.ds(r0, ch), :]
        if chained:
            h = h + a_ref[pl.ds(r0, ch), :] * c_rep
            b_ref[pl.ds(r0, ch), :] = h
        y_ref[pl.ds(r0, ch), :] = (h * jax.nn.gelu(gate_ref[pl.ds(r0, ch), :])).astype(y_ref.dtype)
        return carry

    lax.fori_loop(0, rows // ch, out_chunk, 0)
    if chained:
        hl_ref[...] = b_ref[rows - 1:rows, :]
    else:
        hl_ref[...] = fin[0]


def _lru_core(proj, params, buf, h0, *, n_seq, n_str, n_steps, chained):
    w = proj.shape[1] // 2
    lw = 256
    rows = n_str * n_steps
    ns = 1 if chained else n_str
    cw, cb, wa, ba, wx, bx, lam = params
    nb = lw // LRU_BLOCK
    vec = lambda s, n: (0, n)
    in_specs = [
        pl.BlockSpec((rows, lw), lambda s, n: (s, w // lw + n)),
        pl.BlockSpec((rows, lw), lambda s, n: (s, n)),
        pl.BlockSpec((LRU_CONV, lw), vec),
        pl.BlockSpec((1, lw), vec),
        pl.BlockSpec((nb, LRU_BLOCK, LRU_BLOCK), lambda s, n: (n, 0, 0)),
        pl.BlockSpec((1, lw), vec),
        pl.BlockSpec((nb, LRU_BLOCK, LRU_BLOCK), lambda s, n: (n, 0, 0)),
        pl.BlockSpec((1, lw), vec),
        pl.BlockSpec((1, lw), vec),
        pl.BlockSpec((None, LRU_CONV - 1, ns, lw), lambda s, n: (s, 0, 0, n)),
        pl.BlockSpec((None, ns, lw), lambda s, n: (s, 0, n)),
    ]
    body = functools.partial(_lru_core_body, n_str=n_str, n_steps=n_steps, chained=chained, lw=lw)
    return pl.pallas_call(
        body,
        grid=(n_seq, w // lw),
        in_specs=in_specs,
        out_specs=(
            pl.BlockSpec((rows, lw), lambda s, n: (s, n)),
            pl.BlockSpec((None, ns, lw), lambda s, n: (s, 0, n)),
        ),
        out_shape=(
            jax.ShapeDtypeStruct((n_seq * rows, w), bf16),
            jax.ShapeDtypeStruct((n_seq, ns, w), f32),
        ),
        scratch_shapes=[pltpu.VMEM((rows, lw), f32), pltpu.VMEM((rows, lw), f32)],
        compiler_params=_cparams(2),
        name="lru_core",
    )(proj, proj, cw, cb.reshape(1, w), wa, ba.reshape(1, w), wx, bx.reshape(1, w), lam.reshape(1, w), buf, h0)


def _s5_core_body(u_ref, bm_ref, cm_ref, are_ref, aim_ref, d_ref, sre_ref, sim_ref,
                  y_ref, ore_ref, oim_ref, hs_ref, p_ref, *, n_str, n_steps, chained):
    rows = n_str * n_steps
    half = hs_ref.shape[1] // 2
    spc = _pick(n_steps, tuple(s for s in (64, 32, 16, 8, 4, 2, 1) if s * n_str <= 512))
    ch = spc * n_str

    def proj_chunk(c, carry):
        r0 = pl.multiple_of(c * ch, ch)
        hs_ref[pl.ds(r0, ch), :] = jnp.dot(u_ref[pl.ds(r0, ch), :].astype(bf16), bm_ref[...], preferred_element_type=f32)
        return carry

    lax.fori_loop(0, rows // ch, proj_chunk, 0)

    a_re = jnp.broadcast_to(are_ref[...], (n_str, half))
    a_im = jnp.broadcast_to(aim_ref[...], (n_str, half))

    def scan_step(t, carry):
        r0 = pl.multiple_of(t * n_str, n_str)
        h_re, h_im = carry[0], carry[1]
        n_re = a_re * h_re - a_im * h_im + hs_ref[pl.ds(r0, n_str), :half]
        n_im = a_re * h_im + a_im * h_re + hs_ref[pl.ds(r0, n_str), half:]
        hs_ref[pl.ds(r0, n_str), :half] = n_re
        hs_ref[pl.ds(r0, n_str), half:] = n_im
        if not chained:
            return (n_re, n_im)
        p_re, p_im = carry[2], carry[3]
        q_re = a_re[:1] * p_re - a_im[:1] * p_im
        q_im = a_re[:1] * p_im + a_im[:1] * p_re
        p_ref[pl.ds(t, 1), :half] = q_re
        p_ref[pl.ds(t, 1), half:] = q_im
        return (n_re, n_im, q_re, q_im)

    if chained:
        zero = jnp.zeros((n_str, half), f32)
        init = (zero, zero, jnp.ones((1, half), f32), jnp.zeros((1, half), f32))
    else:
        init = (sre_ref[...], sim_ref[...])
    fin = lax.fori_loop(0, n_steps, scan_step, init, unroll=2)

    if chained:
        pe_re, pe_im = fin[2], fin[3]
        c_re, c_im = [sre_ref[...]], [sim_ref[...]]
        for s in range(1, n_str):
            prev_re, prev_im = c_re[-1], c_im[-1]
            c_re.append(pe_re * prev_re - pe_im * prev_im + fin[0][s - 1:s])
            c_im.append(pe_re * prev_im + pe_im * prev_re + fin[1][s - 1:s])
        c_re = jnp.concatenate(c_re, axis=0)
        c_im = jnp.concatenate(c_im, axis=0)
        ore_ref[...] = fin[0][n_str - 1:] + pe_re * c_re[n_str - 1:] - pe_im * c_im[n_str - 1:]
        oim_ref[...] = fin[1][n_str - 1:] + pe_re * c_im[n_str - 1:] + pe_im * c_re[n_str - 1:]
        c_re = jnp.concatenate([c_re] * spc, axis=0)
        c_im = jnp.concatenate([c_im] * spc, axis=0)
    else:
        ore_ref[...] = fin[0]
        oim_ref[...] = fin[1]

    def per_stream(p):
        return jnp.broadcast_to(p[:, None, :], (spc, n_str, p.shape[1])).reshape(ch, p.shape[1])

    def out_chunk(c, carry):
        r0 = pl.multiple_of(c * ch, ch)
        u = u_ref[pl.ds(r0, ch), :]
        hs = hs_ref[pl.ds(r0, ch), :]
        if chained:
            t0 = pl.multiple_of(c * spc, spc)
            p_re = per_stream(p_ref[pl.ds(t0, spc), :half])
            p_im = per_stream(p_ref[pl.ds(t0, spc), half:])
            hs = jnp.concatenate([hs[:, :half] + p_re * c_re - p_im * c_im,
                                  hs[:, half:] + p_re * c_im + p_im * c_re], axis=1)
        y = jnp.dot(hs.astype(bf16), cm_ref[...], preferred_element_type=f32) + d_ref[...] * u
        y_ref[pl.ds(r0, ch), :] = jax.nn.gelu(y).astype(y_ref.dtype)
        return carry

    lax.fori_loop(0, rows // ch, out_chunk, 0)


def _s5_core(u, mats, s_re, s_im, *, n_seq, n_str, n_steps, chained):
    d = u.shape[1]
    bm, cm, a_re, a_im, dsk = mats
    n_gb = d // LANES
    sw = bm.shape[2] // 2
    rows = n_str * n_steps
    ns = 1 if chained else n_str
    st_spec = pl.BlockSpec((None, ns, sw), lambda s, g: (s, 0, g))
    in_specs = [
        pl.BlockSpec((rows, LANES), lambda s, g: (s, g)),
        pl.BlockSpec((None, LANES, 2 * sw), lambda s, g: (g, 0, 0)),
        pl.BlockSpec((None, 2 * sw, LANES), lambda s, g: (g, 0, 0)),
        pl.BlockSpec((1, sw), lambda s, g: (0, g)),
        pl.BlockSpec((1, sw), lambda s, g: (0, g)),
        pl.BlockSpec((1, LANES), lambda s, g: (0, g)),
        st_spec,
        st_spec,
    ]
    body = functools.partial(_s5_core_body, n_str=n_str, n_steps=n_steps, chained=chained)
    st_shape = jax.ShapeDtypeStruct((n_seq, ns, n_gb * sw), f32)
    return pl.pallas_call(
        body,
        grid=(n_seq, n_gb),
        in_specs=in_specs,
        out_specs=(pl.BlockSpec((rows, LANES), lambda s, g: (s, g)), st_spec, st_spec),
        out_shape=(jax.ShapeDtypeStruct((n_seq * rows, d), bf16), st_shape, st_shape),
        scratch_shapes=[pltpu.VMEM((rows, 2 * sw), f32), pltpu.VMEM((n_steps if chained else SUBLANES, 2 * sw), f32)],
        compiler_params=_cparams(2),
        name="s5_core",
    )(u, bm, cm, a_re, a_im, dsk, s_re, s_im)


def _s5_matrices(a_re, a_im, log_dt, b_re, b_im, c_re, c_im, d_skip):
    g, p = a_re.shape
    a = lax.complex(a_re.astype(f32), a_im.astype(f32))
    step = jnp.exp(log_dt.astype(f32))[:, None]
    a_bar = jnp.exp(a * step)
    b_bar = ((a_bar - 1.0) / a)[:, :, None] * lax.complex(b_re.astype(f32), b_im.astype(f32))
    gpb = LANES // S5_GROUP
    n_gb = g // gpb
    eye = jnp.eye(gpb, dtype=f32)

    def in_block(m):
        m = m.reshape(n_gb, gpb, p, S5_GROUP)
        return jnp.einsum('ngpc,gh->ngchp', m, eye).reshape(n_gb, gpb * S5_GROUP, gpb * p)

    def out_block(m):
        m = m.reshape(n_gb, gpb, S5_GROUP, p)
        return jnp.einsum('ngcp,gh->ngphc', m, eye).reshape(n_gb, gpb * p, gpb * S5_GROUP)

    bm = jnp.concatenate([in_block(jnp.real(b_bar)), in_block(jnp.imag(b_bar))], axis=2).astype(bf16)
    cm = jnp.concatenate([out_block(c_re.astype(f32)), -out_block(c_im.astype(f32))], axis=1).astype(bf16)
    return bm, cm, jnp.real(a_bar).reshape(1, g * p), jnp.imag(a_bar).reshape(1, g * p), d_skip.reshape(1, -1).astype(f32)


def _sconv_body(bg_ref, cg_ref, v_ref, cw_ref, buf_ref, u_ref, nb_ref, cv_ref, *, n_str, n_steps, chained):
    rows = n_str * n_steps
    ch = _pick(n_steps, (16, 8, 4, 2, 1)) * n_str
    n_halo = SCONV_W - 1
    cw = [cw_ref[j:j + 1, :] for j in range(SCONV_W)]

    def prod_chunk(c, carry):
        r0 = pl.multiple_of(c * ch, ch)
        cv_ref[pl.ds(r0, ch), :] = cg_ref[pl.ds(r0, ch), :] * v_ref[pl.ds(r0, ch), :]
        return carry

    lax.fori_loop(0, rows // ch, prod_chunk, 0)
    halo = jnp.concatenate(
        [_halo_rows(cv_ref, buf_ref[n_halo - j], j, n_str=n_str, n_steps=n_steps, chained=chained)
         for j in range(n_halo, 0, -1)], axis=0)

    def emit(xin, r0):
        z = xin[0:ch, :] * cw[0]
        for j in range(1, SCONV_W):
            z = z + xin[j * n_str:j * n_str + ch, :] * cw[j]
        u_ref[pl.ds(r0, ch), :] = (bg_ref[pl.ds(r0, ch), :] * z).astype(u_ref.dtype)

    emit(jnp.concatenate([halo, cv_ref[0:ch, :]], axis=0), 0)

    def out_chunk(c, carry):
        r0 = pl.multiple_of(c * ch, ch)
        emit(cv_ref[pl.ds(r0 - n_halo * n_str, ch + n_halo * n_str), :], r0)
        return carry

    lax.fori_loop(1, rows // ch, out_chunk, 0)
    for j in range(n_halo):
        step_rows = cv_ref[(n_steps - n_halo + j) * n_str:(n_steps - n_halo + j + 1) * n_str, :]
        nb_ref[j] = step_rows[n_str - 1:, :] if chained else step_rows


def _sconv_core(proj, conv_w, buf, *, n_seq, n_str, n_steps, chained):
    d = proj.shape[1] // 3
    lw = 256
    rows = n_str * n_steps
    ns = 1 if chained else n_str
    nl = d // lw
    body = functools.partial(_sconv_body, n_str=n_str, n_steps=n_steps, chained=chained)
    buf_spec = pl.BlockSpec((None, SCONV_W - 1, ns, lw), lambda s, n: (s, 0, 0, n))
    return pl.pallas_call(
        body,
        grid=(n_seq, nl),
        in_specs=[
            pl.BlockSpec((rows, lw), lambda s, n: (s, n)),
            pl.BlockSpec((rows, lw), lambda s, n: (s, nl + n)),
            pl.BlockSpec((rows, lw), lambda s, n: (s, 2 * nl + n)),
            pl.BlockSpec((SCONV_W, lw), lambda s, n: (0, n)),
            buf_spec,
        ],
        out_specs=(pl.BlockSpec((rows, lw), lambda s, n: (s, n)), buf_spec),
        out_shape=(
            jax.ShapeDtypeStruct((n_seq * rows, d), bf16),
            jax.ShapeDtypeStruct((n_seq, SCONV_W - 1, ns, d), f32),
        ),
        scratch_shapes=[pltpu.VMEM((rows, lw), f32)],
        compiler_params=_cparams(2),
        name="sconv_core",
    )(proj, proj, proj, conv_w, buf)


class _Group:
    def __init__(self, n_batch, t_seq, offset, has_past):
        self.n_batch, self.t_seq, self.offset, self.has_past = n_batch, t_seq, offset, has_past
        self.chained = not has_past
        if self.chained:
            self.n_seq, self.n_str, self.n_steps = n_batch, PROMPT_STREAMS, t_seq // PROMPT_STREAMS
        else:
            self.n_seq, self.n_str, self.n_steps = 1, n_batch, t_seq
        self.n_rows = n_batch * t_seq
        self.dims = dict(n_seq=self.n_seq, n_str=self.n_str, n_steps=self.n_steps, chained=self.chained)

    def to_streams(self, x):
        f = x.shape[-1]
        if self.chained:
            return x.reshape(self.n_batch, self.n_str, self.n_steps, f).transpose(0, 2, 1, 3).reshape(self.n_rows, f)
        return x.reshape(self.n_batch, self.t_seq, f).transpose(1, 0, 2).reshape(self.n_rows, f)

    def to_natural(self, x):
        f = x.shape[-1]
        if self.chained:
            return x.reshape(self.n_batch, self.n_steps, self.n_str, f).transpose(0, 2, 1, 3).reshape(self.n_rows, f)
        return x.reshape(self.t_seq, self.n_batch, f).transpose(1, 0, 2).reshape(self.n_rows, f)

    def last_steps(self, x, k, c0, c1):
        if self.chained:
            rows = [x[b * self.t_seq + (self.n_steps - k + j) * self.n_str + self.n_str - 1, c0:c1]
                    for b in range(self.n_batch) for j in range(k)]
            return jnp.stack(rows).reshape(self.n_batch, k, c1 - c0)
        r0 = (self.t_seq - k) * self.n_batch
        return x[r0:, c0:c1].reshape(k, self.n_batch, c1 - c0).transpose(1, 0, 2)

    def state_in(self, s):
        if s.ndim == 2:
            return s.astype(f32)[None]
        return s.astype(f32).transpose(1, 0, 2)[None]

    def zeros_state(self, k, width):
        shape = (self.n_seq, 1, width) if k is None else (self.n_seq, k, 1, width)
        return jnp.zeros(shape, f32)

    def state_out(self, s):
        return s.reshape(self.n_batch, -1)


def kernel(x_prompt, x_sample, cache_attn_k, cache_attn_v, cache_idx_k, state_lru_conv, state_lru_h, state_s5_re, state_s5_im, state_sconv, norm_mix, norm_ffn, attn_w_in, attn_g_q, attn_g_k, attn_g_ki, attn_w_o, lru_w_in, lru_conv_w, lru_conv_b, lru_w_a, lru_b_a, lru_w_x, lru_b_x, lru_lambda, lru_w_out, s5_a_re, s5_a_im, s5_log_dt, s5_b_re, s5_b_im, s5_c_re, s5_c_im, s5_d, s5_w_glu, sc_w_in, sc_conv_w, sc_w_out, ffn_w_up, ffn_w_down):
    bp, tp, d = x_prompt.shape
    bs, ts, _ = x_sample.shape
    n_past = cache_attn_k.shape[2]
    depth = norm_mix.shape[0]
    n_mix = 4
    groups = [_Group(bp, tp, 0, False), _Group(bs, ts, n_past, True)]
    xs = [x_prompt.reshape(bp * tp, d).astype(f32), x_sample.reshape(bs * ts, d).astype(f32)]
    layouts = ["natural", "natural"]
    outs = [[[] for _ in range(8)] for _ in groups]
    w_up_bf, w_down_bf = ffn_w_up.astype(bf16), ffn_w_down.astype(bf16)

    def wanted(i):
        return "natural" if i >= depth or i % n_mix == 0 else "streams"

    for i in range(depth):
        m, r = i % n_mix, i // n_mix
        for gi, g in enumerate(groups):
            if layouts[gi] != wanted(i):
                xs[gi] = g.to_natural(xs[gi]) if wanted(i) == "natural" else g.to_streams(xs[gi])
                layouts[gi] = wanted(i)
        if m == 0:
            n_in = attn_w_in.shape[2]
            n_pad = -(-n_in // 768) * 768
            w_in_p = jnp.pad(attn_w_in[r], ((0, 0), (0, n_pad - n_in))).astype(bf16)
            w_o = attn_w_o[r].astype(bf16)
            pasts = [None, (cache_attn_k, cache_attn_v, cache_idx_k, r)]
            for gi, g in enumerate(groups):
                xs[gi], k, v, ki = _attn_layer(xs[gi], g, pasts[gi], norm_mix[i], w_in_p, attn_g_q[r], attn_g_k[r],
                                               attn_g_ki[r], w_o)
                outs[gi][0].append(k.reshape(g.n_batch, g.t_seq, N_KV_HEADS, HEAD_DIM))
                outs[gi][1].append(v.reshape(g.n_batch, g.t_seq, N_KV_HEADS, HEAD_DIM))
                outs[gi][2].append(ki.reshape(g.n_batch, g.t_seq, IDX_DIM))
        elif m == 1:
            w = lru_w_in.shape[2] // 2
            w_in, w_out = lru_w_in[r].astype(bf16), lru_w_out[r].astype(bf16)
            params = (lru_conv_w[r], lru_conv_b[r], lru_w_a[r].astype(bf16), lru_b_a[r], lru_w_x[r].astype(bf16),
                      lru_b_x[r], lru_lambda[r])
            for gi, g in enumerate(groups):
                proj = _matmul(xs[gi], [w_in], gain=norm_mix[i], name="lru_in")
                if g.has_past:
                    buf, h0 = g.state_in(state_lru_conv[r]), g.state_in(state_lru_h[r])
                else:
                    buf, h0 = g.zeros_state(LRU_CONV - 1, w), g.zeros_state(None, w)
                y, hl = _lru_core(proj, params, buf, h0, **g.dims)
                outs[gi][3].append(g.last_steps(proj, LRU_CONV - 1, w, 2 * w))
                outs[gi][4].append(g.state_out(hl))
                xs[gi] = _matmul(y, [w_out], residual=xs[gi], name="lru_out")
        elif m == 2:
            mats = _s5_matrices(s5_a_re[r], s5_a_im[r], s5_log_dt[r], s5_b_re[r], s5_b_im[r], s5_c_re[r],
                                s5_c_im[r], s5_d[r])
            n_grp, n_state = s5_a_re.shape[1], s5_a_re.shape[2]
            gp = n_grp * n_state
            wg = s5_w_glu[r].astype(bf16)
            wg = [wg[:, :d], wg[:, d:]]
            for gi, g in enumerate(groups):
                u = _norm(xs[gi], norm_mix[i])
                if g.has_past:
                    s_re = g.state_in(state_s5_re[r].reshape(g.n_batch, gp))
                    s_im = g.state_in(state_s5_im[r].reshape(g.n_batch, gp))
                else:
                    s_re, s_im = g.zeros_state(None, gp), g.zeros_state(None, gp)
                y, o_re, o_im = _s5_core(u, mats, s_re, s_im, **g.dims)
                outs[gi][5].append(g.state_out(o_re).reshape(g.n_batch, n_grp, n_state))
                outs[gi][6].append(g.state_out(o_im).reshape(g.n_batch, n_grp, n_state))
                xs[gi] = _matmul(y, wg, residual=xs[gi], glu=True, name="s5_glu")
        else:
            w_in, w_out = sc_w_in[r].astype(bf16), sc_w_out[r].astype(bf16)
            for gi, g in enumerate(groups):
                proj = _matmul(xs[gi], [w_in], gain=norm_mix[i], name="sconv_in")
                buf = g.state_in(state_sconv[r]) if g.has_past else g.zeros_state(SCONV_W - 1, d)
                u, nb = _sconv_core(proj, sc_conv_w[r], buf, **g.dims)
                outs[gi][7].append(nb.transpose(0, 2, 1, 3).reshape(g.n_batch, SCONV_W - 1, d))
                xs[gi] = _matmul(u, [w_out], residual=xs[gi], name="sconv_out")
        for gi, g in enumerate(groups):
            relayout = None
            if g.chained and layouts[gi] != wanted(i + 1):
                relayout = "to_" + wanted(i + 1)
                layouts[gi] = wanted(i + 1)
            xs[gi] = _ffn(xs[gi], norm_ffn[i], w_up_bf, w_down_bf, i, streams=(g.n_str, g.n_steps), relayout=relayout)

    for gi, g in enumerate(groups):
        if layouts[gi] != "natural":
            xs[gi] = g.to_natural(xs[gi])
    y_prompt = xs[0].reshape(bp, tp, d)
    y_sample = xs[1].reshape(bs, ts, d)
    st = [[jnp.stack(l) for l in outs[gi]] for gi in range(2)]
    return (y_prompt, y_sample, *st[0], *st[1])
```

```python
import functools
import math

import jax
import jax.numpy as jnp
from jax import lax
from jax.experimental import pallas as pl
from jax.experimental.pallas import tpu as pltpu

f32 = jnp.float32
bf16 = jnp.bfloat16
i32 = jnp.int32

LANES = 128
SUBLANES = 8
VMEM_LIMIT = 56 * 1024 * 1024

CHUNK = 64
EPS = 1e-6
N_HEADS = 16
N_KV_HEADS = 4
HEAD_DIM = 128
GQA_GROUP = N_HEADS // N_KV_HEADS
ROT_DIM = HEAD_DIM // 4
ROPE_THETA = 500000.0
ATTN_SCALE = HEAD_DIM ** -0.5
N_IDX_HEADS = 16
IDX_DIM = 64
IDX_ROT_DIM = IDX_DIM // 4
TOPK_MAX = 256
LRU_BLOCK = 128
LRU_CONV = 4
LRU_C = 8.0
S5_GROUP = 16
S5_STATE = 64
SCONV_W = 3
PROMPT_STREAMS = 8

NEG_BIG = -1e30
INT_MIN = -(2 ** 31)
LOG2E = math.log2(math.e)
EXP_SCALE = ATTN_SCALE * LOG2E


def _cparams(n_axes):
    return pltpu.CompilerParams(dimension_semantics=("arbitrary",) * n_axes, vmem_limit_bytes=VMEM_LIMIT)


def _pick(n, cands):
    for c in cands:
        if n % c == 0:
            return c
    raise ValueError(f"no tile in {cands} divides {n}")


def _sigmoid(x):
    return 0.5 + 0.5 * jnp.tanh(0.5 * x)


def _rmsnorm_rows(x, g):
    ms = jnp.mean(x * x, axis=-1, keepdims=True)
    return x * lax.rsqrt(ms + EPS) * g


def _mm_body(*refs, norm, n_w, has_res, glu):
    it = iter(refs)
    x_ref = next(it)
    g_ref = next(it) if norm else None
    w_refs = [next(it) for _ in range(n_w)]
    r_ref = next(it) if has_res else None
    o_ref = next(it)
    h_ref = next(it) if norm else None

    if norm:
        @pl.when(pl.program_id(1) == 0)
        def _():
            h_ref[...] = _rmsnorm_rows(x_ref[...], g_ref[...]).astype(bf16)
        h = h_ref[...]
    else:
        h = x_ref[...]
    acc = jnp.dot(h, w_refs[0][...], preferred_element_type=f32)
    if glu:
        acc = acc * jax.nn.sigmoid(jnp.dot(h, w_refs[1][...], preferred_element_type=f32))
    if has_res:
        acc = acc + r_ref[...]
    o_ref[...] = acc.astype(o_ref.dtype)


def _matmul(x, ws, *, gain=None, residual=None, out_dtype=f32, glu=False, name="mm"):
    n, k = x.shape
    m = ws[0].shape[1]
    norm = gain is not None
    tm = _pick(n, (1024, 512, 256, 128, 64))
    tn = _pick(m, (1024, 768, 512, 256, 128))
    in_specs = [pl.BlockSpec((tm, k), lambda i, j: (i, 0))]
    args = [x]
    if norm:
        in_specs.append(pl.BlockSpec((1, k), lambda i, j: (0, 0)))
        args.append(gain.reshape(1, k).astype(f32))
    for w in ws:
        in_specs.append(pl.BlockSpec((k, tn), lambda i, j: (0, j)))
        args.append(w)
    if residual is not None:
        in_specs.append(pl.BlockSpec((tm, tn), lambda i, j: (i, j)))
        args.append(residual)
    body = functools.partial(_mm_body, norm=norm, n_w=len(ws), has_res=residual is not None, glu=glu)
    return pl.pallas_call(
        body,
        grid=(n // tm, m // tn),
        in_specs=in_specs,
        out_specs=pl.BlockSpec((tm, tn), lambda i, j: (i, j)),
        out_shape=jax.ShapeDtypeStruct((n, m), out_dtype),
        scratch_shapes=[pltpu.VMEM((tm, k), bf16)] if norm else [],
        compiler_params=_cparams(2),
        name=name,
    )(*args)


def _ffn_body(x_ref, g_ref, wu_ref, wd_ref, o_ref, h_ref, *, relayout):
    tm, d = h_ref.shape

    @pl.when(pl.program_id(1) == 0)
    def _():
        x = x_ref[...]
        if relayout == "to_streams":
            x = jnp.swapaxes(x, 0, 1).reshape(tm, d)
        elif relayout == "to_natural":
            n_str = o_ref.shape[0]
            x = jnp.swapaxes(x.reshape(tm // n_str, n_str, d), 0, 1)
        h_ref[...] = _rmsnorm_rows(x.reshape(tm, d), g_ref[...]).astype(bf16)
        o_ref[...] = x

    u = jnp.dot(h_ref[...], wu_ref[...], preferred_element_type=f32)
    u = jnp.square(jnp.maximum(u, 0.0)).astype(bf16)
    o_ref[...] += jnp.dot(u, wd_ref[...], preferred_element_type=f32).reshape(o_ref.shape)


def _ffn(x, gain, w_up, w_down, layer, *, streams=None, relayout=None):
    n, d = x.shape
    ff = w_up.shape[2]
    tf = _pick(ff, (1024, 512, 256, 128))
    plain = pl.BlockSpec((_pick(n, (512, 256, 128, 64)), d), lambda i, j: (i, 0))
    in_spec = out_spec = plain
    out_shape = (n, d)
    tm = plain.block_shape[0]
    if relayout is not None:
        n_str, n_steps = streams
        tj = _pick(n_steps, (64, 32, 16, 8))
        tm = n_str * tj
        n_run = n_steps // tj
        nat_shape = (n // n_steps, n_steps, d)
        nat_spec = pl.BlockSpec((n_str, tj, d), lambda i, j: (i // n_run, i % n_run, 0))
        str_spec = pl.BlockSpec((tm, d), lambda i, j: (i, 0))
        if relayout == "to_streams":
            x, in_spec, out_spec = x.reshape(nat_shape), nat_spec, str_spec
        else:
            in_spec, out_spec, out_shape = str_spec, nat_spec, nat_shape
    out = pl.pallas_call(
        functools.partial(_ffn_body, relayout=relayout),
        grid=(n // tm, ff // tf),
        in_specs=[
            in_spec,
            pl.BlockSpec((1, d), lambda i, j: (0, 0)),
            pl.BlockSpec((None, d, tf), lambda i, j: (layer, 0, j)),
            pl.BlockSpec((None, tf, d), lambda i, j: (layer, j, 0)),
        ],
        out_specs=out_spec,
        out_shape=jax.ShapeDtypeStruct(out_shape, f32),
        scratch_shapes=[pltpu.VMEM((tm, d), bf16)],
        compiler_params=_cparams(2),
        name="ffn",
    )(x, gain.reshape(1, d).astype(f32), w_up, w_down)
    return out.reshape(n, d)


def _norm_body(x_ref, g_ref, o_ref):
    o_ref[...] = _rmsnorm_rows(x_ref[...], g_ref[...])


def _norm(x, gain):
    n, d = x.shape
    tm = _pick(n, (512, 256, 128, 64))
    return pl.pallas_call(
        _norm_body,
        grid=(n // tm,),
        in_specs=[pl.BlockSpec((tm, d), lambda i: (i, 0)), pl.BlockSpec((1, d), lambda i: (0, 0))],
        out_specs=pl.BlockSpec((tm, d), lambda i: (i, 0)),
        out_shape=jax.ShapeDtypeStruct((n, d), f32),
        compiler_params=_cparams(1),
        name="rmsnorm",
    )(x, gain.reshape(1, d).astype(f32))


def _rope_tables(pos, width, rot):
    half = rot // 2
    inv = ROPE_THETA ** (-jnp.arange(half, dtype=f32) / half)
    ang = pos.astype(f32)[:, None] * inv[None, :]
    cos, sin = jnp.cos(ang), jnp.sin(ang)
    n = pos.shape[0]
    pad = jnp.zeros((n, width - rot), f32)
    zero = jnp.zeros((n, half), f32)
    c = jnp.concatenate([cos, cos, pad + 1.0], axis=1)
    s_up = jnp.concatenate([-sin, zero, pad], axis=1)
    s_dn = jnp.concatenate([zero, sin, pad], axis=1)
    reps = LANES // width
    return tuple(jnp.tile(t, (1, reps)) for t in (c, s_up, s_dn))


def _rope(x, c, s_up, s_dn, half):
    return x * c + pltpu.roll(x, LANES - half, axis=1) * s_up + pltpu.roll(x, half, axis=1) * s_dn


def _attn_post_body(q_ref, k_ref, qi_ref, kw_ref, c_ref, su_ref, sd_ref, ci_ref, sui_ref, sdi_ref,
                    gq_ref, gk_ref, gki_ref, qo_ref, ko_ref, qio_ref, kio_ref):
    c, su, sd = c_ref[...], su_ref[...], sd_ref[...]
    ci, sui, sdi = ci_ref[...], sui_ref[...], sdi_ref[...]
    for h in range(N_HEADS):
        sl = slice(h * HEAD_DIM, (h + 1) * HEAD_DIM)
        y = _rope(_rmsnorm_rows(q_ref[:, sl], gq_ref[...]), c, su, sd, ROT_DIM // 2)
        qo_ref[:, sl] = y.astype(qo_ref.dtype)
    for h in range(N_KV_HEADS):
        sl = slice(h * HEAD_DIM, (h + 1) * HEAD_DIM)
        ko_ref[:, sl] = _rope(_rmsnorm_rows(k_ref[:, sl], gk_ref[...]), c, su, sd, ROT_DIM // 2)
    for j in range(N_IDX_HEADS // 2):
        y = _rope(qi_ref[:, j * LANES:(j + 1) * LANES], ci, sui, sdi, IDX_ROT_DIM // 2).astype(qio_ref.dtype)
        qio_ref[2 * j] = y[:, :IDX_DIM]
        qio_ref[2 * j + 1] = y[:, IDX_DIM:]
    kw = kw_ref[:, :LANES]
    lane = lax.broadcasted_iota(i32, kw.shape, 1)
    ms = jnp.sum(jnp.where(lane < IDX_DIM, kw * kw, 0.0), axis=-1, keepdims=True) * (1.0 / IDX_DIM)
    y = _rope(kw * lax.rsqrt(ms + EPS) * gki_ref[...], ci, sui, sdi, IDX_ROT_DIM // 2)
    kio_ref[...] = y[:, :IDX_DIM]


def _attn_post(proj, pos, g_q, g_k, g_ki):
    n = proj.shape[0]
    tm = _pick(n, (256, 128, 64))
    q_w, kv_w, iq_w = N_HEADS * HEAD_DIM, N_KV_HEADS * HEAD_DIM, N_IDX_HEADS * IDX_DIM
    tabs = _rope_tables(pos, HEAD_DIM, ROT_DIM) + _rope_tables(pos, IDX_DIM, IDX_ROT_DIM)
    gki = jnp.concatenate([g_ki.astype(f32), jnp.zeros((LANES - IDX_DIM,), f32)]).reshape(1, LANES)
    row = lambda i: (i, 0)
    const = lambda i: (0, 0)
    in_specs = [
        pl.BlockSpec((tm, q_w), row),
        pl.BlockSpec((tm, kv_w), lambda i: (i, q_w // kv_w)),
        pl.BlockSpec((tm, iq_w), lambda i: (i, (q_w + 2 * kv_w) // iq_w)),
        pl.BlockSpec((tm, kv_w), lambda i: (i, (q_w + 2 * kv_w + iq_w) // kv_w)),
    ] + [pl.BlockSpec((tm, LANES), row)] * 6 + [pl.BlockSpec((1, LANES), const)] * 3
    out_shape = (
        jax.ShapeDtypeStruct((n, q_w), bf16),
        jax.ShapeDtypeStruct((n, kv_w), f32),
        jax.ShapeDtypeStruct((N_IDX_HEADS, n, IDX_DIM), bf16),
        jax.ShapeDtypeStruct((n, IDX_DIM), f32),
    )
    out_specs = (
        pl.BlockSpec((tm, q_w), row),
        pl.BlockSpec((tm, kv_w), row),
        pl.BlockSpec((N_IDX_HEADS, tm, IDX_DIM), lambda i: (0, i, 0)),
        pl.BlockSpec((tm, IDX_DIM), row),
    )
    return pl.pallas_call(
        _attn_post_body,
        grid=(n // tm,),
        in_specs=in_specs,
        out_specs=out_specs,
        out_shape=out_shape,
        compiler_params=_cparams(1),
        name="attn_post",
    )(proj, proj, proj, proj, *tabs, g_q.reshape(1, -1).astype(f32), g_k.reshape(1, -1).astype(f32), gki)


def _sortable(x):
    b = pltpu.bitcast(x, i32)
    return b ^ ((b >> 31) & 0x7FFFFFFF)


def _fold_rows(x, op):
    return op(x.reshape(x.shape[0] // SUBLANES, SUBLANES, x.shape[1]), axis=0)


def _select_bias(qi_refs, wi_ref, ki_seg_refs, keys_ref, bias_ref, *, seg_lens, tq, n_sel, causal, q_tile, blk,
                 n_blk, n_chunks, score_rows=256):
    n_seg = len(seg_lens)
    n_pack = LANES // tq
    offs = [sum(seg_lens[:s]) for s in range(n_seg)]
    wi_rows = [wi_ref[h:h + 1, :] for h in range(N_IDX_HEADS)]
    qi_rows = []
    for h in range(N_IDX_HEADS):
        for w in range(n_pack):
            qh = qi_refs[w][h]
            parts = [qh if v == w else jnp.zeros_like(qh) for v in range(n_pack)]
            qi_rows.append(jnp.concatenate(parts, axis=1) if n_pack > 1 else qh)
    qi_all = jnp.concatenate(qi_rows, axis=0)

    for s in range(n_seg):
        ki_refs = ki_seg_refs[s]
        lc = _pick(seg_lens[s], (256, 128, 64))
        n_sub = max(1, min(score_rows, seg_lens[s]) // lc)

        def score_chunk(c, carry, ki_refs=ki_refs, lc=lc, n_sub=n_sub, off=offs[s]):
            for h2 in range(n_sub):
                r0 = pl.multiple_of((c * n_sub + h2) * lc, lc)
                kic = [r[pl.ds(r0, lc), :].astype(bf16) for r in ki_refs]
                kic = jnp.concatenate(kic, axis=1) if n_pack > 1 else kic[0]
                sc = lax.dot_general(kic, qi_all, (((1,), (1,)), ((), ())), preferred_element_type=f32)
                acc = jnp.zeros((lc, LANES), f32)
                for h in range(N_IDX_HEADS):
                    acc = acc + jnp.maximum(sc[:, h * LANES:(h + 1) * LANES], 0.0) * wi_rows[h]
                key = _sortable(acc)
                if causal:
                    kpos = r0 + lax.broadcasted_iota(i32, (lc, LANES), 0)
                    qpos = q_tile * tq + lax.broadcasted_iota(i32, (lc, LANES), 1)
                    key = jnp.where((kpos // CHUNK) <= (qpos // CHUNK), key, INT_MIN)
                keys_ref[pl.ds(pl.multiple_of(off + r0, CHUNK), lc), :] = key
            return carry

        n_trip = n_chunks(s, lc * n_sub)
        lax.fori_loop(0, n_trip, score_chunk, 0, unroll=2 if isinstance(n_trip, int) and n_trip % 2 == 0 else 1)

    def count(pred):
        def body(i, acc):
            r0 = pl.multiple_of(i * blk, blk)
            hit = jnp.where(pred(keys_ref[pl.ds(r0, blk), :], r0), 1, 0)
            return acc + _fold_rows(hit, jnp.sum)
        acc = lax.fori_loop(0, n_blk, body, jnp.zeros((SUBLANES, LANES), i32), unroll=isinstance(n_blk, int))
        return jnp.sum(acc, axis=0, keepdims=True)

    def count_ge(cand):
        return count(lambda k, r0: k >= cand)

    ans = jnp.where(count_ge(jnp.zeros((1, LANES), i32)) >= n_sel, 0, INT_MIN)

    def bit_step(i, ans):
        cand = ans | (1 << (30 - i))
        return jnp.where(count_ge(cand) >= n_sel, cand, ans)

    ans = lax.fori_loop(0, 31, bit_step, ans)
    tau = jnp.maximum(ans, INT_MIN + 1)
    n_ge = count_ge(tau)
    ties = jnp.max(n_ge) > n_sel

    @pl.when(jnp.logical_not(ties))
    def _():
        def body(i, carry):
            r0 = pl.multiple_of(i * blk, blk)
            keep = keys_ref[pl.ds(r0, blk), :] >= tau
            bias_ref[pl.ds(r0, blk), :] = jnp.where(keep, 0.0, NEG_BIG).astype(bias_ref.dtype)
            return carry
        lax.fori_loop(0, n_blk, body, 0)

    @pl.when(ties)
    def _():
        def row_idx(r0):
            return r0 + lax.broadcasted_iota(i32, (blk, LANES), 0)
        need = n_sel - count(lambda k, r0: k > tau)

        def idx_step(i, m):
            cand = m | (1 << (14 - i))
            n_below = count(lambda k, r0: jnp.logical_and(k == tau, row_idx(r0) < cand))
            return jnp.where(n_below <= need, cand, m)

        m_cut = lax.fori_loop(0, 15, idx_step, jnp.zeros((1, LANES), i32))

        def body(i, carry):
            r0 = pl.multiple_of(i * blk, blk)
            k = keys_ref[pl.ds(r0, blk), :]
            keep = jnp.logical_or(k > tau, jnp.logical_and(k == tau, row_idx(r0) < m_cut))
            bias_ref[pl.ds(r0, blk), :] = jnp.where(keep, 0.0, NEG_BIG).astype(bias_ref.dtype)
            return carry
        lax.fori_loop(0, n_blk, body, 0)


def _augmented_queries(q_heads, tq, lane0):
    qg = jnp.concatenate(q_heads, axis=0)
    wq = qg.shape[0]
    bias_lane = lane0 + lax.broadcasted_iota(i32, (wq, LANES), 0) % tq
    one_hot = (lax.broadcasted_iota(i32, (wq, LANES), 1) == bias_lane).astype(bf16)
    return jnp.concatenate([qg, one_hot], axis=1)


def _write_heads(o_ref, acc, l_acc, col0, tq):
    o_t = (acc / jnp.sum(l_acc, axis=0, keepdims=True)).T
    for j in range(GQA_GROUP):
        c = col0 + j * HEAD_DIM
        o_ref[:, c:c + HEAD_DIM] = o_t[j * tq:(j + 1) * tq, :].astype(o_ref.dtype)


def _attn_past_body(*refs, segs, tq, n_sel):
    n_seg = len(segs)
    n_pack = LANES // tq
    seg_lens = [n for n, _ in segs]
    it = iter(refs)
    qi_refs = [next(it) for _ in range(n_pack)]
    wi_ref, q_ref = next(it), next(it)
    seg_refs = [([next(it) for _ in range(n_pack)], next(it), next(it)) for _ in range(n_seg)]
    o_ref = next(it)
    keys_ref, bias_ref, sc_ref, acc_ref = (next(it) for _ in range(4))

    l_tot = sum(seg_lens)
    offs = [sum(seg_lens[:s]) for s in range(n_seg)]
    wq = GQA_GROUP * tq
    in_pack = pl.program_id(1) // N_KV_HEADS
    g_head = pl.program_id(1) % N_KV_HEADS
    blk = _pick(l_tot, (512, 320, 256, 192, 128, 64))
    n_chunks = lambda s, lc: seg_lens[s] // lc

    @pl.when(pl.program_id(1) == 0)
    def _():
        _select_bias(qi_refs, wi_ref, [r[0] for r in seg_refs], keys_ref, bias_ref, seg_lens=seg_lens, tq=tq,
                     n_sel=n_sel, causal=False, q_tile=0, blk=blk, n_blk=l_tot // blk, n_chunks=n_chunks)

    q_aug = _augmented_queries([q_ref[:, j * HEAD_DIM:(j + 1) * HEAD_DIM] for j in range(GQA_GROUP)], tq, in_pack * tq)

    def read_kv(ref, interleaved, r0, lc):
        if interleaved:
            return ref[pl.ds(r0 * N_KV_HEADS + g_head, lc, stride=N_KV_HEADS), :].astype(bf16)
        return ref[pl.ds(r0, lc), :].astype(bf16)

    m_acc = jnp.full((SUBLANES, wq), NEG_BIG, f32)
    for s in range(n_seg):
        k_ref = seg_refs[s][1]
        lc = _pick(seg_lens[s], (1024, 512, 256, 128, 64))

        def logits_chunk(c, m_acc, k_ref=k_ref, lc=lc, off=offs[s], inter=segs[s][1]):
            r0 = pl.multiple_of(c * lc, lc)
            row = pl.multiple_of(off + r0, CHUNK)
            k_aug = jnp.concatenate([read_kv(k_ref, inter, r0, lc), bias_ref[pl.ds(row, lc), :]], axis=1)
            sc = lax.dot_general(k_aug, q_aug, (((1,), (1,)), ((), ())), preferred_element_type=f32)
            sc_ref[pl.ds(row, lc), :] = sc
            return jnp.maximum(m_acc, _fold_rows(sc, jnp.max))

        m_acc = lax.fori_loop(0, n_chunks(s, lc), logits_chunk, m_acc, unroll=True)
    m_row = jnp.max(m_acc, axis=0, keepdims=True)

    acc_ref[...] = jnp.zeros(acc_ref.shape, f32)
    l_acc = jnp.zeros((SUBLANES, wq), f32)
    for s in range(n_seg):
        v_ref = seg_refs[s][2]
        lc = _pick(seg_lens[s], (1024, 512, 256, 128, 64))

        def value_chunk(c, l_acc, v_ref=v_ref, lc=lc, off=offs[s], inter=segs[s][1]):
            r0 = pl.multiple_of(c * lc, lc)
            row = pl.multiple_of(off + r0, CHUNK)
            p = jnp.exp2((sc_ref[pl.ds(row, lc), :] - m_row) * EXP_SCALE)
            vc = read_kv(v_ref, inter, r0, lc)
            acc_ref[...] += lax.dot_general(vc, p.astype(bf16), (((0,), (0,)), ((), ())), preferred_element_type=f32)
            return l_acc + _fold_rows(p, jnp.sum)

        l_acc = lax.fori_loop(0, n_chunks(s, lc), value_chunk, l_acc, unroll=True)

    _write_heads(o_ref, acc_ref[...], l_acc, 0, tq)


def _attn_past(q, qi, wi_t, segs, *, n_batch, t_seq, n_sel):
    n = q.shape[0]
    tq = t_seq
    n_pack = LANES // tq
    assert n_batch % n_pack == 0 and LANES % tq == 0
    wq = GQA_GROUP * tq
    kv_w = GQA_GROUP * HEAD_DIM
    batch_of = lambda bp, e: bp * n_pack + e // N_KV_HEADS
    head_of = lambda e: e % N_KV_HEADS
    in_specs, args = [], []
    for w in range(n_pack):
        in_specs.append(pl.BlockSpec((N_IDX_HEADS, tq, IDX_DIM), lambda bp, e, w=w: (0, bp * n_pack + w, 0)))
        args.append(qi)
    in_specs += [
        pl.BlockSpec((None, N_IDX_HEADS, LANES), lambda bp, e: (bp, 0, 0)),
        pl.BlockSpec((tq, kv_w), lambda bp, e: (batch_of(bp, e), head_of(e))),
    ]
    args += [wi_t, q]
    seg_meta = []
    for ki, k, v, n_keys, imap, interleaved in segs:
        lead = (None,) * (ki.ndim - 2)
        kv_rows = n_keys * N_KV_HEADS if interleaved else n_keys
        kv_map = lambda bp, e, imap=imap, inter=interleaved: imap(batch_of(bp, e), 0 if inter else head_of(e))
        for w in range(n_pack):
            in_specs.append(pl.BlockSpec(lead + (n_keys, IDX_DIM), lambda bp, e, imap=imap, w=w: imap(bp * n_pack + w, 0)))
            args.append(ki)
        in_specs += [pl.BlockSpec(lead + (kv_rows, HEAD_DIM), kv_map)] * 2
        args += [k, v]
        seg_meta.append((n_keys, interleaved))
    l_tot = sum(m[0] for m in seg_meta)
    body = functools.partial(_attn_past_body, segs=tuple(seg_meta), tq=tq, n_sel=n_sel)
    return pl.pallas_call(
        body,
        grid=(n_batch // n_pack, n_pack * N_KV_HEADS),
        in_specs=in_specs,
        out_specs=pl.BlockSpec((tq, kv_w), lambda bp, e: (batch_of(bp, e), head_of(e))),
        out_shape=jax.ShapeDtypeStruct((n, N_HEADS * HEAD_DIM), bf16),
        scratch_shapes=[
            pltpu.VMEM((l_tot, LANES), i32),
            pltpu.VMEM((l_tot, LANES), bf16),
            pltpu.VMEM((l_tot, wq), f32),
            pltpu.VMEM((HEAD_DIM, wq), f32),
        ],
        compiler_params=_cparams(2),
        name="attn_past",
    )(*args)


def _attn_causal_body(qi_ref, wi_ref, q_ref, ki_ref, k_ref, v_ref, o_ref, keys_ref, bias_ref, sc_ref, acc_ref,
                      m_ref, l_ref, *, t_seq, n_sel):
    tq = LANES
    wq = GQA_GROUP * tq
    q_tile = pl.program_id(1)
    lc = _pick(t_seq, (1024, 512))
    n_lc = jnp.minimum(((q_tile + 1) * tq + lc - 1) // lc, t_seq // lc)
    blk = 512
    _select_bias([qi_ref], wi_ref, [[ki_ref]], keys_ref, bias_ref, seg_lens=[t_seq], tq=tq, n_sel=n_sel, causal=True,
                 q_tile=q_tile, blk=blk, n_blk=n_lc * (lc // blk), score_rows=512, n_chunks=lambda s, c: n_lc * (lc // c))

    def q_aug(g):
        heads = [q_ref[:, (g * GQA_GROUP + j) * HEAD_DIM:(g * GQA_GROUP + j + 1) * HEAD_DIM] for j in range(GQA_GROUP)]
        return _augmented_queries(heads, tq, 0)

    def fold_lanes(x, op):
        parts = [x[:, j * LANES:(j + 1) * LANES] for j in range(x.shape[1] // LANES)]
        while len(parts) > 1:
            parts = [op(parts[i], parts[i + 1]) for i in range(0, len(parts) - 1, 2)] + parts[len(parts) & ~1:]
        return parts[0]

    def logits(g, qa, c):
        r0 = pl.multiple_of(c * lc, lc)
        k_aug = jnp.concatenate([k_ref[pl.ds(r0, lc), g * HEAD_DIM:(g + 1) * HEAD_DIM], bias_ref[pl.ds(r0, lc), :]], axis=1)
        sc = lax.dot_general(qa, k_aug, (((1,), (1,)), ((), ())), preferred_element_type=f32)
        sc_ref[g % 2, :, pl.ds(r0, lc)] = sc
        m_ref[(g % 2) * wq:(g % 2 + 1) * wq, :] = jnp.maximum(m_ref[(g % 2) * wq:(g % 2 + 1) * wq, :], fold_lanes(sc, jnp.maximum))

    def values(g, m_all, c):
        r0 = pl.multiple_of(c * lc, lc)
        sc = sc_ref[g % 2, :, pl.ds(r0, lc)]
        p = jnp.exp2((sc - jnp.concatenate([m_all] * (lc // LANES), axis=1)) * EXP_SCALE)
        l_ref[...] += fold_lanes(p, jnp.add)
        vc = v_ref[pl.ds(r0, lc), g * HEAD_DIM:(g + 1) * HEAD_DIM]
        acc_ref[...] += jnp.dot(p.astype(bf16), vc, preferred_element_type=f32)

    def reset_max(g):
        m_ref[(g % 2) * wq:(g % 2 + 1) * wq, :] = jnp.full((wq, LANES), NEG_BIG, f32)

    def loop(body):
        lax.fori_loop(0, n_lc, lambda c, carry: (body(c), carry)[1], 0)

    reset_max(0)
    qa = q_aug(0)
    loop(lambda c, qa=qa: logits(0, qa, c))
    for g in range(N_KV_HEADS):
        m_part = m_ref[(g % 2) * wq:(g % 2 + 1) * wq, :]
        m_all = jnp.broadcast_to(jnp.max(m_part, axis=1, keepdims=True), (wq, LANES))
        acc_ref[...] = jnp.zeros(acc_ref.shape, f32)
        l_ref[...] = jnp.zeros(l_ref.shape, f32)
        if g + 1 < N_KV_HEADS:
            reset_max(g + 1)
            qa = q_aug(g + 1)

            def both(c, g=g, m_all=m_all, qa=qa):
                values(g, m_all, c)
                logits(g + 1, qa, c)

            loop(both)
        else:
            loop(lambda c, g=g, m_all=m_all: values(g, m_all, c))
        o = acc_ref[...] / jnp.sum(l_ref[...], axis=1, keepdims=True)
        for j in range(GQA_GROUP):
            col = (g * GQA_GROUP + j) * HEAD_DIM
            o_ref[:, col:col + HEAD_DIM] = o[j * tq:(j + 1) * tq, :].astype(o_ref.dtype)


def _attn_causal(q, qi, wi_t, ki, k, v, *, n_batch, t_seq, n_sel):
    n = q.shape[0]
    tq = LANES
    assert t_seq % 512 == 0
    nq = t_seq // tq
    q_w, kv_w = N_HEADS * HEAD_DIM, N_KV_HEADS * HEAD_DIM
    row = lambda b, i: (b * nq + i, 0)
    whole = lambda b, i: (b, 0, 0)
    body = functools.partial(_attn_causal_body, t_seq=t_seq, n_sel=n_sel)
    return pl.pallas_call(
        body,
        grid=(n_batch, nq),
        in_specs=[
            pl.BlockSpec((N_IDX_HEADS, tq, IDX_DIM), lambda b, i: (0, b * nq + i, 0)),
            pl.BlockSpec((None, N_IDX_HEADS, LANES), lambda b, i: (b, 0, i)),
            pl.BlockSpec((tq, q_w), row),
            pl.BlockSpec((None, t_seq, IDX_DIM), whole, pipeline_mode=pl.Buffered(1)),
            pl.BlockSpec((None, t_seq, kv_w), whole, pipeline_mode=pl.Buffered(1)),
            pl.BlockSpec((None, t_seq, kv_w), whole, pipeline_mode=pl.Buffered(1)),
        ],
        out_specs=pl.BlockSpec((tq, q_w), row),
        out_shape=jax.ShapeDtypeStruct((n, q_w), bf16),
        scratch_shapes=[
            pltpu.VMEM((t_seq, LANES), i32),
            pltpu.VMEM((t_seq, LANES), bf16),
            pltpu.VMEM((2, GQA_GROUP * tq, t_seq), f32),
            pltpu.VMEM((GQA_GROUP * tq, HEAD_DIM), f32),
            pltpu.VMEM((2 * GQA_GROUP * tq, LANES), f32),
            pltpu.VMEM((GQA_GROUP * tq, LANES), f32),
        ],
        compiler_params=_cparams(2),
        name="attn_causal",
    )(qi, wi_t, q, ki, k, v)


def _attn_group(q, qi, wi, ki_new, k_new, v_new, past, *, n_batch, t_seq, offset):
    kv_w = N_KV_HEADS * HEAD_DIM
    n_pack = max(1, LANES // t_seq)
    wi_g = wi.reshape(n_batch // n_pack, n_pack, t_seq, N_IDX_HEADS).transpose(0, 3, 1, 2)
    wi_g = wi_g.reshape(n_batch // n_pack, N_IDX_HEADS, n_pack * t_seq)
    ki_g = ki_new.reshape(n_batch, t_seq, IDX_DIM)
    n_sel = min(TOPK_MAX, (offset + t_seq) // 4)

    if past is None:
        k_g = k_new.astype(bf16).reshape(n_batch, t_seq, kv_w)
        v_g = v_new.astype(bf16).reshape(n_batch, t_seq, kv_w)
        return _attn_causal(q, qi, wi_g, ki_g, k_g, v_g, n_batch=n_batch, t_seq=t_seq, n_sel=n_sel)

    assert offset % CHUNK == 0 and t_seq <= CHUNK
    k_g = k_new.reshape(n_batch, t_seq, kv_w)
    v_g = v_new.reshape(n_batch, t_seq, kv_w)
    pk, pv, pki, r = past
    n_past = pk.shape[2]
    pk = pk.reshape(pk.shape[0], n_batch, n_past * N_KV_HEADS, HEAD_DIM)
    pv = pv.reshape(pv.shape[0], n_batch, n_past * N_KV_HEADS, HEAD_DIM)
    past_map = lambda b, hb: (r, b, 0, 0)
    new_map = lambda b, hb: (b, 0, hb)
    segs = [(pki, pk, pv, n_past, past_map, True), (ki_g, k_g, v_g, t_seq, new_map, False)]
    return _attn_past(q, qi, wi_g, segs, n_batch=n_batch, t_seq=t_seq, n_sel=n_sel)


def _attn_layer(x, group, past, gain, w_in_p, g_q, g_k, g_ki, w_o):
    q_w, kv_w, iq_w = N_HEADS * HEAD_DIM, N_KV_HEADS * HEAD_DIM, N_IDX_HEADS * IDX_DIM
    proj = _matmul(x, [w_in_p], gain=gain, name="attn_in")
    pos = group.offset + jnp.tile(jnp.arange(group.t_seq), group.n_batch)
    q, k, qi, ki = _attn_post(proj, pos, g_q, g_k, g_ki)
    v = proj[:, q_w + kv_w:q_w + 2 * kv_w]
    wi = proj[:, q_w + 2 * kv_w + iq_w + IDX_DIM:q_w + 2 * kv_w + iq_w + IDX_DIM + N_IDX_HEADS]
    o = _attn_group(q, qi, wi, ki, k, v, past, n_batch=group.n_batch, t_seq=group.t_seq, offset=group.offset)
    x = _matmul(o, [w_o], residual=x, name="attn_out")
    return x, k, v, ki


def _halo_rows(x_ref, init, j, *, n_str, n_steps, chained):
    if not chained:
        return init
    prev = x_ref[(n_steps - j) * n_str:(n_steps - j + 1) * n_str, :]
    shifted = pltpu.roll(prev, 1, axis=0)
    first = lax.broadcasted_iota(i32, shifted.shape, 0) == 0
    return jnp.where(first, init, shifted)


def _chain_carries(h_end, p_end, c0):
    rows = [c0]
    for s in range(1, h_end.shape[0]):
        rows.append(p_end[s - 1:s] * rows[-1] + h_end[s - 1:s])
    return jnp.concatenate(rows, axis=0)


def _lru_core_body(xb_ref, gate_ref, cw_ref, cb_ref, wa_ref, ba_ref, wx_ref, bx_ref, lam_ref, buf_ref, h0_ref,
                   y_ref, hl_ref, a_ref, b_ref, *, n_str, n_steps, chained, lw):
    rows = n_str * n_steps
    ch = _pick(n_steps, (16, 8, 4, 2, 1)) * n_str
    n_halo = LRU_CONV - 1
    cw = [cw_ref[j:j + 1, :] for j in range(LRU_CONV)]
    cb = cb_ref[...]
    sp = jax.nn.softplus(-lam_ref[...])
    halo = jnp.concatenate(
        [_halo_rows(xb_ref, buf_ref[n_halo - j], j, n_str=n_str, n_steps=n_steps, chained=chained)
         for j in range(n_halo, 0, -1)], axis=0)

    def gates(xin, r0):
        xc = cb
        for j in range(LRU_CONV):
            xc = xc + xin[j * n_str:j * n_str + ch, :] * cw[j]
        xcb = xc.astype(bf16)
        for t in range(lw // LRU_BLOCK):
            sl = slice(t * LRU_BLOCK, (t + 1) * LRU_BLOCK)
            xt = xcb[:, sl]
            r = _sigmoid(jnp.dot(xt, wa_ref[t], preferred_element_type=f32) + ba_ref[:, sl])
            ig = _sigmoid(jnp.dot(xt, wx_ref[t], preferred_element_type=f32) + bx_ref[:, sl])
            log_a = (-LRU_C) * r * sp[:, sl]
            th = jnp.tanh(log_a)
            mult = jnp.sqrt(-2.0 * th / (1.0 - th))
            a_ref[pl.ds(r0, ch), sl] = jnp.exp(log_a)
            b_ref[pl.ds(r0, ch), sl] = mult * (ig * xc[:, sl])

    gates(jnp.concatenate([halo, xb_ref[0:ch, :]], axis=0), 0)

    def gate_chunk(c, carry):
        r0 = pl.multiple_of(c * ch, ch)
        gates(xb_ref[pl.ds(r0 - n_halo * n_str, ch + n_halo * n_str), :], r0)
        return carry

    lax.fori_loop(1, rows // ch, gate_chunk, 0, unroll=2)

    def scan_step(t, carry):
        r0 = pl.multiple_of(t * n_str, n_str)
        a = a_ref[pl.ds(r0, n_str), :]
        h = a * carry[0] + b_ref[pl.ds(r0, n_str), :]
        b_ref[pl.ds(r0, n_str), :] = h
        if not chained:
            return (h,)
        p = a * carry[1]
        a_ref[pl.ds(r0, n_str), :] = p
        return (h, p)

    if chained:
        init = (jnp.zeros((n_str, lw), f32), jnp.ones((n_str, lw), f32))
    else:
        init = (h0_ref[...],)
    fin = lax.fori_loop(0, n_steps, scan_step, init, unroll=8)

    if chained:
        carries = _chain_carries(fin[0], fin[1], h0_ref[...])
        c_rep = jnp.concatenate([carries] * (ch // n_str), axis=0)

    def out_chunk(c, carry):
        r0 = pl.multiple_of(c * ch, ch)
        h = b_ref[pl.ds(r0, ch), :]
        if chained:
            h = h + a_ref[pl.ds(r0, ch), :] * c_rep
            b_ref[pl.ds(r0, ch), :] = h
        y_ref[pl.ds(r0, ch), :] = (h * jax.nn.gelu(gate_ref[pl.ds(r0, ch), :])).astype(y_ref.dtype)
        return carry

    lax.fori_loop(0, rows // ch, out_chunk, 0)
    if chained:
        hl_ref[...] = b_ref[rows - 1:rows, :]
    else:
        hl_ref[...] = fin[0]


def _lru_core(proj, params, buf, h0, *, n_seq, n_str, n_steps, chained):
    w = proj.shape[1] // 2
    lw = 256
    rows = n_str * n_steps
    ns = 1 if chained else n_str
    cw, cb, wa, ba, wx, bx, lam = params
    nb = lw // LRU_BLOCK
    vec = lambda s, n: (0, n)
    in_specs = [
        pl.BlockSpec((rows, lw), lambda s, n: (s, w // lw + n)),
        pl.BlockSpec((rows, lw), lambda s, n: (s, n)),
        pl.BlockSpec((LRU_CONV, lw), vec),
        pl.BlockSpec((1, lw), vec),
        pl.BlockSpec((nb, LRU_BLOCK, LRU_BLOCK), lambda s, n: (n, 0, 0)),
        pl.BlockSpec((1, lw), vec),
        pl.BlockSpec((nb, LRU_BLOCK, LRU_BLOCK), lambda s, n: (n, 0, 0)),
        pl.BlockSpec((1, lw), vec),
        pl.BlockSpec((1, lw), vec),
        pl.BlockSpec((None, LRU_CONV - 1, ns, lw), lambda s, n: (s, 0, 0, n)),
        pl.BlockSpec((None, ns, lw), lambda s, n: (s, 0, n)),
    ]
    body = functools.partial(_lru_core_body, n_str=n_str, n_steps=n_steps, chained=chained, lw=lw)
    return pl.pallas_call(
        body,
        grid=(n_seq, w // lw),
        in_specs=in_specs,
        out_specs=(
            pl.BlockSpec((rows, lw), lambda s, n: (s, n)),
            pl.BlockSpec((None, ns, lw), lambda s, n: (s, 0, n)),
        ),
        out_shape=(
            jax.ShapeDtypeStruct((n_seq * rows, w), bf16),
            jax.ShapeDtypeStruct((n_seq, ns, w), f32),
        ),
        scratch_shapes=[pltpu.VMEM((rows, lw), f32), pltpu.VMEM((rows, lw), f32)],
        compiler_params=_cparams(2),
        name="lru_core",
    )(proj, proj, cw, cb.reshape(1, w), wa, ba.reshape(1, w), wx, bx.reshape(1, w), lam.reshape(1, w), buf, h0)


def _s5_core_body(u_ref, bm_ref, cm_ref, are_ref, aim_ref, d_ref, sre_ref, sim_ref,
                  y_ref, ore_ref, oim_ref, hs_ref, p_ref, *, n_str, n_steps, chained):
    rows = n_str * n_steps
    half = hs_ref.shape[1] // 2
    spc = _pick(n_steps, tuple(s for s in (64, 32, 16, 8, 4, 2, 1) if s * n_str <= 512))
    ch = spc * n_str

    def proj_chunk(c, carry):
        r0 = pl.multiple_of(c * ch, ch)
        hs_ref[pl.ds(r0, ch), :] = jnp.dot(u_ref[pl.ds(r0, ch), :].astype(bf16), bm_ref[...], preferred_element_type=f32)
        return carry

    lax.fori_loop(0, rows // ch, proj_chunk, 0, unroll=2)

    a_re = jnp.broadcast_to(are_ref[...], (n_str, half))
    a_im = jnp.broadcast_to(aim_ref[...], (n_str, half))

    def scan_step(t, carry):
        r0 = pl.multiple_of(t * n_str, n_str)
        h_re, h_im = carry[0], carry[1]
        n_re = a_re * h_re - a_im * h_im + hs_ref[pl.ds(r0, n_str), :half]
        n_im = a_re * h_im + a_im * h_re + hs_ref[pl.ds(r0, n_str), half:]
        hs_ref[pl.ds(r0, n_str), :half] = n_re
        hs_ref[pl.ds(r0, n_str), half:] = n_im
        if not chained:
            return (n_re, n_im)
        p_re, p_im = carry[2], carry[3]
        q_re = a_re[:1] * p_re - a_im[:1] * p_im
        q_im = a_re[:1] * p_im + a_im[:1] * p_re
        p_ref[pl.ds(t, 1), :half] = q_re
        p_ref[pl.ds(t, 1), half:] = q_im
        return (n_re, n_im, q_re, q_im)

    if chained:
        zero = jnp.zeros((n_str, half), f32)
        init = (zero, zero, jnp.ones((1, half), f32), jnp.zeros((1, half), f32))
    else:
        init = (sre_ref[...], sim_ref[...])
    fin = lax.fori_loop(0, n_steps, scan_step, init, unroll=2)

    if chained:
        pe_re, pe_im = fin[2], fin[3]
        c_re, c_im = [sre_ref[...]], [sim_ref[...]]
        for s in range(1, n_str):
            prev_re, prev_im = c_re[-1], c_im[-1]
            c_re.append(pe_re * prev_re - pe_im * prev_im + fin[0][s - 1:s])
            c_im.append(pe_re * prev_im + pe_im * prev_re + fin[1][s - 1:s])
        c_re = jnp.concatenate(c_re, axis=0)
        c_im = jnp.concatenate(c_im, axis=0)
        ore_ref[...] = fin[0][n_str - 1:] + pe_re * c_re[n_str - 1:] - pe_im * c_im[n_str - 1:]
        oim_ref[...] = fin[1][n_str - 1:] + pe_re * c_im[n_str - 1:] + pe_im * c_re[n_str - 1:]
        c_re = jnp.concatenate([c_re] * spc, axis=0)
        c_im = jnp.concatenate([c_im] * spc, axis=0)
    else:
        ore_ref[...] = fin[0]
        oim_ref[...] = fin[1]

    def per_stream(p):
        return jnp.broadcast_to(p[:, None, :], (spc, n_str, p.shape[1])).reshape(ch, p.shape[1])

    def out_chunk(c, carry):
        r0 = pl.multiple_of(c * ch, ch)
        u = u_ref[pl.ds(r0, ch), :]
        hs = hs_ref[pl.ds(r0, ch), :]
        if chained:
            t0 = pl.multiple_of(c * spc, spc)
            p_re = per_stream(p_ref[pl.ds(t0, spc), :half])
            p_im = per_stream(p_ref[pl.ds(t0, spc), half:])
            hs = jnp.concatenate([hs[:, :half] + p_re * c_re - p_im * c_im,
                                  hs[:, half:] + p_re * c_im + p_im * c_re], axis=1)
        y = jnp.dot(hs.astype(bf16), cm_ref[...], preferred_element_type=f32) + d_ref[...] * u
        y_ref[pl.ds(r0, ch), :] = jax.nn.gelu(y).astype(y_ref.dtype)
        return carry

    lax.fori_loop(0, rows // ch, out_chunk, 0)


def _s5_core(u, mats, s_re, s_im, *, n_seq, n_str, n_steps, chained):
    d = u.shape[1]
    bm, cm, a_re, a_im, dsk = mats
    n_gb = d // LANES
    sw = bm.shape[2] // 2
    rows = n_str * n_steps
    ns = 1 if chained else n_str
    st_spec = pl.BlockSpec((None, ns, sw), lambda s, g: (s, 0, g))
    in_specs = [
        pl.BlockSpec((rows, LANES), lambda s, g: (s, g)),
        pl.BlockSpec((None, LANES, 2 * sw), lambda s, g: (g, 0, 0)),
        pl.BlockSpec((None, 2 * sw, LANES), lambda s, g: (g, 0, 0)),
        pl.BlockSpec((1, sw), lambda s, g: (0, g)),
        pl.BlockSpec((1, sw), lambda s, g: (0, g)),
        pl.BlockSpec((1, LANES), lambda s, g: (0, g)),
        st_spec,
        st_spec,
    ]
    body = functools.partial(_s5_core_body, n_str=n_str, n_steps=n_steps, chained=chained)
    st_shape = jax.ShapeDtypeStruct((n_seq, ns, n_gb * sw), f32)
    return pl.pallas_call(
        body,
        grid=(n_seq, n_gb),
        in_specs=in_specs,
        out_specs=(pl.BlockSpec((rows, LANES), lambda s, g: (s, g)), st_spec, st_spec),
        out_shape=(jax.ShapeDtypeStruct((n_seq * rows, d), bf16), st_shape, st_shape),
        scratch_shapes=[pltpu.VMEM((rows, 2 * sw), f32), pltpu.VMEM((n_steps if chained else SUBLANES, 2 * sw), f32)],
        compiler_params=_cparams(2),
        name="s5_core",
    )(u, bm, cm, a_re, a_im, dsk, s_re, s_im)


def _s5_matrices(a_re, a_im, log_dt, b_re, b_im, c_re, c_im, d_skip):
    g, p = a_re.shape
    a = lax.complex(a_re.astype(f32), a_im.astype(f32))
    step = jnp.exp(log_dt.astype(f32))[:, None]
    a_bar = jnp.exp(a * step)
    b_bar = ((a_bar - 1.0) / a)[:, :, None] * lax.complex(b_re.astype(f32), b_im.astype(f32))
    gpb = LANES // S5_GROUP
    n_gb = g // gpb
    eye = jnp.eye(gpb, dtype=f32)

    def in_block(m):
        m = m.reshape(n_gb, gpb, p, S5_GROUP)
        return jnp.einsum('ngpc,gh->ngchp', m, eye).reshape(n_gb, gpb * S5_GROUP, gpb * p)

    def out_block(m):
        m = m.reshape(n_gb, gpb, S5_GROUP, p)
        return jnp.einsum('ngcp,gh->ngphc', m, eye).reshape(n_gb, gpb * p, gpb * S5_GROUP)

    bm = jnp.concatenate([in_block(jnp.real(b_bar)), in_block(jnp.imag(b_bar))], axis=2).astype(bf16)
    cm = jnp.concatenate([out_block(c_re.astype(f32)), -out_block(c_im.astype(f32))], axis=1).astype(bf16)
    return bm, cm, jnp.real(a_bar).reshape(1, g * p), jnp.imag(a_bar).reshape(1, g * p), d_skip.reshape(1, -1).astype(f32)


def _sconv_body(bg_ref, cg_ref, v_ref, cw_ref, buf_ref, u_ref, nb_ref, cv_ref, *, n_str, n_steps, chained):
    rows = n_str * n_steps
    ch = _pick(n_steps, (16, 8, 4, 2, 1)) * n_str
    n_halo = SCONV_W - 1
    cw = [cw_ref[j:j + 1, :] for j in range(SCONV_W)]

    def prod_chunk(c, carry):
        r0 = pl.multiple_of(c * ch, ch)
        cv_ref[pl.ds(r0, ch), :] = cg_ref[pl.ds(r0, ch), :] * v_ref[pl.ds(r0, ch), :]
        return carry

    lax.fori_loop(0, rows // ch, prod_chunk, 0)
    halo = jnp.concatenate(
        [_halo_rows(cv_ref, buf_ref[n_halo - j], j, n_str=n_str, n_steps=n_steps, chained=chained)
         for j in range(n_halo, 0, -1)], axis=0)

    def emit(xin, r0):
        z = xin[0:ch, :] * cw[0]
        for j in range(1, SCONV_W):
            z = z + xin[j * n_str:j * n_str + ch, :] * cw[j]
        u_ref[pl.ds(r0, ch), :] = (bg_ref[pl.ds(r0, ch), :] * z).astype(u_ref.dtype)

    emit(jnp.concatenate([halo, cv_ref[0:ch, :]], axis=0), 0)

    def out_chunk(c, carry):
        r0 = pl.multiple_of(c * ch, ch)
        emit(cv_ref[pl.ds(r0 - n_halo * n_str, ch + n_halo * n_str), :], r0)
        return carry

    lax.fori_loop(1, rows // ch, out_chunk, 0)
    for j in range(n_halo):
        step_rows = cv_ref[(n_steps - n_halo + j) * n_str:(n_steps - n_halo + j + 1) * n_str, :]
        nb_ref[j] = step_rows[n_str - 1:, :] if chained else step_rows


def _sconv_core(proj, conv_w, buf, *, n_seq, n_str, n_steps, chained):
    d = proj.shape[1] // 3
    lw = 256
    rows = n_str * n_steps
    ns = 1 if chained else n_str
    nl = d // lw
    body = functools.partial(_sconv_body, n_str=n_str, n_steps=n_steps, chained=chained)
    buf_spec = pl.BlockSpec((None, SCONV_W - 1, ns, lw), lambda s, n: (s, 0, 0, n))
    return pl.pallas_call(
        body,
        grid=(n_seq, nl),
        in_specs=[
            pl.BlockSpec((rows, lw), lambda s, n: (s, n)),
            pl.BlockSpec((rows, lw), lambda s, n: (s, nl + n)),
            pl.BlockSpec((rows, lw), lambda s, n: (s, 2 * nl + n)),
            pl.BlockSpec((SCONV_W, lw), lambda s, n: (0, n)),
            buf_spec,
        ],
        out_specs=(pl.BlockSpec((rows, lw), lambda s, n: (s, n)), buf_spec),
        out_shape=(
            jax.ShapeDtypeStruct((n_seq * rows, d), bf16),
            jax.ShapeDtypeStruct((n_seq, SCONV_W - 1, ns, d), f32),
        ),
        scratch_shapes=[pltpu.VMEM((rows, lw), f32)],
        compiler_params=_cparams(2),
        name="sconv_core",
    )(proj, proj, proj, conv_w, buf)


class _Group:
    def __init__(self, n_batch, t_seq, offset, has_past):
        self.n_batch, self.t_seq, self.offset, self.has_past = n_batch, t_seq, offset, has_past
        self.chained = not has_past
        if self.chained:
            self.n_seq, self.n_str, self.n_steps = n_batch, PROMPT_STREAMS, t_seq // PROMPT_STREAMS
        else:
            self.n_seq, self.n_str, self.n_steps = 1, n_batch, t_seq
        self.n_rows = n_batch * t_seq
        self.dims = dict(n_seq=self.n_seq, n_str=self.n_str, n_steps=self.n_steps, chained=self.chained)

    def to_streams(self, x):
        f = x.shape[-1]
        if self.chained:
            return x.reshape(self.n_batch, self.n_str, self.n_steps, f).transpose(0, 2, 1, 3).reshape(self.n_rows, f)
        return x.reshape(self.n_batch, self.t_seq, f).transpose(1, 0, 2).reshape(self.n_rows, f)

    def to_natural(self, x):
        f = x.shape[-1]
        if self.chained:
            return x.reshape(self.n_batch, self.n_steps, self.n_str, f).transpose(0, 2, 1, 3).reshape(self.n_rows, f)
        return x.reshape(self.t_seq, self.n_batch, f).transpose(1, 0, 2).reshape(self.n_rows, f)

    def last_steps(self, x, k, c0, c1):
        if self.chained:
            rows = [x[b * self.t_seq + (self.n_steps - k + j) * self.n_str + self.n_str - 1, c0:c1]
                    for b in range(self.n_batch) for j in range(k)]
            return jnp.stack(rows).reshape(self.n_batch, k, c1 - c0)
        r0 = (self.t_seq - k) * self.n_batch
        return x[r0:, c0:c1].reshape(k, self.n_batch, c1 - c0).transpose(1, 0, 2)

    def state_in(self, s):
        if s.ndim == 2:
            return s.astype(f32)[None]
        return s.astype(f32).transpose(1, 0, 2)[None]

    def zeros_state(self, k, width):
        shape = (self.n_seq, 1, width) if k is None else (self.n_seq, k, 1, width)
        return jnp.zeros(shape, f32)

    def state_out(self, s):
        return s.reshape(self.n_batch, -1)


def kernel(x_prompt, x_sample, cache_attn_k, cache_attn_v, cache_idx_k, state_lru_conv, state_lru_h, state_s5_re, state_s5_im, state_sconv, norm_mix, norm_ffn, attn_w_in, attn_g_q, attn_g_k, attn_g_ki, attn_w_o, lru_w_in, lru_conv_w, lru_conv_b, lru_w_a, lru_b_a, lru_w_x, lru_b_x, lru_lambda, lru_w_out, s5_a_re, s5_a_im, s5_log_dt, s5_b_re, s5_b_im, s5_c_re, s5_c_im, s5_d, s5_w_glu, sc_w_in, sc_conv_w, sc_w_out, ffn_w_up, ffn_w_down):
    bp, tp, d = x_prompt.shape
    bs, ts, _ = x_sample.shape
    n_past = cache_attn_k.shape[2]
    depth = norm_mix.shape[0]
    n_mix = 4
    groups = [_Group(bp, tp, 0, False), _Group(bs, ts, n_past, True)]
    xs = [x_prompt.reshape(bp * tp, d).astype(f32), x_sample.reshape(bs * ts, d).astype(f32)]
    layouts = ["natural", "natural"]
    outs = [[[] for _ in range(8)] for _ in groups]
    w_up_bf, w_down_bf = ffn_w_up.astype(bf16), ffn_w_down.astype(bf16)

    def wanted(i):
        return "natural" if i >= depth or i % n_mix == 0 else "streams"

    for i in range(depth):
        m, r = i % n_mix, i // n_mix
        for gi, g in enumerate(groups):
            if layouts[gi] != wanted(i):
                xs[gi] = g.to_natural(xs[gi]) if wanted(i) == "natural" else g.to_streams(xs[gi])
                layouts[gi] = wanted(i)
        if m == 0:
            n_in = attn_w_in.shape[2]
            n_pad = -(-n_in // 768) * 768
            w_in_p = jnp.pad(attn_w_in[r], ((0, 0), (0, n_pad - n_in))).astype(bf16)
            w_o = attn_w_o[r].astype(bf16)
            pasts = [None, (cache_attn_k, cache_attn_v, cache_idx_k, r)]
            for gi, g in enumerate(groups):
                xs[gi], k, v, ki = _attn_layer(xs[gi], g, pasts[gi], norm_mix[i], w_in_p, attn_g_q[r], attn_g_k[r],
                                               attn_g_ki[r], w_o)
                outs[gi][0].append(k.reshape(g.n_batch, g.t_seq, N_KV_HEADS, HEAD_DIM))
                outs[gi][1].append(v.reshape(g.n_batch, g.t_seq, N_KV_HEADS, HEAD_DIM))
                outs[gi][2].append(ki.reshape(g.n_batch, g.t_seq, IDX_DIM))
        elif m == 1:
            w = lru_w_in.shape[2] // 2
            w_in, w_out = lru_w_in[r].astype(bf16), lru_w_out[r].astype(bf16)
            params = (lru_conv_w[r], lru_conv_b[r], lru_w_a[r].astype(bf16), lru_b_a[r], lru_w_x[r].astype(bf16),
                      lru_b_x[r], lru_lambda[r])
            for gi, g in enumerate(groups):
                proj = _matmul(xs[gi], [w_in], gain=norm_mix[i], name="lru_in")
                if g.has_past:
                    buf, h0 = g.state_in(state_lru_conv[r]), g.state_in(state_lru_h[r])
                else:
                    buf, h0 = g.zeros_state(LRU_CONV - 1, w), g.zeros_state(None, w)
                y, hl = _lru_core(proj, params, buf, h0, **g.dims)
                outs[gi][3].append(g.last_steps(proj, LRU_CONV - 1, w, 2 * w))
                outs[gi][4].append(g.state_out(hl))
                xs[gi] = _matmul(y, [w_out], residual=xs[gi], name="lru_out")
        elif m == 2:
            mats = _s5_matrices(s5_a_re[r], s5_a_im[r], s5_log_dt[r], s5_b_re[r], s5_b_im[r], s5_c_re[r],
                                s5_c_im[r], s5_d[r])
            n_grp, n_state = s5_a_re.shape[1], s5_a_re.shape[2]
            gp = n_grp * n_state
            wg = s5_w_glu[r].astype(bf16)
            wg = [wg[:, :d], wg[:, d:]]
            for gi, g in enumerate(groups):
                u = _norm(xs[gi], norm_mix[i])
                if g.has_past:
                    s_re = g.state_in(state_s5_re[r].reshape(g.n_batch, gp))
                    s_im = g.state_in(state_s5_im[r].reshape(g.n_batch, gp))
                else:
                    s_re, s_im = g.zeros_state(None, gp), g.zeros_state(None, gp)
                y, o_re, o_im = _s5_core(u, mats, s_re, s_im, **g.dims)
                outs[gi][5].append(g.state_out(o_re).reshape(g.n_batch, n_grp, n_state))
                outs[gi][6].append(g.state_out(o_im).reshape(g.n_batch, n_grp, n_state))
                xs[gi] = _matmul(y, wg, residual=xs[gi], glu=True, name="s5_glu")
        else:
            w_in, w_out = sc_w_in[r].astype(bf16), sc_w_out[r].astype(bf16)
            for gi, g in enumerate(groups):
                proj = _matmul(xs[gi], [w_in], gain=norm_mix[i], name="sconv_in")
                buf = g.state_in(state_sconv[r]) if g.has_past else g.zeros_state(SCONV_W - 1, d)
                u, nb = _sconv_core(proj, sc_conv_w[r], buf, **g.dims)
                outs[gi][7].append(nb.transpose(0, 2, 1, 3).reshape(g.n_batch, SCONV_W - 1, d))
                xs[gi] = _matmul(u, [w_out], residual=xs[gi], name="sconv_out")
        for gi, g in enumerate(groups):
            relayout = None
            if g.chained and layouts[gi] != wanted(i + 1):
                relayout = "to_" + wanted(i + 1)
                layouts[gi] = wanted(i + 1)
            xs[gi] = _ffn(xs[gi], norm_ffn[i], w_up_bf, w_down_bf, i, streams=(g.n_str, g.n_steps), relayout=relayout)

    for gi, g in enumerate(groups):
        if layouts[gi] != "natural":
            xs[gi] = g.to_natural(xs[gi])
    y_prompt = xs[0].reshape(bp, tp, d)
    y_sample = xs[1].reshape(bs, ts, d)
    st = [[jnp.stack(l) for l in outs[gi]] for gi in range(2)]
    return (y_prompt, y_sample, *st[0], *st[1])
```
